```python
import jax, jax.numpy as jnp
from jax import lax
import numpy as np

D_MODEL = 1024
BATCH = 32
SEQ = 2048
DEPTH = 4

HEAD_DIM = 64
MIX_WIDTH = D_MODEL
SB_WIDTH = MIX_WIDTH // 2
SB_HEADS = SB_WIDTH // HEAD_DIM
SWA_WIDTH = MIX_WIDTH - SB_WIDTH
SWA_Q_HEADS = SWA_WIDTH // HEAD_DIM
SWA_GROUP = 4
SWA_KV_HEADS = SWA_Q_HEADS // SWA_GROUP
SWA_KV_WIDTH = SWA_KV_HEADS * HEAD_DIM
IN_COLS = 3 * SB_WIDTH + SWA_WIDTH + 2 * SWA_KV_WIDTH
WINDOW = 128
QBLK = 128
N_BUCKETS = 32
MAX_DISTANCE = 128
N_MEM = 256
MEM_HEADS = 4
MEM_HEAD_DIM = 128
MEM_WIDTH = MEM_HEADS * MEM_HEAD_DIM
D_FF = (7 * D_MODEL) // 2
N_EXPERTS = 8
TOP_K = 2
MOE_BLK = 256
EPS = 1e-6
N_DENSE = (DEPTH + 1) // 2
N_MOE = DEPTH // 2

kernel_name = "hymba_stickbreak_swa_sink_memxattn_moe"


def rms_norm(x, g):
    xf = x.astype(jnp.float32)
    y = xf * lax.rsqrt(jnp.mean(xf * xf, axis=-1, keepdims=True) + EPS)
    return (y * g.astype(jnp.float32)).astype(x.dtype)


def t5_buckets(dist):
    n = np.maximum(dist, 0)
    max_exact = N_BUCKETS // 2
    large = max_exact + (np.log(np.maximum(n, 1) / max_exact) / np.log(MAX_DISTANCE / max_exact)
                         * (N_BUCKETS - max_exact)).astype(np.int32)
    large = np.minimum(large, N_BUCKETS - 1)
    return np.where(n < max_exact, n, large).astype(np.int32)


def stick_breaking_attention(q, k, v):
    S, d = q.shape[2], q.shape[3]
    scale = d ** -0.5
    outs = []
    for i in range(S // QBLK):
        t0, t1 = i * QBLK, (i + 1) * QBLK
        qb, kp, vp = q[:, :, t0:t1], k[:, :, :t1], v[:, :, :t1]
        z = jnp.einsum('bhqd,bhkd->bhqk', qb, kp).astype(jnp.float32) * scale
        strict = (jnp.arange(t1)[None, :] < (t0 + jnp.arange(QBLK))[:, None])
        log_beta = jax.nn.log_sigmoid(z)
        log_one_minus = jnp.where(strict, jax.nn.log_sigmoid(-z), 0.0)
        tail = lax.cumsum(log_one_minus, axis=3, reverse=True) - log_one_minus
        a = jnp.where(strict, jnp.exp(log_beta + tail), 0.0)
        outs.append(jnp.einsum('bhqk,bhkd->bhqd', a.astype(v.dtype), vp))
    return jnp.concatenate(outs, axis=2)


def sliding_window_sink_attention(q, k, v, sinks, bias):
    B, S, _, d = q.shape
    nb = S // WINDOW
    qb = q.reshape(B, nb, WINDOW, SWA_KV_HEADS, SWA_GROUP, d)

    def band(t):
        tb = t.reshape(B, nb, WINDOW, SWA_KV_HEADS, d)
        prev = jnp.pad(tb, ((0, 0), (1, 0), (0, 0), (0, 0), (0, 0)))[:, :-1]
        return jnp.concatenate([prev, tb], axis=2)

    kw, vw = band(k), band(v)
    s = jnp.einsum('bnqhgd,bnkhd->bnhgqk', qb, kw).astype(jnp.float32) * (d ** -0.5)
    s = s + bias.astype(jnp.float32).reshape(SWA_KV_HEADS, SWA_GROUP, WINDOW, 2 * WINDOW)
    blk = jnp.arange(nb)[:, None, None]
    tpos = blk * WINDOW + jnp.arange(WINDOW)[None, :, None]
    spos = (blk - 1) * WINDOW + jnp.arange(2 * WINDOW)[None, None, :]
    valid = (spos <= tpos) & (tpos - spos < WINDOW) & (spos >= 0)
    s = jnp.where(valid[None, :, None, None], s, -jnp.inf)
    sink = sinks.astype(jnp.float32).reshape(1, 1, SWA_KV_HEADS, SWA_GROUP, 1, 1)
    m = jnp.maximum(jnp.max(s, axis=-1, keepdims=True), sink)
    p = jnp.exp(s - m)
    p = p / (jnp.sum(p, axis=-1, keepdims=True) + jnp.exp(sink - m))
    o = jnp.einsum('bnhgqk,bnkhd->bnqhgd', p.astype(v.dtype), vw)
    return o.reshape(B, S, SWA_Q_HEADS * d)


def memory_cross_attention(h, m, wq, wkv, q_gain, k_gain):
    B, S, _ = h.shape
    N = m.shape[1]
    q = (h @ wq).reshape(B, S, MEM_HEADS, MEM_HEAD_DIM)
    kv = m @ wkv
    k = kv[..., :MEM_WIDTH].reshape(B, N, MEM_HEADS, MEM_HEAD_DIM)
    v = kv[..., MEM_WIDTH:].reshape(B, N, MEM_HEADS, MEM_HEAD_DIM)
    q, k = rms_norm(q, q_gain), rms_norm(k, k_gain)
    s = jnp.einsum('bshd,bnhd->bhsn', q, k).astype(jnp.float32) * (MEM_HEAD_DIM ** -0.5)
    p = jax.nn.softmax(s, axis=-1)
    o = jnp.einsum('bhsn,bnhd->bshd', p.astype(v.dtype), v)
    return o.reshape(B, S, MEM_WIDTH)


def swiglu(h, w_gate, w_up, w_down):
    return (jax.nn.silu(h @ w_gate) * (h @ w_up)) @ w_down


def moe_swiglu(h, router_w, router_b, w_gate, w_up, w_down):
    T, D = h.shape
    logits = (h @ router_w).astype(jnp.float32) + router_b.astype(jnp.float32)
    top_val, top_idx = lax.top_k(logits, TOP_K)
    gates = jax.nn.softmax(top_val, axis=-1)
    flat_e = top_idx.reshape(-1)
    flat_t = jnp.repeat(jnp.arange(T, dtype=jnp.int32), TOP_K)
    flat_g = gates.reshape(-1).astype(h.dtype)
    onehot = jax.nn.one_hot(flat_e, N_EXPERTS, dtype=jnp.int32)
    rank = jnp.sum(jnp.cumsum(onehot, axis=0) * onehot, axis=-1) - 1
    counts = jnp.sum(onehot, axis=0)
    pcounts = (counts + MOE_BLK - 1) // MOE_BLK * MOE_BLK
    pends = jnp.cumsum(pcounts)
    pstarts = pends - pcounts
    dest = pstarts[flat_e] + rank
    P = T * TOP_K + N_EXPERTS * MOE_BLK
    nb = P // MOE_BLK
    row_tok = jnp.zeros((P,), jnp.int32).at[dest].set(flat_t)
    row_gate = jnp.zeros((P,), h.dtype).at[dest].set(flat_g)
    blk_e = jnp.minimum(jnp.searchsorted(pends, jnp.arange(nb) * MOE_BLK, side='right'), N_EXPERTS - 1)

    def one_block(args):
        tok, e = args
        return swiglu(h[tok], w_gate[e], w_up[e], w_down[e])

    ys = lax.map(one_block, (row_tok.reshape(nb, MOE_BLK), blk_e)).reshape(P, D)
    return jax.ops.segment_sum(ys * row_gate[:, None], row_tok, num_segments=T)


def setup_inputs(seed: int = 0) -> dict:
    key = jax.random.key(seed)
    ks = iter(jax.random.split(key, 40))
    f32 = jnp.float32
    res_scale = (2 * DEPTH) ** -0.5

    def w(shape, fan_in, extra=1.0):
        return jax.random.normal(next(ks), shape, f32) * (fan_in ** -0.5) * extra

    def gain(shape):
        return 1.0 + 0.02 * jax.random.normal(next(ks), shape, f32)

    return {
        "x": jax.random.normal(next(ks), (BATCH, SEQ, D_MODEL), f32),
        "mem": jax.random.normal(next(ks), (BATCH, N_MEM, D_MODEL), f32),
        "norm_mix": gain((DEPTH, D_MODEL)),
        "w_in": w((DEPTH, D_MODEL, IN_COLS), D_MODEL),
        "sb_out_gain": gain((DEPTH, SB_WIDTH)),
        "swa_q_gain": gain((DEPTH, HEAD_DIM)),
        "swa_k_gain": gain((DEPTH, HEAD_DIM)),
        "swa_sinks": 0.5 * jax.random.normal(next(ks), (DEPTH, SWA_Q_HEADS), f32),
        "swa_out_gain": gain((DEPTH, SWA_WIDTH)),
        "rel_bias": 0.5 * jax.random.normal(next(ks), (N_BUCKETS, SWA_Q_HEADS), f32),
        "w_out": w((DEPTH, MIX_WIDTH, D_MODEL), MIX_WIDTH, res_scale),
        "norm_xattn": gain((DEPTH, D_MODEL)),
        "norm_mem": gain((DEPTH, D_MODEL)),
        "xattn_wq": w((DEPTH, D_MODEL, MEM_WIDTH), D_MODEL),
        "xattn_wkv": w((DEPTH, D_MODEL, 2 * MEM_WIDTH), D_MODEL),
        "xattn_q_gain": gain((DEPTH, MEM_HEAD_DIM)),
        "xattn_k_gain": gain((DEPTH, MEM_HEAD_DIM)),
        "xattn_wo": w((DEPTH, MEM_WIDTH, D_MODEL), MEM_WIDTH, res_scale),
        "norm_ffn": gain((DEPTH, D_MODEL)),
        "dense_w_gate": w((N_DENSE, D_MODEL, D_FF), D_MODEL),
        "dense_w_up": w((N_DENSE, D_MODEL, D_FF), D_MODEL),
        "dense_w_down": w((N_DENSE, D_FF, D_MODEL), D_FF, res_scale),
        "router_w": w((N_MOE, D_MODEL, N_EXPERTS), D_MODEL),
        "router_b": 0.01 * jax.random.normal(next(ks), (N_MOE, N_EXPERTS), f32),
        "exp_w_gate": w((N_MOE, N_EXPERTS, D_MODEL, D_FF), D_MODEL),
        "exp_w_up": w((N_MOE, N_EXPERTS, D_MODEL, D_FF), D_MODEL),
        "exp_w_down": w((N_MOE, N_EXPERTS, D_FF, D_MODEL), D_FF, res_scale),
    }


def reference(x, mem, norm_mix, w_in, sb_out_gain, swa_q_gain, swa_k_gain, swa_sinks,
              swa_out_gain, rel_bias, w_out, norm_xattn, norm_mem, xattn_wq, xattn_wkv,
              xattn_q_gain, xattn_k_gain, xattn_wo, norm_ffn, dense_w_gate, dense_w_up,
              dense_w_down, router_w, router_b, exp_w_gate, exp_w_up, exp_w_down):
    B, S, D = x.shape
    dist = WINDOW + np.arange(WINDOW)[:, None] - np.arange(2 * WINDOW)[None, :]
    swa_bias = jnp.transpose(rel_bias[t5_buckets(dist)], (2, 0, 1))
    o1 = 3 * SB_WIDTH
    o2 = o1 + SWA_WIDTH
    o3 = o2 + SWA_KV_WIDTH
    for l in range(DEPTH):
        h = rms_norm(x, norm_mix[l])
        proj = h @ w_in[l]
        sb_q = proj[..., 0:SB_WIDTH].reshape(B, S, SB_HEADS, HEAD_DIM).transpose(0, 2, 1, 3)
        sb_k = proj[..., SB_WIDTH:2 * SB_WIDTH].reshape(B, S, SB_HEADS, HEAD_DIM).transpose(0, 2, 1, 3)
        sb_v = proj[..., 2 * SB_WIDTH:o1].reshape(B, S, SB_HEADS, HEAD_DIM).transpose(0, 2, 1, 3)
        sb_o = stick_breaking_attention(sb_q, sb_k, sb_v).transpose(0, 2, 1, 3).reshape(B, S, SB_WIDTH)
        sw_q = rms_norm(proj[..., o1:o2].reshape(B, S, SWA_Q_HEADS, HEAD_DIM), swa_q_gain[l])
        sw_k = rms_norm(proj[..., o2:o3].reshape(B, S, SWA_KV_HEADS, HEAD_DIM), swa_k_gain[l])
        sw_v = proj[..., o3:].reshape(B, S, SWA_KV_HEADS, HEAD_DIM)
        sw_o = sliding_window_sink_attention(sw_q, sw_k, sw_v, swa_sinks[l], swa_bias)
        mixed = jnp.concatenate([rms_norm(sb_o, sb_out_gain[l]), rms_norm(sw_o, swa_out_gain[l])], axis=-1)
        x = x + mixed @ w_out[l]
        hx = rms_norm(x, norm_xattn[l])
        hm = rms_norm(mem, norm_mem[l])
        x = x + memory_cross_attention(hx, hm, xattn_wq[l], xattn_wkv[l],
                                       xattn_q_gain[l], xattn_k_gain[l]) @ xattn_wo[l]
        hf = rms_norm(x, norm_ffn[l])
        i = l // 2
        if l % 2 == 0:
            x = x + swiglu(hf, dense_w_gate[i], dense_w_up[i], dense_w_down[i])
        else:
            y = moe_swiglu(hf.reshape(B * S, D), router_w[i], router_b[i],
                           exp_w_gate[i], exp_w_up[i], exp_w_down[i])
            x = x + y.reshape(B, S, D)
    return x
```

```python
import functools

import numpy as np
import jax
import jax.numpy as jnp
from jax import lax
from jax.experimental import pallas as pl
from jax.experimental.pallas import tpu as pltpu

F32 = jnp.float32
BF16 = jnp.bfloat16

HEAD_DIM = 64
SB_WIDTH = 512
SWA_WIDTH = 512
SWA_Q_HEADS = 8
SWA_GROUP = 4
SWA_KV_HEADS = 2
WINDOW = 128
N_BUCKETS = 32
MAX_DISTANCE = 128
MEM_HEADS = 4
MEM_HEAD_DIM = 128
MEM_WIDTH = 512
N_EXPERTS = 8
EPS = 1e-6
LANES = 128

VMEM_LIMIT = 56 * 1024 * 1024

TOKEN_TILE = 512
FFN_TOKEN_TILE = 1024
FFN_COL_TILE = 512
SB_TILE = 256
MOE_ROW_TILE = 1024


def _params(*sem):
    return pltpu.CompilerParams(dimension_semantics=sem, vmem_limit_bytes=VMEM_LIMIT)


def _rms(x, g):
    return x * lax.rsqrt(jnp.mean(x * x, axis=-1, keepdims=True) + EPS) * g


def _dot(a, b):
    return jnp.dot(a, b, preferred_element_type=F32)


def _dot_nt(a, b):
    return lax.dot_general(a, b, (((1,), (1,)), ((), ())), preferred_element_type=F32)


def _inproj_kernel(x_ref, g_ref, w_ref, o_ref, *, col_tile):
    h = _rms(x_ref[...], g_ref[...]).astype(BF16)
    for c in range(w_ref.shape[1] // col_tile):
        sl = slice(c * col_tile, (c + 1) * col_tile)
        o_ref[:, sl] = _dot(h, w_ref[:, sl]).astype(BF16)


def _inproj(x2, g, w, l):
    T, D = x2.shape
    N = w.shape[-1]
    tm = TOKEN_TILE
    return pl.pallas_call(
        functools.partial(_inproj_kernel, col_tile=N // 3),
        grid=(T // tm,),
        in_specs=[
            pl.BlockSpec((tm, D), lambda i: (i, 0)),
            pl.BlockSpec((None, 1, D), lambda i: (l, 0, 0)),
            pl.BlockSpec((None, D, N), lambda i: (l, 0, 0)),
        ],
        out_specs=pl.BlockSpec((tm, N), lambda i: (i, 0)),
        out_shape=jax.ShapeDtypeStruct((T, N), BF16),
        compiler_params=_params("parallel"),
        name="inproj",
    )(x2, g, w)


def _sb_kernel(q_ref, k_ref, v_ref, o_ref, acc_ref, c_ref, *, tile):
    S = q_ref.shape[0]
    lane = lax.broadcasted_iota(jnp.int32, (1, LANES), 1)
    lo_half = lane < HEAD_DIM
    row = lax.broadcasted_iota(jnp.int32, (tile, tile), 0)
    col = lax.broadcasted_iota(jnp.int32, (tile, tile), 1)
    strict = col < row
    suffix = (row > col).astype(BF16)

    def q_block(i, _):
        q0 = pl.multiple_of(i * tile, tile)
        q2 = q_ref[pl.ds(q0, tile), :] * jnp.asarray(HEAD_DIM ** -0.5, BF16)
        zero = jnp.zeros_like(q2)
        qs = (jnp.where(lo_half, q2, zero), jnp.where(lo_half, zero, q2))
        acc_ref[...] = jnp.zeros_like(acc_ref)
        c_ref[...] = jnp.zeros_like(c_ref)

        def key_tile(j, masked):
            k0 = pl.multiple_of(j * tile, tile)
            k2 = k_ref[pl.ds(k0, tile), :]
            v2 = v_ref[pl.ds(k0, tile), :]
            for h in range(2):
                z = _dot_nt(qs[h], k2)
                log_beta = jnp.minimum(z, 0.0) - jnp.log1p(jnp.exp(-jnp.abs(z)))
                log_om = log_beta - z
                if masked:
                    log_om = jnp.where(strict, log_om, 0.0)
                hi = log_om.astype(BF16)
                lo = (log_om - hi.astype(F32)).astype(BF16)
                tail = _dot(hi, suffix) + _dot(lo, suffix)
                c = c_ref[h]
                a = jnp.exp(log_beta + tail + c)
                if masked:
                    a = jnp.where(strict, a, 0.0)
                acc_ref[h] += _dot(a.astype(BF16), v2)
                c_ref[h] = c + jnp.sum(log_om, axis=-1, keepdims=True)

        key_tile(i, True)

        def off_diag(jj, _):
            key_tile(i - 1 - jj, False)
            return 0

        lax.fori_loop(0, i, off_diag, 0)
        o_ref[pl.ds(q0, tile), :] = jnp.where(lo_half, acc_ref[0], acc_ref[1])
        return 0

    lax.fori_loop(0, S // tile, q_block, 0)


def _sb_attention(proj3):
    B, S, _ = proj3.shape
    pairs = SB_WIDTH // LANES
    tile = min(SB_TILE, S)
    spec = lambda off: pl.BlockSpec((None, S, LANES), lambda b, p: (b, 0, off + p))
    return pl.pallas_call(
        functools.partial(_sb_kernel, tile=tile),
        grid=(B, pairs),
        in_specs=[spec(0), spec(pairs), spec(2 * pairs)],
        out_specs=pl.BlockSpec((None, S, LANES), lambda b, p: (b, 0, p)),
        out_shape=jax.ShapeDtypeStruct((B, S, SB_WIDTH), F32),
        scratch_shapes=[pltpu.VMEM((2, tile, LANES), F32), pltpu.VMEM((2, tile, 1), F32)],
        compiler_params=_params("parallel", "parallel"),
        name="sb_attention",
    )(proj3, proj3, proj3)


def _swa_kernel(sink_ref, q_ref, kp_ref, kc_ref, vp_ref, vc_ref, qg_ref, kg_ref, bias_ref, o_ref):
    n = pl.program_id(1)
    W = WINDOW
    q = q_ref[...].astype(F32)
    k = jnp.concatenate([kp_ref[...], kc_ref[...]], axis=0).astype(F32)
    v = jnp.concatenate([vp_ref[...], vc_ref[...]], axis=0)
    row = lax.broadcasted_iota(jnp.int32, (W, 2 * W), 0)
    col = lax.broadcasted_iota(jnp.int32, (W, 2 * W), 1)
    valid = (col > row) & (col <= row + W) & ((n > 0) | (col >= W))
    for g in range(SWA_KV_HEADS):
        ks = slice(g * HEAD_DIM, (g + 1) * HEAD_DIM)
        kn = _rms(k[:, ks], kg_ref[...]).astype(BF16)
        vh = v[:, ks]
        for hh in range(SWA_GROUP):
            h = g * SWA_GROUP + hh
            qsl = slice(h * HEAD_DIM, (h + 1) * HEAD_DIM)
            qn = _rms(q[:, qsl], qg_ref[...]).astype(BF16)
            s = _dot_nt(qn, kn) * (HEAD_DIM ** -0.5) + bias_ref[h]
            s = jnp.where(valid, s, -jnp.inf)
            sink = sink_ref[h]
            m = jnp.maximum(jnp.max(s, axis=-1, keepdims=True), sink)
            p = jnp.exp(s - m)
            den = jnp.sum(p, axis=-1, keepdims=True) + jnp.exp(sink - m)
            o_ref[:, qsl] = _dot(p.astype(BF16), vh) / den


def _swa_attention(proj3, sinks, q_gain, k_gain, bias):
    B, S, _ = proj3.shape
    W = WINDOW
    q_blk = (3 * SB_WIDTH) // SWA_WIDTH
    k_blk = (3 * SB_WIDTH + SWA_WIDTH) // LANES
    v_blk = k_blk + 1
    prev = lambda c: pl.BlockSpec((None, W, LANES), lambda b, n: (b, jnp.maximum(n - 1, 0), c))
    cur = lambda c: pl.BlockSpec((None, W, LANES), lambda b, n: (b, n, c))
    return pl.pallas_call(
        _swa_kernel,
        grid=(B, S // W),
        in_specs=[
            pl.BlockSpec(memory_space=pltpu.SMEM),
            pl.BlockSpec((None, W, SWA_WIDTH), lambda b, n: (b, n, q_blk)),
            prev(k_blk), cur(k_blk), prev(v_blk), cur(v_blk),
            pl.BlockSpec((1, HEAD_DIM), lambda b, n: (0, 0)),
            pl.BlockSpec((1, HEAD_DIM), lambda b, n: (0, 0)),
            pl.BlockSpec((SWA_Q_HEADS, W, 2 * W), lambda b, n: (0, 0, 0)),
        ],
        out_specs=pl.BlockSpec((None, W, SWA_WIDTH), lambda b, n: (b, n, 0)),
        out_shape=jax.ShapeDtypeStruct((B, S, SWA_WIDTH), F32),
        compiler_params=_params("parallel", "parallel"),
        name="swa_attention",
    )(sinks, proj3, proj3, proj3, proj3, proj3, q_gain, k_gain, bias)


def _outproj_kernel(sb_ref, sw_ref, gsb_ref, gsw_ref, w_ref, x_ref, o_ref):
    a = _rms(sb_ref[...], gsb_ref[...]).astype(BF16)
    b = _rms(sw_ref[...], gsw_ref[...]).astype(BF16)
    o_ref[...] = x_ref[...] + _dot(a, w_ref[:SB_WIDTH, :]) + _dot(b, w_ref[SB_WIDTH:, :])


def _outproj(sb_o, sw_o, g_sb, g_sw, w, x2, l):
    T, D = x2.shape
    tm = TOKEN_TILE
    return pl.pallas_call(
        _outproj_kernel,
        grid=(T // tm,),
        in_specs=[
            pl.BlockSpec((tm, SB_WIDTH), lambda i: (i, 0)),
            pl.BlockSpec((tm, SWA_WIDTH), lambda i: (i, 0)),
            pl.BlockSpec((None, 1, SB_WIDTH), lambda i: (l, 0, 0)),
            pl.BlockSpec((None, 1, SWA_WIDTH), lambda i: (l, 0, 0)),
            pl.BlockSpec((None, SB_WIDTH + SWA_WIDTH, D), lambda i: (l, 0, 0)),
            pl.BlockSpec((tm, D), lambda i: (i, 0)),
        ],
        out_specs=pl.BlockSpec((tm, D), lambda i: (i, 0)),
        out_shape=jax.ShapeDtypeStruct((T, D), F32),
        compiler_params=_params("parallel"),
        name="outproj",
    )(sb_o, sw_o, g_sb, g_sw, w, x2)


def _memkv_kernel(m_ref, g_ref, w_ref, kg_ref, k_ref, v_ref):
    h = _rms(m_ref[...], g_ref[...]).astype(BF16)
    kv = _dot(h, w_ref[...])
    for hd in range(MEM_HEADS):
        sl = slice(hd * MEM_HEAD_DIM, (hd + 1) * MEM_HEAD_DIM)
        k_ref[:, sl] = _rms(kv[:, sl], kg_ref[...]).astype(BF16)
    v_ref[...] = kv[:, MEM_WIDTH:].astype(BF16)


def _memkv(mem, norm_mem, wkv, k_gain):
    B, N, D = mem.shape
    L = wkv.shape[0]
    out = jax.ShapeDtypeStruct((L, B, N, MEM_WIDTH), BF16)
    return pl.pallas_call(
        _memkv_kernel,
        grid=(L, B),
        in_specs=[
            pl.BlockSpec((None, N, D), lambda l, b: (b, 0, 0)),
            pl.BlockSpec((None, 1, D), lambda l, b: (l, 0, 0)),
            pl.BlockSpec((None, D, 2 * MEM_WIDTH), lambda l, b: (l, 0, 0)),
            pl.BlockSpec((None, 1, MEM_HEAD_DIM), lambda l, b: (l, 0, 0)),
        ],
        out_specs=[pl.BlockSpec((None, None, N, MEM_WIDTH), lambda l, b: (l, b, 0, 0))] * 2,
        out_shape=[out, out],
        compiler_params=_params("parallel", "parallel"),
        name="memkv",
    )(mem, norm_mem, wkv, k_gain)


def _xattn_kernel(x_ref, gx_ref, wq_ref, qg_ref, k_ref, v_ref, wo_ref, o_ref):
    x = x_ref[...]
    h = _rms(x, gx_ref[...]).astype(BF16)
    q = _dot(h, wq_ref[...])
    outs = []
    for hd in range(MEM_HEADS):
        sl = slice(hd * MEM_HEAD_DIM, (hd + 1) * MEM_HEAD_DIM)
        qn = _rms(q[:, sl], qg_ref[...]).astype(BF16)
        s = _dot_nt(qn, k_ref[:, sl]) * (MEM_HEAD_DIM ** -0.5)
        p = jnp.exp(s - jnp.max(s, axis=-1, keepdims=True))
        den = jnp.sum(p, axis=-1, keepdims=True)
        outs.append((_dot(p.astype(BF16), v_ref[:, sl]) / den).astype(BF16))
    o_ref[...] = x + _dot(jnp.concatenate(outs, axis=1), wo_ref[...])


def _xattn(x3, norm_x, wq, q_gain, kmem, vmem, wo, l):
    B, S, D = x3.shape
    N = kmem.shape[2]
    tm = min(TOKEN_TILE, S)
    return pl.pallas_call(
        _xattn_kernel,
        grid=(B, S // tm),
        in_specs=[
            pl.BlockSpec((None, tm, D), lambda b, i: (b, i, 0)),
            pl.BlockSpec((None, 1, D), lambda b, i: (l, 0, 0)),
            pl.BlockSpec((None, D, MEM_WIDTH), lambda b, i: (l, 0, 0)),
            pl.BlockSpec((None, 1, MEM_HEAD_DIM), lambda b, i: (l, 0, 0)),
            pl.BlockSpec((None, None, N, MEM_WIDTH), lambda b, i: (l, b, 0, 0)),
            pl.BlockSpec((None, None, N, MEM_WIDTH), lambda b, i: (l, b, 0, 0)),
            pl.BlockSpec((None, MEM_WIDTH, D), lambda b, i: (l, 0, 0)),
        ],
        out_specs=pl.BlockSpec((None, tm, D), lambda b, i: (b, i, 0)),
        out_shape=jax.ShapeDtypeStruct((B, S, D), F32),
        compiler_params=_params("parallel", "parallel"),
        name="xattn",
    )(x3, norm_x, wq, q_gain, kmem, vmem, wo)


def _swiglu_step(h, wg_ref, wu_ref, wd_ref):
    a = _dot(h, wg_ref[...])
    u = _dot(h, wu_ref[...])
    act = a / (1.0 + jnp.exp(-a)) * u
    return _dot(act.astype(BF16), wd_ref[...])


def _ffn_kernel(x_ref, g_ref, wg_ref, wu_ref, wd_ref, o_ref, h_ref):
    @pl.when(pl.program_id(1) == 0)
    def _():
        x = x_ref[...]
        h_ref[...] = _rms(x, g_ref[...]).astype(BF16)
        o_ref[...] = x

    o_ref[...] += _swiglu_step(h_ref[...], wg_ref, wu_ref, wd_ref)


def _ffn_dense(x2, g, wg, wu, wd, l, i_dense):
    T, D = x2.shape
    F = wg.shape[-1]
    tm, tf = min(FFN_TOKEN_TILE, T), FFN_COL_TILE
    return pl.pallas_call(
        _ffn_kernel,
        grid=(T // tm, F // tf),
        in_specs=[
            pl.BlockSpec((tm, D), lambda i, f: (i, 0)),
            pl.BlockSpec((None, 1, D), lambda i, f: (l, 0, 0)),
            pl.BlockSpec((None, D, tf), lambda i, f: (i_dense, 0, f)),
            pl.BlockSpec((None, D, tf), lambda i, f: (i_dense, 0, f)),
            pl.BlockSpec((None, tf, D), lambda i, f: (i_dense, f, 0)),
        ],
        out_specs=pl.BlockSpec((tm, D), lambda i, f: (i, 0)),
        out_shape=jax.ShapeDtypeStruct((T, D), F32),
        scratch_shapes=[pltpu.VMEM((tm, D), BF16)],
        compiler_params=_params("parallel", "arbitrary"),
        name="ffn_dense",
    )(x2, g, wg, wu, wd)


def _moe_ffn_kernel(be_ref, xs_ref, wg_ref, wu_ref, wd_ref, o_ref, h_ref):
    del be_ref

    @pl.when(pl.program_id(1) == 0)
    def _():
        h_ref[...] = xs_ref[...].astype(BF16)
        o_ref[...] = jnp.zeros_like(o_ref)

    o_ref[...] += _swiglu_step(h_ref[...], wg_ref, wu_ref, wd_ref)


def _moe_ffn(blk_e, xs, wg, wu, wd, i_moe):
    P, D = xs.shape
    F = wg.shape[-1]
    tm, tf = MOE_ROW_TILE, FFN_COL_TILE
    return pl.pallas_call(
        _moe_ffn_kernel,
        grid_spec=pltpu.PrefetchScalarGridSpec(
            num_scalar_prefetch=1,
            grid=(P // tm, F // tf),
            in_specs=[
                pl.BlockSpec((tm, D), lambda i, f, be: (i, 0)),
                pl.BlockSpec((None, None, D, tf), lambda i, f, be: (i_moe, be[i], 0, f)),
                pl.BlockSpec((None, None, D, tf), lambda i, f, be: (i_moe, be[i], 0, f)),
                pl.BlockSpec((None, None, tf, D), lambda i, f, be: (i_moe, be[i], f, 0)),
            ],
            out_specs=pl.BlockSpec((tm, D), lambda i, f, be: (i, 0)),
            scratch_shapes=[pltpu.VMEM((tm, D), BF16)],
        ),
        out_shape=jax.ShapeDtypeStruct((P, D), F32),
        compiler_params=_params("parallel", "arbitrary"),
        name="moe_ffn",
    )(blk_e, xs, wg, wu, wd)


def _router_kernel(x_ref, g_ref, rw_ref, rb_ref, hf_ref, ei_ref, gt_ref, cnt_ref, carry_ref):
    tm = x_ref.shape[0]

    @pl.when(pl.program_id(0) == 0)
    def _():
        carry_ref[...] = jnp.zeros_like(carry_ref)

    hf = _rms(x_ref[...], g_ref[...])
    hf_ref[...] = hf
    lane = lax.broadcasted_iota(jnp.int32, (tm, LANES), 1)
    logits = jnp.dot(hf, rw_ref[...], precision=lax.Precision.HIGHEST, preferred_element_type=F32) + rb_ref[...]
    logits = jnp.where(lane < N_EXPERTS, logits, -jnp.inf)
    m1 = jnp.max(logits, axis=-1, keepdims=True)
    i1 = jnp.min(jnp.where(logits == m1, lane, LANES), axis=-1, keepdims=True)
    rest = jnp.where(lane == i1, -jnp.inf, logits)
    m2 = jnp.max(rest, axis=-1, keepdims=True)
    i2 = jnp.min(jnp.where(rest == m2, lane, LANES), axis=-1, keepdims=True)
    e = jnp.exp(m2 - m1)
    g1 = 1.0 / (1.0 + e)
    g2 = e / (1.0 + e)
    oh1 = lane == i1
    oh2 = lane == i2
    oh = (oh1 | oh2).astype(F32)
    row = lax.broadcasted_iota(jnp.int32, (tm, tm), 0)
    col = lax.broadcasted_iota(jnp.int32, (tm, tm), 1)
    earlier = (row > col).astype(BF16)
    before = _dot(earlier, oh.astype(BF16)) + carry_ref[...]
    r1 = jnp.sum(jnp.where(oh1, before, 0.0), axis=-1, keepdims=True).astype(jnp.int32)
    r2 = jnp.sum(jnp.where(oh2, before, 0.0), axis=-1, keepdims=True).astype(jnp.int32)
    carry_ref[...] += jnp.sum(oh, axis=0, keepdims=True)
    cnt_ref[...] = carry_ref[...]
    ei = jnp.where(lane == 0, i1, jnp.where(lane == 1, i2, jnp.where(lane == 2, r1, jnp.where(lane == 3, r2, 0))))
    ei_ref[...] = ei[:, :8]
    gt = jnp.where(lane == 0, g1, jnp.where(lane == 1, g2, 0.0))
    gt_ref[...] = gt[:, :8]


def _router(x2, g, rw, rb, l, i_moe):
    T, D = x2.shape
    tm = TOKEN_TILE
    return pl.pallas_call(
        _router_kernel,
        grid=(T // tm,),
        in_specs=[
            pl.BlockSpec((tm, D), lambda i: (i, 0)),
            pl.BlockSpec((None, 1, D), lambda i: (l, 0, 0)),
            pl.BlockSpec((None, D, LANES), lambda i: (i_moe, 0, 0)),
            pl.BlockSpec((None, 1, LANES), lambda i: (i_moe, 0, 0)),
        ],
        out_specs=[
            pl.BlockSpec((tm, D), lambda i: (i, 0)),
            pl.BlockSpec((tm, 8), lambda i: (i, 0)),
            pl.BlockSpec((tm, 8), lambda i: (i, 0)),
            pl.BlockSpec((1, LANES), lambda i: (0, 0)),
        ],
        out_shape=[
            jax.ShapeDtypeStruct((T, D), F32),
            jax.ShapeDtypeStruct((T, 8), jnp.int32),
            jax.ShapeDtypeStruct((T, 8), F32),
            jax.ShapeDtypeStruct((1, LANES), F32),
        ],
        scratch_shapes=[pltpu.VMEM((1, LANES), F32)],
        compiler_params=_params("arbitrary"),
        name="router",
    )(x2, g, rw, rb)


def _dispatch_kernel(dest_ref, hf_ref, xs_in_ref, xs_ref, sem):
    del xs_in_ref
    tm = hf_ref.shape[0]

    def row_copy(r, d):
        return pltpu.make_async_copy(hf_ref.at[pl.ds(r, 1)], xs_ref.at[pl.ds(d, 1)], sem)

    def issue(r, _):
        row_copy(r, dest_ref[2 * r]).start()
        row_copy(r, dest_ref[2 * r + 1]).start()
        return 0

    lax.fori_loop(0, tm, issue, 0)

    def drain(r, _):
        row_copy(0, 0).wait()
        row_copy(0, 0).wait()
        return 0

    lax.fori_loop(0, tm, drain, 0)


def _dispatch(dest, hf, xs_zero):
    T, D = hf.shape
    tm = TOKEN_TILE
    return pl.pallas_call(
        _dispatch_kernel,
        grid=(T // tm,),
        in_specs=[
            pl.BlockSpec((2 * tm,), lambda i: (i,), memory_space=pltpu.SMEM),
            pl.BlockSpec((tm, D), lambda i: (i, 0)),
            pl.BlockSpec(memory_space=pl.ANY),
        ],
        out_specs=pl.BlockSpec(memory_space=pl.ANY),
        out_shape=jax.ShapeDtypeStruct(xs_zero.shape, xs_zero.dtype),
        scratch_shapes=[pltpu.SemaphoreType.DMA(())],
        input_output_aliases={2: 0},
        compiler_params=_params("arbitrary"),
        name="moe_dispatch",
    )(dest, hf, xs_zero)


def _combine_kernel(dest_ref, x_ref, gt_ref, y_ref, o_ref, buf_ref, sem):
    tm = x_ref.shape[0]

    def row_copy(slot, r, d):
        return pltpu.make_async_copy(y_ref.at[pl.ds(d, 1)], buf_ref.at[slot, pl.ds(r, 1)], sem)

    def issue(r, _):
        row_copy(0, r, dest_ref[2 * r]).start()
        row_copy(1, r, dest_ref[2 * r + 1]).start()
        return 0

    lax.fori_loop(0, tm, issue, 0)

    def drain(r, _):
        row_copy(0, 0, 0).wait()
        row_copy(1, 0, 0).wait()
        return 0

    lax.fori_loop(0, tm, drain, 0)
    gt = gt_ref[...]
    o_ref[...] = x_ref[...] + gt[:, 0:1] * buf_ref[0] + gt[:, 1:2] * buf_ref[1]


def _combine(dest, x2, gates, y):
    T, D = x2.shape
    tm = TOKEN_TILE
    return pl.pallas_call(
        _combine_kernel,
        grid=(T // tm,),
        in_specs=[
            pl.BlockSpec((2 * tm,), lambda i: (i,), memory_space=pltpu.SMEM),
            pl.BlockSpec((tm, D), lambda i: (i, 0)),
            pl.BlockSpec((tm, 8), lambda i: (i, 0)),
            pl.BlockSpec(memory_space=pl.ANY),
        ],
        out_specs=pl.BlockSpec((tm, D), lambda i: (i, 0)),
        out_shape=jax.ShapeDtypeStruct((T, D), F32),
        scratch_shapes=[pltpu.VMEM((2, tm, D), F32), pltpu.SemaphoreType.DMA(())],
        compiler_params=_params("arbitrary"),
        name="moe_combine",
    )(dest, x2, gates, y)


def _moe(x2, g, rw, rb, wg, wu, wd, l, i_moe):
    T, D = x2.shape
    tmm = MOE_ROW_TILE
    P = 2 * T + N_EXPERTS * tmm
    hf, ei, gates, cnt = _router(x2, g, rw, rb, l, i_moe)
    counts = cnt[0, :N_EXPERTS].astype(jnp.int32)
    pcounts = (counts + tmm - 1) // tmm * tmm
    pends = jnp.cumsum(pcounts)
    pstarts = pends - pcounts
    dest = (jnp.take(pstarts, ei[:, 0:2]) + ei[:, 2:4]).reshape(-1)
    blk_e = jnp.minimum(jnp.searchsorted(pends, jnp.arange(P // tmm, dtype=jnp.int32) * tmm, side="right"),
                        N_EXPERTS - 1).astype(jnp.int32)
    xs = _dispatch(dest, hf, jnp.zeros((P, D), F32))
    y = _moe_ffn(blk_e, xs, wg, wu, wd, i_moe)
    return _combine(dest, x2, gates, y)


def _t5_buckets(dist):
    n = np.maximum(dist, 0)
    max_exact = N_BUCKETS // 2
    large = max_exact + (np.log(np.maximum(n, 1) / max_exact) / np.log(MAX_DISTANCE / max_exact)
                         * (N_BUCKETS - max_exact)).astype(np.int32)
    large = np.minimum(large, N_BUCKETS - 1)
    return np.where(n < max_exact, n, large).astype(np.int32)


def kernel(x, mem, norm_mix, w_in, sb_out_gain, swa_q_gain, swa_k_gain, swa_sinks, swa_out_gain, rel_bias, w_out, norm_xattn, norm_mem, xattn_wq, xattn_wkv, xattn_q_gain, xattn_k_gain, xattn_wo, norm_ffn, dense_w_gate, dense_w_up, dense_w_down, router_w, router_b, exp_w_gate, exp_w_up, exp_w_down):
    B, S, D = x.shape
    depth = w_in.shape[0]
    T = B * S
    row3 = lambda a: a.reshape(a.shape[0], 1, a.shape[1])
    bf = lambda a: a.astype(BF16)

    dist = WINDOW + np.arange(WINDOW)[:, None] - np.arange(2 * WINDOW)[None, :]
    swa_bias = jnp.transpose(rel_bias[_t5_buckets(dist)], (2, 0, 1))
    router_w_p = jnp.pad(router_w, ((0, 0), (0, 0), (0, LANES - N_EXPERTS)))
    router_b_p = row3(jnp.pad(router_b, ((0, 0), (0, LANES - N_EXPERTS))))

    w_in_b, w_out_b = bf(w_in), bf(w_out)
    wq_b, wkv_b, wo_b = bf(xattn_wq), bf(xattn_wkv), bf(xattn_wo)
    dg_b, du_b, dd_b = bf(dense_w_gate), bf(dense_w_up), bf(dense_w_down)
    eg_b, eu_b, ed_b = bf(exp_w_gate), bf(exp_w_up), bf(exp_w_down)
    norm_mix3, norm_x3, norm_f3 = row3(norm_mix), row3(norm_xattn), row3(norm_ffn)
    sb_g3, sw_g3 = row3(sb_out_gain), row3(swa_out_gain)
    xq_g3, xk_g3 = row3(xattn_q_gain), row3(xattn_k_gain)

    kmem, vmem = _memkv(mem, row3(norm_mem), wkv_b, xk_g3)

    x2 = x.reshape(T, D)
    for l in range(depth):
        proj = _inproj(x2, norm_mix3, w_in_b, l).reshape(B, S, -1)
        sb_o = _sb_attention(proj)
        sw_o = _swa_attention(proj, swa_sinks[l], swa_q_gain[l][None, :], swa_k_gain[l][None, :], swa_bias)
        x2 = _outproj(sb_o.reshape(T, -1), sw_o.reshape(T, -1), sb_g3, sw_g3, w_out_b, x2, l)
        x2 = _xattn(x2.reshape(B, S, D), norm_x3, wq_b, xq_g3, kmem, vmem, wo_b, l).reshape(T, D)
        if l % 2 == 0:
            x2 = _ffn_dense(x2, norm_f3, dg_b, du_b, dd_b, l, l // 2)
        else:
            x2 = _moe(x2, norm_f3, router_w_p, router_b_p, eg_b, eu_b, ed_b, l, l // 2)
    return x2.reshape(B, S, D)
```

```python
import functools

import numpy as np
import jax
import jax.numpy as jnp
from jax import lax
from jax.experimental import pallas as pl
from jax.experimental.pallas import tpu as pltpu

F32 = jnp.float32
BF16 = jnp.bfloat16

HEAD_DIM = 64
SB_WIDTH = 512
SWA_WIDTH = 512
SWA_Q_HEADS = 8
SWA_GROUP = 4
SWA_KV_HEADS = 2
WINDOW = 128
N_BUCKETS = 32
MAX_DISTANCE = 128
MEM_HEADS = 4
MEM_HEAD_DIM = 128
MEM_WIDTH = 512
N_EXPERTS = 8
EPS = 1e-6
LANES = 128

VMEM_LIMIT = 56 * 1024 * 1024

TOKEN_TILE = 512
FFN_TOKEN_TILE = 1024
FFN_COL_TILE = 512
SB_TILE = 256
MOE_ROW_TILE = 1024
DMA_ISSUE_UNROLL = 8
SB_LOG_UNDERFLOW = -104.0


def _params(*sem):
    return pltpu.CompilerParams(dimension_semantics=("arbitrary",) * len(sem), vmem_limit_bytes=VMEM_LIMIT)


def _rms(x, g):
    return x * lax.rsqrt(jnp.mean(x * x, axis=-1, keepdims=True) + EPS) * g


def _dot(a, b):
    return jnp.dot(a, b, preferred_element_type=F32)


def _dot_nt(a, b):
    return lax.dot_general(a, b, (((1,), (1,)), ((), ())), preferred_element_type=F32)


def _inproj_kernel(x_ref, g_ref, w_ref, o_ref, *, col_tile):
    h = _rms(x_ref[...], g_ref[...]).astype(BF16)
    for c in range(w_ref.shape[1] // col_tile):
        sl = slice(c * col_tile, (c + 1) * col_tile)
        o_ref[:, sl] = _dot(h, w_ref[:, sl]).astype(BF16)


def _inproj(x2, g, w, l):
    T, D = x2.shape
    N = w.shape[-1]
    tm = TOKEN_TILE
    return pl.pallas_call(
        functools.partial(_inproj_kernel, col_tile=N // 3),
        grid=(T // tm,),
        in_specs=[
            pl.BlockSpec((tm, D), lambda i: (i, 0)),
            pl.BlockSpec((None, 1, D), lambda i: (l, 0, 0)),
            pl.BlockSpec((None, D, N), lambda i: (l, 0, 0)),
        ],
        out_specs=pl.BlockSpec((tm, N), lambda i: (i, 0)),
        out_shape=jax.ShapeDtypeStruct((T, N), BF16),
        compiler_params=_params("parallel"),
        name="inproj",
    )(x2, g, w)


def _sb_kernel(q_ref, k_ref, v_ref, o_ref, kt_ref, acc_ref, c_ref, *, tile):
    S = q_ref.shape[0]
    lane = lax.broadcasted_iota(jnp.int32, (1, LANES), 1)
    lo_half = lane < HEAD_DIM
    row = lax.broadcasted_iota(jnp.int32, (tile, tile), 0)
    col = lax.broadcasted_iota(jnp.int32, (tile, tile), 1)
    strict2 = jnp.concatenate([col < row, col < row], axis=0)
    suffix = (row > col).astype(BF16)

    for j in range(S // tile):
        kt_ref[j] = k_ref[j * tile:(j + 1) * tile, :].T

    def q_block(i, _):
        q0 = pl.multiple_of(i * tile, tile)
        q2 = q_ref[pl.ds(q0, tile), :] * jnp.asarray(HEAD_DIM ** -0.5, BF16)
        zero = jnp.zeros_like(q2)
        qs = jnp.concatenate([jnp.where(lo_half, q2, zero), jnp.where(lo_half, zero, q2)], axis=0)
        acc_ref[...] = jnp.zeros_like(acc_ref)
        c_ref[...] = jnp.zeros_like(c_ref)

        def key_tile(j, masked):
            kt = kt_ref[j]
            v2 = v_ref[pl.ds(pl.multiple_of(j * tile, tile), tile), :]
            z = _dot(qs, kt)
            log_beta = jnp.minimum(z, 0.0) - jnp.log(1.0 + jnp.exp(-jnp.abs(z)))
            log_om = log_beta - z
            if masked:
                log_om = jnp.where(strict2, log_om, 0.0)
            hi = log_om.astype(BF16)
            lo = (log_om - hi.astype(F32)).astype(BF16)
            tail = _dot(jnp.concatenate([hi, lo], axis=0), suffix)
            c = c_ref[...]
            a = jnp.exp(log_beta + (tail[:2 * tile] + tail[2 * tile:]) + c)
            if masked:
                a = jnp.where(strict2, a, 0.0)
            acc_ref[...] += _dot(a.astype(BF16), v2)
            c_ref[...] = c + jnp.sum(log_om, axis=-1, keepdims=True)

        key_tile(i, True)

        def off_diag(state):
            j, _ = state
            key_tile(j, False)
            return j - 1, jnp.max(c_ref[...])

        lax.while_loop(lambda s: (s[0] >= 0) & (s[1] > SB_LOG_UNDERFLOW), off_diag,
                       (i - 1, jnp.float32(0.0)))
        o_ref[pl.ds(q0, tile), :] = jnp.where(lo_half, acc_ref[:tile], acc_ref[tile:])
        return 0

    lax.fori_loop(0, S // tile, q_block, 0)


def _sb_attention(proj3):
    B, S, _ = proj3.shape
    pairs = SB_WIDTH // LANES
    tile = min(SB_TILE, S)
    spec = lambda off: pl.BlockSpec((None, S, LANES), lambda b, p: (b, 0, off + p))
    return pl.pallas_call(
        functools.partial(_sb_kernel, tile=tile),
        grid=(B, pairs),
        in_specs=[spec(0), spec(pairs), spec(2 * pairs)],
        out_specs=pl.BlockSpec((None, S, LANES), lambda b, p: (b, 0, p)),
        out_shape=jax.ShapeDtypeStruct((B, S, SB_WIDTH), F32),
        scratch_shapes=[pltpu.VMEM((S // tile, LANES, tile), BF16),
                        pltpu.VMEM((2 * tile, LANES), F32), pltpu.VMEM((2 * tile, 1), F32)],
        compiler_params=_params("parallel", "parallel"),
        name="sb_attention",
    )(proj3, proj3, proj3)


def _group_mean_sq(x, ones_ref):
    sq = x * x
    hi = sq.astype(BF16)
    lo = (sq - hi.astype(F32)).astype(BF16)
    return _dot(hi, ones_ref[...]) + _dot(lo, ones_ref[...])


def _swa_kernel(sink_ref, q_ref, kp_ref, kc_ref, vp_ref, vc_ref, qg_ref, kg_ref, gq_ref, gk_ref, sel_ref,
                bias_ref, o_ref):
    q = q_ref[...].astype(F32)
    k = jnp.concatenate([kp_ref[...], kc_ref[...]], axis=0).astype(F32)
    v = jnp.concatenate([vp_ref[...], vc_ref[...]], axis=0)
    qn = (q * lax.rsqrt(_group_mean_sq(q, gq_ref) + EPS) * qg_ref[...] * (HEAD_DIM ** -0.5)).astype(BF16)
    kn_t = (k * lax.rsqrt(_group_mean_sq(k, gk_ref) + EPS) * kg_ref[...]).astype(BF16).T
    lane = lax.broadcasted_iota(jnp.int32, (1, LANES), 1)
    lo_half = lane < HEAD_DIM
    zero = jnp.zeros((WINDOW, LANES), BF16)
    W = WINDOW
    head_row = lax.broadcasted_iota(jnp.int32, (SWA_GROUP * W, 1), 0) // W
    for g in range(SWA_KV_HEADS):
        kt_g = kn_t[g * HEAD_DIM:(g + 1) * HEAD_DIM, :]
        kt_dup = jnp.concatenate([kt_g, kt_g], axis=0)
        v_dup = _dot(v, sel_ref[g]).astype(BF16)
        rows = []
        for t in range(g * SWA_GROUP // 2, (g + 1) * SWA_GROUP // 2):
            qt = qn[:, t * LANES:(t + 1) * LANES]
            rows += [jnp.where(lo_half, qt, zero), jnp.where(lo_half, zero, qt)]
        s = _dot(jnp.concatenate(rows, axis=0), kt_dup)
        s = s + bias_ref[g * SWA_GROUP:(g + 1) * SWA_GROUP].reshape(SWA_GROUP * W, 2 * W)
        sink = jnp.zeros((SWA_GROUP * W, 1), F32)
        for hh in range(SWA_GROUP):
            sink = jnp.where(head_row == hh, sink_ref[g * SWA_GROUP + hh], sink)
        m = jnp.maximum(jnp.max(s, axis=-1, keepdims=True), sink)
        p = jnp.exp(s - m)
        den = jnp.sum(p, axis=-1, keepdims=True) + jnp.exp(sink - m)
        r = _dot(p.astype(BF16), v_dup) / den
        for tt in range(SWA_GROUP // 2):
            t = g * SWA_GROUP // 2 + tt
            o_ref[:, t * LANES:(t + 1) * LANES] = jnp.where(
                lo_half, r[2 * tt * W:(2 * tt + 1) * W], r[(2 * tt + 1) * W:(2 * tt + 2) * W])


def _swa_attention(proj3, sinks, q_gain, k_gain, bias):
    B, S, _ = proj3.shape
    W = WINDOW
    q_blk = (3 * SB_WIDTH) // SWA_WIDTH
    k_blk = (3 * SB_WIDTH + SWA_WIDTH) // LANES
    v_blk = k_blk + 1
    group = lambda n: (np.arange(n)[:, None] // HEAD_DIM == np.arange(n)[None, :] // HEAD_DIM) / HEAD_DIM
    ones_q = jnp.asarray(group(SWA_WIDTH), BF16)
    ones_k = jnp.asarray(group(LANES), BF16)
    sel = jnp.asarray(np.stack([np.arange(LANES)[:, None] == g * HEAD_DIM + np.arange(LANES)[None, :] % HEAD_DIM
                                for g in range(SWA_KV_HEADS)]), BF16)
    prev = lambda c: pl.BlockSpec((None, W, LANES), lambda b, n: (b, jnp.maximum(n - 1, 0), c))
    cur = lambda c: pl.BlockSpec((None, W, LANES), lambda b, n: (b, n, c))
    const = lambda *shape: pl.BlockSpec(shape, lambda b, n: (0,) * len(shape))
    return pl.pallas_call(
        _swa_kernel,
        grid=(B, S // W),
        in_specs=[
            pl.BlockSpec(memory_space=pltpu.SMEM),
            pl.BlockSpec((None, W, SWA_WIDTH), lambda b, n: (b, n, q_blk)),
            prev(k_blk), cur(k_blk), prev(v_blk), cur(v_blk),
            const(1, SWA_WIDTH), const(1, LANES),
            const(SWA_WIDTH, SWA_WIDTH), const(LANES, LANES), const(SWA_KV_HEADS, LANES, LANES),
            pl.BlockSpec((None, SWA_Q_HEADS, W, 2 * W), lambda b, n: (jnp.minimum(n, 1), 0, 0, 0)),
        ],
        out_specs=pl.BlockSpec((None, W, SWA_WIDTH), lambda b, n: (b, n, 0)),
        out_shape=jax.ShapeDtypeStruct((B, S, SWA_WIDTH), F32),
        compiler_params=_params("arbitrary", "arbitrary"),
        name="swa_attention",
    )(sinks, proj3, proj3, proj3, proj3, proj3, q_gain, k_gain, ones_q, ones_k, sel, bias)


def _outproj_kernel(sb_ref, sw_ref, gsb_ref, gsw_ref, w_ref, x_ref, o_ref):
    a = _rms(sb_ref[...], gsb_ref[...]).astype(BF16)
    b = _rms(sw_ref[...], gsw_ref[...]).astype(BF16)
    o_ref[...] = x_ref[...] + _dot(a, w_ref[:SB_WIDTH, :]) + _dot(b, w_ref[SB_WIDTH:, :])


def _outproj(sb_o, sw_o, g_sb, g_sw, w, x2, l):
    T, D = x2.shape
    tm = TOKEN_TILE
    return pl.pallas_call(
        _outproj_kernel,
        grid=(T // tm,),
        in_specs=[
            pl.BlockSpec((tm, SB_WIDTH), lambda i: (i, 0)),
            pl.BlockSpec((tm, SWA_WIDTH), lambda i: (i, 0)),
            pl.BlockSpec((None, 1, SB_WIDTH), lambda i: (l, 0, 0)),
            pl.BlockSpec((None, 1, SWA_WIDTH), lambda i: (l, 0, 0)),
            pl.BlockSpec((None, SB_WIDTH + SWA_WIDTH, D), lambda i: (l, 0, 0)),
            pl.BlockSpec((tm, D), lambda i: (i, 0)),
        ],
        out_specs=pl.BlockSpec((tm, D), lambda i: (i, 0)),
        out_shape=jax.ShapeDtypeStruct((T, D), F32),
        compiler_params=_params("parallel"),
        name="outproj",
    )(sb_o, sw_o, g_sb, g_sw, w, x2)


def _memkv_kernel(m_ref, g_ref, w_ref, kg_ref, k_ref, v_ref):
    h = _rms(m_ref[...], g_ref[...]).astype(BF16)
    kv = _dot(h, w_ref[...])
    for hd in range(MEM_HEADS):
        sl = slice(hd * MEM_HEAD_DIM, (hd + 1) * MEM_HEAD_DIM)
        k_ref[:, sl] = _rms(kv[:, sl], kg_ref[...]).astype(BF16)
    v_ref[...] = kv[:, MEM_WIDTH:].astype(BF16)


def _memkv(mem, norm_mem, wkv, k_gain):
    B, N, D = mem.shape
    L = wkv.shape[0]
    out = jax.ShapeDtypeStruct((L, B, N, MEM_WIDTH), BF16)
    return pl.pallas_call(
        _memkv_kernel,
        grid=(L, B),
        in_specs=[
            pl.BlockSpec((None, N, D), lambda l, b: (b, 0, 0)),
            pl.BlockSpec((None, 1, D), lambda l, b: (l, 0, 0)),
            pl.BlockSpec((None, D, 2 * MEM_WIDTH), lambda l, b: (l, 0, 0)),
            pl.BlockSpec((None, 1, MEM_HEAD_DIM), lambda l, b: (l, 0, 0)),
        ],
        out_specs=[pl.BlockSpec((None, None, N, MEM_WIDTH), lambda l, b: (l, b, 0, 0))] * 2,
        out_shape=[out, out],
        compiler_params=_params("parallel", "parallel"),
        name="memkv",
    )(mem, norm_mem, wkv, k_gain)


def _xattn_kernel(x_ref, gx_ref, wq_ref, qg_ref, k_ref, v_ref, wo_ref, o_ref):
    x = x_ref[...]
    h = _rms(x, gx_ref[...]).astype(BF16)
    q = _dot(h, wq_ref[...])
    outs = []
    for hd in range(MEM_HEADS):
        sl = slice(hd * MEM_HEAD_DIM, (hd + 1) * MEM_HEAD_DIM)
        qn = _rms(q[:, sl], qg_ref[...]).astype(BF16)
        s = _dot_nt(qn, k_ref[:, sl]) * (MEM_HEAD_DIM ** -0.5)
        p = jnp.exp(s - jnp.max(s, axis=-1, keepdims=True))
        den = jnp.sum(p, axis=-1, keepdims=True)
        outs.append((_dot(p.astype(BF16), v_ref[:, sl]) / den).astype(BF16))
    o_ref[...] = x + _dot(jnp.concatenate(outs, axis=1), wo_ref[...])


def _xattn(x3, norm_x, wq, q_gain, kmem, vmem, wo, l):
    B, S, D = x3.shape
    N = kmem.shape[2]
    tm = min(TOKEN_TILE, S)
    return pl.pallas_call(
        _xattn_kernel,
        grid=(B, S // tm),
        in_specs=[
            pl.BlockSpec((None, tm, D), lambda b, i: (b, i, 0)),
            pl.BlockSpec((None, 1, D), lambda b, i: (l, 0, 0)),
            pl.BlockSpec((None, D, MEM_WIDTH), lambda b, i: (l, 0, 0)),
            pl.BlockSpec((None, 1, MEM_HEAD_DIM), lambda b, i: (l, 0, 0)),
            pl.BlockSpec((None, None, N, MEM_WIDTH), lambda b, i: (l, b, 0, 0)),
            pl.BlockSpec((None, None, N, MEM_WIDTH), lambda b, i: (l, b, 0, 0)),
            pl.BlockSpec((None, MEM_WIDTH, D), lambda b, i: (l, 0, 0)),
        ],
        out_specs=pl.BlockSpec((None, tm, D), lambda b, i: (b, i, 0)),
        out_shape=jax.ShapeDtypeStruct((B, S, D), F32),
        compiler_params=_params("parallel", "parallel"),
        name="xattn",
    )(x3, norm_x, wq, q_gain, kmem, vmem, wo)


def _swiglu_step(h, wg_ref, wu_ref, wd_ref):
    a = _dot(h, wg_ref[...])
    u = _dot(h, wu_ref[...])
    act = a / (1.0 + jnp.exp(-a)) * u
    return _dot(act.astype(BF16), wd_ref[...])


def _ffn_kernel(x_ref, g_ref, wg_ref, wu_ref, wd_ref, o_ref, h_ref):
    @pl.when(pl.program_id(1) == 0)
    def _():
        x = x_ref[...]
        h_ref[...] = _rms(x, g_ref[...]).astype(BF16)
        o_ref[...] = x

    o_ref[...] += _swiglu_step(h_ref[...], wg_ref, wu_ref, wd_ref)


def _ffn_dense(x2, g, wg, wu, wd, l, i_dense):
    T, D = x2.shape
    F = wg.shape[-1]
    tm, tf = min(FFN_TOKEN_TILE, T), FFN_COL_TILE
    return pl.pallas_call(
        _ffn_kernel,
        grid=(T // tm, F // tf),
        in_specs=[
            pl.BlockSpec((tm, D), lambda i, f: (i, 0)),
            pl.BlockSpec((None, 1, D), lambda i, f: (l, 0, 0)),
            pl.BlockSpec((None, D, tf), lambda i, f: (i_dense, 0, f)),
            pl.BlockSpec((None, D, tf), lambda i, f: (i_dense, 0, f)),
            pl.BlockSpec((None, tf, D), lambda i, f: (i_dense, f, 0)),
        ],
        out_specs=pl.BlockSpec((tm, D), lambda i, f: (i, 0)),
        out_shape=jax.ShapeDtypeStruct((T, D), F32),
        scratch_shapes=[pltpu.VMEM((tm, D), BF16)],
        compiler_params=_params("parallel", "arbitrary"),
        name="ffn_dense",
    )(x2, g, wg, wu, wd)


def _moe_ffn_kernel(be_ref, xs_ref, wg_ref, wu_ref, wd_ref, o_ref, h_ref):
    del be_ref

    @pl.when(pl.program_id(1) == 0)
    def _():
        h_ref[...] = xs_ref[...].astype(BF16)
        o_ref[...] = jnp.zeros_like(o_ref)

    o_ref[...] += _swiglu_step(h_ref[...], wg_ref, wu_ref, wd_ref)


def _moe_ffn(blk_e, xs, wg, wu, wd, i_moe):
    P, D = xs.shape
    F = wg.shape[-1]
    tm, tf = MOE_ROW_TILE, FFN_COL_TILE
    return pl.pallas_call(
        _moe_ffn_kernel,
        grid_spec=pltpu.PrefetchScalarGridSpec(
            num_scalar_prefetch=1,
            grid=(P // tm, F // tf),
            in_specs=[
                pl.BlockSpec((tm, D), lambda i, f, be: (i, 0)),
                pl.BlockSpec((None, None, D, tf), lambda i, f, be: (i_moe, be[i], 0, f)),
                pl.BlockSpec((None, None, D, tf), lambda i, f, be: (i_moe, be[i], 0, f)),
                pl.BlockSpec((None, None, tf, D), lambda i, f, be: (i_moe, be[i], f, 0)),
            ],
            out_specs=pl.BlockSpec((tm, D), lambda i, f, be: (i, 0)),
            scratch_shapes=[pltpu.VMEM((tm, D), BF16)],
        ),
        out_shape=jax.ShapeDtypeStruct((P, D), F32),
        compiler_params=_params("parallel", "arbitrary"),
        name="moe_ffn",
    )(blk_e, xs, wg, wu, wd)


def _router_kernel(x_ref, g_ref, rw_ref, rb_ref, hf_ref, ei_ref, gt_ref, cnt_ref, carry_ref):
    tm = x_ref.shape[0]

    @pl.when(pl.program_id(0) == 0)
    def _():
        carry_ref[...] = jnp.zeros_like(carry_ref)

    hf = _rms(x_ref[...], g_ref[...])
    hf_ref[...] = hf
    lane = lax.broadcasted_iota(jnp.int32, (tm, LANES), 1)
    logits = jnp.dot(hf, rw_ref[...], precision=lax.Precision.HIGHEST, preferred_element_type=F32) + rb_ref[...]
    logits = jnp.where(lane < N_EXPERTS, logits, -jnp.inf)
    m1 = jnp.max(logits, axis=-1, keepdims=True)
    i1 = jnp.min(jnp.where(logits == m1, lane, LANES), axis=-1, keepdims=True)
    rest = jnp.where(lane == i1, -jnp.inf, logits)
    m2 = jnp.max(rest, axis=-1, keepdims=True)
    i2 = jnp.min(jnp.where(rest == m2, lane, LANES), axis=-1, keepdims=True)
    e = jnp.exp(m2 - m1)
    g1 = 1.0 / (1.0 + e)
    g2 = e / (1.0 + e)
    oh1 = lane == i1
    oh2 = lane == i2
    oh = (oh1 | oh2).astype(F32)
    row = lax.broadcasted_iota(jnp.int32, (tm, tm), 0)
    col = lax.broadcasted_iota(jnp.int32, (tm, tm), 1)
    earlier = (row > col).astype(BF16)
    before = _dot(earlier, oh.astype(BF16)) + carry_ref[...]
    r1 = jnp.sum(jnp.where(oh1, before, 0.0), axis=-1, keepdims=True).astype(jnp.int32)
    r2 = jnp.sum(jnp.where(oh2, before, 0.0), axis=-1, keepdims=True).astype(jnp.int32)
    carry_ref[...] += jnp.sum(oh, axis=0, keepdims=True)
    cnt_ref[...] = carry_ref[...]
    ei = jnp.where(lane == 0, i1, jnp.where(lane == 1, i2, jnp.where(lane == 2, r1, jnp.where(lane == 3, r2, 0))))
    ei_ref[...] = ei[:, :8]
    gt = jnp.where(lane == 0, g1, jnp.where(lane == 1, g2, 0.0))
    gt_ref[...] = gt[:, :8]


def _router(x2, g, rw, rb, l, i_moe):
    T, D = x2.shape
    tm = TOKEN_TILE
    return pl.pallas_call(
        _router_kernel,
        grid=(T // tm,),
        in_specs=[
            pl.BlockSpec((tm, D), lambda i: (i, 0)),
            pl.BlockSpec((None, 1, D), lambda i: (l, 0, 0)),
            pl.BlockSpec((None, D, LANES), lambda i: (i_moe, 0, 0)),
            pl.BlockSpec((None, 1, LANES), lambda i: (i_moe, 0, 0)),
        ],
        out_specs=[
            pl.BlockSpec((tm, D), lambda i: (i, 0)),
            pl.BlockSpec((tm, 8), lambda i: (i, 0)),
            pl.BlockSpec((tm, 8), lambda i: (i, 0)),
            pl.BlockSpec((1, LANES), lambda i: (0, 0)),
        ],
        out_shape=[
            jax.ShapeDtypeStruct((T, D), F32),
            jax.ShapeDtypeStruct((T, 8), jnp.int32),
            jax.ShapeDtypeStruct((T, 8), F32),
            jax.ShapeDtypeStruct((1, LANES), F32),
        ],
        scratch_shapes=[pltpu.VMEM((1, LANES), F32)],
        compiler_params=_params("arbitrary"),
        name="router",
    )(x2, g, rw, rb)


def _dispatch_kernel(dest_ref, hf_ref, xs_in_ref, xs_ref, sem):
    del xs_in_ref
    tm = hf_ref.shape[0]

    def row_copy(r, d):
        return pltpu.make_async_copy(hf_ref.at[pl.ds(r, 1)], xs_ref.at[pl.ds(d, 1)], sem)

    def issue(r, _):
        row_copy(r, dest_ref[2 * r]).start()
        row_copy(r, dest_ref[2 * r + 1]).start()
        return 0

    lax.fori_loop(0, tm, issue, 0, unroll=DMA_ISSUE_UNROLL)
    for _ in range(2):
        pltpu.make_async_copy(hf_ref, xs_ref.at[pl.ds(0, tm)], sem).wait()


def _dispatch(dest, hf, xs_zero):
    T, D = hf.shape
    tm = TOKEN_TILE
    return pl.pallas_call(
        _dispatch_kernel,
        grid=(T // tm,),
        in_specs=[
            pl.BlockSpec((2 * tm,), lambda i: (i,), memory_space=pltpu.SMEM),
            pl.BlockSpec((tm, D), lambda i: (i, 0)),
            pl.BlockSpec(memory_space=pl.ANY),
        ],
        out_specs=pl.BlockSpec(memory_space=pl.ANY),
        out_shape=jax.ShapeDtypeStruct(xs_zero.shape, xs_zero.dtype),
        scratch_shapes=[pltpu.SemaphoreType.DMA(())],
        input_output_aliases={2: 0},
        compiler_params=_params("arbitrary"),
        name="moe_dispatch",
    )(dest, hf, xs_zero)


def _combine_kernel(dest_ref, x_ref, gt_ref, y_ref, o_ref, buf_ref, sem):
    tm = x_ref.shape[0]

    def row_copy(slot, r, d):
        return pltpu.make_async_copy(y_ref.at[pl.ds(d, 1)], buf_ref.at[slot, pl.ds(r, 1)], sem)

    def issue(r, _):
        row_copy(0, r, dest_ref[2 * r]).start()
        row_copy(1, r, dest_ref[2 * r + 1]).start()
        return 0

    lax.fori_loop(0, tm, issue, 0, unroll=DMA_ISSUE_UNROLL)
    for slot in range(2):
        pltpu.make_async_copy(y_ref.at[pl.ds(0, tm)], buf_ref.at[slot], sem).wait()
    gt = gt_ref[...]
    o_ref[...] = x_ref[...] + gt[:, 0:1] * buf_ref[0] + gt[:, 1:2] * buf_ref[1]


def _combine(dest, x2, gates, y):
    T, D = x2.shape
    tm = TOKEN_TILE
    return pl.pallas_call(
        _combine_kernel,
        grid=(T // tm,),
        in_specs=[
            pl.BlockSpec((2 * tm,), lambda i: (i,), memory_space=pltpu.SMEM),
            pl.BlockSpec((tm, D), lambda i: (i, 0)),
            pl.BlockSpec((tm, 8), lambda i: (i, 0)),
            pl.BlockSpec(memory_space=pl.ANY),
        ],
        out_specs=pl.BlockSpec((tm, D), lambda i: (i, 0)),
        out_shape=jax.ShapeDtypeStruct((T, D), F32),
        scratch_shapes=[pltpu.VMEM((2, tm, D), F32), pltpu.SemaphoreType.DMA(())],
        compiler_params=_params("arbitrary"),
        name="moe_combine",
    )(dest, x2, gates, y)


def _moe(x2, g, rw, rb, wg, wu, wd, l, i_moe):
    T, D = x2.shape
    tmm = MOE_ROW_TILE
    P = 2 * T + N_EXPERTS * tmm
    hf, ei, gates, cnt = _router(x2, g, rw, rb, l, i_moe)
    counts = cnt[0, :N_EXPERTS].astype(jnp.int32)
    pcounts = (counts + tmm - 1) // tmm * tmm
    pends = jnp.cumsum(pcounts)
    pstarts = pends - pcounts
    dest = (jnp.take(pstarts, ei[:, 0:2]) + ei[:, 2:4]).reshape(-1)
    blk_e = jnp.minimum(jnp.searchsorted(pends, jnp.arange(P // tmm, dtype=jnp.int32) * tmm, side="right"),
                        N_EXPERTS - 1).astype(jnp.int32)
    xs = _dispatch(dest, hf, jnp.zeros((P, D), F32))
    y = _moe_ffn(blk_e, xs, wg, wu, wd, i_moe)
    return _combine(dest, x2, gates, y)


def _t5_buckets(dist):
    n = np.maximum(dist, 0)
    max_exact = N_BUCKETS // 2
    large = max_exact + (np.log(np.maximum(n, 1) / max_exact) / np.log(MAX_DISTANCE / max_exact)
                         * (N_BUCKETS - max_exact)).astype(np.int32)
    large = np.minimum(large, N_BUCKETS - 1)
    return np.where(n < max_exact, n, large).astype(np.int32)


def kernel(x, mem, norm_mix, w_in, sb_out_gain, swa_q_gain, swa_k_gain, swa_sinks, swa_out_gain, rel_bias, w_out, norm_xattn, norm_mem, xattn_wq, xattn_wkv, xattn_q_gain, xattn_k_gain, xattn_wo, norm_ffn, dense_w_gate, dense_w_up, dense_w_down, router_w, router_b, exp_w_gate, exp_w_up, exp_w_down):
    B, S, D = x.shape
    depth = w_in.shape[0]
    T = B * S
    row3 = lambda a: a.reshape(a.shape[0], 1, a.shape[1])
    bf = lambda a: a.astype(BF16)

    dist = WINDOW + np.arange(WINDOW)[:, None] - np.arange(2 * WINDOW)[None, :]
    swa_bias = jnp.transpose(rel_bias[_t5_buckets(dist)], (2, 0, 1))
    band = (dist >= 0) & (dist < WINDOW)
    band = np.stack([band & (np.arange(2 * WINDOW)[None, :] >= WINDOW), band])
    swa_bias = jnp.where(band[:, None], swa_bias[None], -jnp.inf)
    router_w_p = jnp.pad(router_w, ((0, 0), (0, 0), (0, LANES - N_EXPERTS)))
    router_b_p = row3(jnp.pad(router_b, ((0, 0), (0, LANES - N_EXPERTS))))

    w_in_b, w_out_b = bf(w_in), bf(w_out)
    wq_b, wkv_b, wo_b = bf(xattn_wq), bf(xattn_wkv), bf(xattn_wo)
    dg_b, du_b, dd_b = bf(dense_w_gate), bf(dense_w_up), bf(dense_w_down)
    eg_b, eu_b, ed_b = bf(exp_w_gate), bf(exp_w_up), bf(exp_w_down)
    norm_mix3, norm_x3, norm_f3 = row3(norm_mix), row3(norm_xattn), row3(norm_ffn)
    sb_g3, sw_g3 = row3(sb_out_gain), row3(swa_out_gain)
    xq_g3, xk_g3 = row3(xattn_q_gain), row3(xattn_k_gain)

    kmem, vmem = _memkv(mem, row3(norm_mem), wkv_b, xk_g3)

    x2 = x.reshape(T, D)
    for l in range(depth):
        proj = _inproj(x2, norm_mix3, w_in_b, l).reshape(B, S, -1)
        sb_o = _sb_attention(proj)
        sw_o = _swa_attention(proj, swa_sinks[l], jnp.tile(swa_q_gain[l], SWA_Q_HEADS)[None, :],
                              jnp.tile(swa_k_gain[l], SWA_KV_HEADS)[None, :], swa_bias)
        x2 = _outproj(sb_o.reshape(T, -1), sw_o.reshape(T, -1), sb_g3, sw_g3, w_out_b, x2, l)
        x2 = _xattn(x2.reshape(B, S, D), norm_x3, wq_b, xq_g3, kmem, vmem, wo_b, l).reshape(T, D)
        if l % 2 == 0:
            x2 = _ffn_dense(x2, norm_f3, dg_b, du_b, dd_b, l, l // 2)
        else:
            x2 = _moe(x2, norm_f3, router_w_p, router_b_p, eg_b, eu_b, ed_b, l, l // 2)
    return x2.reshape(B, S, D)
```

```python
import functools

import numpy as np
import jax
import jax.numpy as jnp
from jax import lax
from jax.experimental import pallas as pl
from jax.experimental.pallas import tpu as pltpu

F32 = jnp.float32
BF16 = jnp.bfloat16

HEAD_DIM = 64
SB_WIDTH = 512
SWA_WIDTH = 512
SWA_Q_HEADS = 8
SWA_GROUP = 4
SWA_KV_HEADS = 2
WINDOW = 128
N_BUCKETS = 32
MAX_DISTANCE = 128
MEM_HEADS = 4
MEM_HEAD_DIM = 128
MEM_WIDTH = 512
N_EXPERTS = 8
EPS = 1e-6
LANES = 128

VMEM_LIMIT = 56 * 1024 * 1024

TOKEN_TILE = 512
FFN_TOKEN_TILE = 1024
FFN_COL_TILE = 512
SB_TILE = 256
MOE_ROW_TILE = 1024
DMA_ISSUE_UNROLL = 8
SB_LOG_UNDERFLOW = -104.0


def _params(*sem):
    return pltpu.CompilerParams(dimension_semantics=("arbitrary",) * len(sem), vmem_limit_bytes=VMEM_LIMIT)


def _rms(x, g):
    return x * lax.rsqrt(jnp.mean(x * x, axis=-1, keepdims=True) + EPS) * g


def _dot(a, b):
    return jnp.dot(a, b, preferred_element_type=F32)


def _dot_nt(a, b):
    return lax.dot_general(a, b, (((1,), (1,)), ((), ())), preferred_element_type=F32)


def _inproj_kernel(x_ref, g_ref, w_ref, o_ref, *, col_tile):
    h = _rms(x_ref[...], g_ref[...]).astype(BF16)
    for c in range(w_ref.shape[1] // col_tile):
        sl = slice(c * col_tile, (c + 1) * col_tile)
        o_ref[:, sl] = _dot(h, w_ref[:, sl]).astype(BF16)


def _inproj(x2, g, w, l):
    T, D = x2.shape
    N = w.shape[-1]
    tm = TOKEN_TILE
    return pl.pallas_call(
        functools.partial(_inproj_kernel, col_tile=N // 3),
        grid=(T // tm,),
        in_specs=[
            pl.BlockSpec((tm, D), lambda i: (i, 0)),
            pl.BlockSpec((None, 1, D), lambda i: (l, 0, 0)),
            pl.BlockSpec((None, D, N), lambda i: (l, 0, 0)),
        ],
        out_specs=pl.BlockSpec((tm, N), lambda i: (i, 0)),
        out_shape=jax.ShapeDtypeStruct((T, N), BF16),
        compiler_params=_params("parallel"),
        name="inproj",
    )(x2, g, w)


def _sb_kernel(q_ref, k_ref, v_ref, suf_ref, o_ref, kt_ref, z_ref, acc_ref, c_ref, *, tile):
    S = q_ref.shape[0]
    lane = lax.broadcasted_iota(jnp.int32, (1, LANES), 1)
    lo_half = lane < HEAD_DIM
    row = lax.broadcasted_iota(jnp.int32, (tile, tile), 0)
    col = lax.broadcasted_iota(jnp.int32, (tile, tile), 1)
    strict = col < row
    suffix = suf_ref[...]

    for j in range(S // tile):
        kt_ref[j] = k_ref[j * tile:(j + 1) * tile, :].T

    def q_block(i, _):
        q0 = pl.multiple_of(i * tile, tile)
        q2 = q_ref[pl.ds(q0, tile), :] * jnp.asarray(HEAD_DIM ** -0.5, BF16)
        zero = jnp.zeros_like(q2)
        qs = jnp.concatenate([jnp.where(lo_half, q2, zero), jnp.where(lo_half, zero, q2)], axis=0)
        acc_ref[...] = jnp.zeros_like(acc_ref)
        c_ref[...] = jnp.zeros_like(c_ref)

        heads = range(2)
        sl = [slice(h * tile, (h + 1) * tile) for h in heads]

        def scores(j, slot):
            kt = kt_ref[j]
            for h in heads:
                z_ref[slot, sl[h]] = _dot(qs[sl[h]], kt)

        def key_tile(j, slot, masked):
            z = [z_ref[slot, sl[h]] for h in heads]
            scores(jnp.maximum(j - 1, 0), 1 - slot)
            v2 = v_ref[pl.ds(pl.multiple_of(j * tile, tile), tile), :]
            log_beta = [jnp.minimum(z[h], 0.0) - jnp.log(1.0 + jnp.exp(-jnp.abs(z[h]))) for h in heads]
            log_om = [log_beta[h] - z[h] for h in heads]
            if masked:
                log_om = [jnp.where(strict, log_om[h], 0.0) for h in heads]
            hi = [log_om[h].astype(BF16) for h in heads]
            lo = [(log_om[h] - hi[h].astype(F32)).astype(BF16) for h in heads]
            tail = [_dot(jnp.concatenate([hi[h], lo[h]], axis=0), suffix) for h in heads]
            c = [c_ref[sl[h]] for h in heads]
            a = [jnp.exp(log_beta[h] + (tail[h][:tile] + tail[h][tile:]) + c[h]) for h in heads]
            if masked:
                a = [jnp.where(strict, a[h], 0.0) for h in heads]
            pv = [_dot(a[h].astype(BF16), v2) for h in heads]
            for h in heads:
                acc_ref[sl[h]] += pv[h]
                c_ref[sl[h]] = c[h] + jnp.sum(log_om[h], axis=-1, keepdims=True)

        scores(i, 0)
        key_tile(i, 0, True)

        def off_diag(state):
            j, slot, _ = state
            key_tile(j, slot, False)
            return j - 1, 1 - slot, jnp.max(c_ref[...])

        lax.while_loop(lambda s: (s[0] >= 0) & (s[2] > SB_LOG_UNDERFLOW), off_diag,
                       (i - 1, jnp.int32(1), jnp.float32(0.0)))
        o_ref[pl.ds(q0, tile), :] = jnp.where(lo_half, acc_ref[:tile], acc_ref[tile:])
        return 0

    lax.fori_loop(0, S // tile, q_block, 0)


def _sb_attention(proj3):
    B, S, _ = proj3.shape
    pairs = SB_WIDTH // LANES
    tile = min(SB_TILE, S)
    suf = jnp.asarray(np.arange(tile)[:, None] > np.arange(tile)[None, :], BF16)
    spec = lambda off: pl.BlockSpec((None, S, LANES), lambda b, p: (b, 0, off + p))
    return pl.pallas_call(
        functools.partial(_sb_kernel, tile=tile),
        grid=(B, pairs),
        in_specs=[spec(0), spec(pairs), spec(2 * pairs),
                  pl.BlockSpec((tile, tile), lambda b, p: (0, 0))],
        out_specs=pl.BlockSpec((None, S, LANES), lambda b, p: (b, 0, p)),
        out_shape=jax.ShapeDtypeStruct((B, S, SB_WIDTH), F32),
        scratch_shapes=[pltpu.VMEM((S // tile, LANES, tile), BF16),
                        pltpu.VMEM((2, 2 * tile, tile), F32),
                        pltpu.VMEM((2 * tile, LANES), F32), pltpu.VMEM((2 * tile, 1), F32)],
        compiler_params=_params("parallel", "parallel"),
        name="sb_attention",
    )(proj3, proj3, proj3, suf)


def _group_mean_sq(x, ones_ref):
    sq = x * x
    hi = sq.astype(BF16)
    lo = (sq - hi.astype(F32)).astype(BF16)
    return _dot(hi, ones_ref[...]) + _dot(lo, ones_ref[...])


def _swa_kernel(sink_ref, q_ref, kp_ref, kc_ref, vp_ref, vc_ref, qg_ref, kg_ref, gq_ref, gk_ref, sel_ref,
                bias_ref, o_ref):
    q = q_ref[...].astype(F32)
    k = jnp.concatenate([kp_ref[...], kc_ref[...]], axis=0).astype(F32)
    v = jnp.concatenate([vp_ref[...], vc_ref[...]], axis=0)
    half = SWA_WIDTH // 2
    q_ms = jnp.concatenate([_group_mean_sq(q[:, :half], gq_ref), _group_mean_sq(q[:, half:], gq_ref)], axis=1)
    qn = (q * lax.rsqrt(q_ms + EPS) * qg_ref[...] * (HEAD_DIM ** -0.5)).astype(BF16)
    kn_t = (k * lax.rsqrt(_group_mean_sq(k, gk_ref) + EPS) * kg_ref[...]).astype(BF16).T
    lane = lax.broadcasted_iota(jnp.int32, (1, LANES), 1)
    lo_half = lane < HEAD_DIM
    zero = jnp.zeros((WINDOW, LANES), BF16)
    W = WINDOW
    head_row = lax.broadcasted_iota(jnp.int32, (SWA_GROUP * W, 1), 0) // W
    groups = range(SWA_KV_HEADS)
    kt_g = [kn_t[g * HEAD_DIM:(g + 1) * HEAD_DIM, :] for g in groups]
    kt_dup = [jnp.concatenate([kt_g[g], kt_g[g]], axis=0) for g in groups]
    v_dup = [_dot(v, sel_ref[g]).astype(BF16) for g in groups]
    lhs = []
    for g in groups:
        rows = []
        for t in range(g * SWA_GROUP // 2, (g + 1) * SWA_GROUP // 2):
            qt = qn[:, t * LANES:(t + 1) * LANES]
            rows += [jnp.where(lo_half, qt, zero), jnp.where(lo_half, zero, qt)]
        lhs.append(jnp.concatenate(rows, axis=0))
    s = [_dot(lhs[g], kt_dup[g]) for g in groups]
    s = [s[g] + bias_ref[g * SWA_GROUP:(g + 1) * SWA_GROUP].reshape(SWA_GROUP * W, 2 * W) for g in groups]
    sink = []
    for g in groups:
        sk = jnp.zeros((SWA_GROUP * W, 1), F32)
        for hh in range(SWA_GROUP):
            sk = jnp.where(head_row == hh, sink_ref[g * SWA_GROUP + hh], sk)
        sink.append(sk)
    m = [jnp.maximum(jnp.max(s[g], axis=-1, keepdims=True), sink[g]) for g in groups]
    p = [jnp.exp(s[g] - m[g]) for g in groups]
    den = [jnp.sum(p[g], axis=-1, keepdims=True) + jnp.exp(sink[g] - m[g]) for g in groups]
    r = [_dot(p[g].astype(BF16), v_dup[g]) / den[g] for g in groups]
    for g in groups:
        for tt in range(SWA_GROUP // 2):
            t = g * SWA_GROUP // 2 + tt
            o_ref[:, t * LANES:(t + 1) * LANES] = jnp.where(
                lo_half, r[g][2 * tt * W:(2 * tt + 1) * W], r[g][(2 * tt + 1) * W:(2 * tt + 2) * W])


def _swa_attention(proj3, sinks, q_gain, k_gain, bias):
    B, S, _ = proj3.shape
    W = WINDOW
    q_blk = (3 * SB_WIDTH) // SWA_WIDTH
    k_blk = (3 * SB_WIDTH + SWA_WIDTH) // LANES
    v_blk = k_blk + 1
    group = lambda n: (np.arange(n)[:, None] // HEAD_DIM == np.arange(n)[None, :] // HEAD_DIM) / HEAD_DIM
    ones_q = jnp.asarray(group(SWA_WIDTH // 2), BF16)
    ones_k = jnp.asarray(group(LANES), BF16)
    sel = jnp.asarray(np.stack([np.arange(LANES)[:, None] == g * HEAD_DIM + np.arange(LANES)[None, :] % HEAD_DIM
                                for g in range(SWA_KV_HEADS)]), BF16)
    prev = lambda c: pl.BlockSpec((None, W, LANES), lambda b, n: (b, jnp.maximum(n - 1, 0), c))
    cur = lambda c: pl.BlockSpec((None, W, LANES), lambda b, n: (b, n, c))
    const = lambda *shape: pl.BlockSpec(shape, lambda b, n: (0,) * len(shape))
    return pl.pallas_call(
        _swa_kernel,
        grid=(B, S // W),
        in_specs=[
            pl.BlockSpec(memory_space=pltpu.SMEM),
            pl.BlockSpec((None, W, SWA_WIDTH), lambda b, n: (b, n, q_blk)),
            prev(k_blk), cur(k_blk), prev(v_blk), cur(v_blk),
            const(1, SWA_WIDTH), const(1, LANES),
            const(SWA_WIDTH // 2, SWA_WIDTH // 2), const(LANES, LANES), const(SWA_KV_HEADS, LANES, LANES),
            pl.BlockSpec((None, SWA_Q_HEADS, W, 2 * W), lambda b, n: (jnp.minimum(n, 1), 0, 0, 0)),
        ],
        out_specs=pl.BlockSpec((None, W, SWA_WIDTH), lambda b, n: (b, n, 0)),
        out_shape=jax.ShapeDtypeStruct((B, S, SWA_WIDTH), F32),
        compiler_params=_params("arbitrary", "arbitrary"),
        name="swa_attention",
    )(sinks, proj3, proj3, proj3, proj3, proj3, q_gain, k_gain, ones_q, ones_k, sel, bias)


def _outproj_kernel(sb_ref, sw_ref, gsb_ref, gsw_ref, w_ref, x_ref, o_ref):
    a = _rms(sb_ref[...], gsb_ref[...]).astype(BF16)
    b = _rms(sw_ref[...], gsw_ref[...]).astype(BF16)
    o_ref[...] = x_ref[...] + _dot(a, w_ref[:SB_WIDTH, :]) + _dot(b, w_ref[SB_WIDTH:, :])


def _outproj(sb_o, sw_o, g_sb, g_sw, w, x2, l):
    T, D = x2.shape
    tm = TOKEN_TILE
    return pl.pallas_call(
        _outproj_kernel,
        grid=(T // tm,),
        in_specs=[
            pl.BlockSpec((tm, SB_WIDTH), lambda i: (i, 0)),
            pl.BlockSpec((tm, SWA_WIDTH), lambda i: (i, 0)),
            pl.BlockSpec((None, 1, SB_WIDTH), lambda i: (l, 0, 0)),
            pl.BlockSpec((None, 1, SWA_WIDTH), lambda i: (l, 0, 0)),
            pl.BlockSpec((None, SB_WIDTH + SWA_WIDTH, D), lambda i: (l, 0, 0)),
            pl.BlockSpec((tm, D), lambda i: (i, 0)),
        ],
        out_specs=pl.BlockSpec((tm, D), lambda i: (i, 0)),
        out_shape=jax.ShapeDtypeStruct((T, D), F32),
        compiler_params=_params("parallel"),
        name="outproj",
    )(sb_o, sw_o, g_sb, g_sw, w, x2)


def _memkv_kernel(m_ref, g_ref, w_ref, kg_ref, k_ref, v_ref):
    h = _rms(m_ref[...], g_ref[...]).astype(BF16)
    kv = _dot(h, w_ref[...])
    for hd in range(MEM_HEADS):
        sl = slice(hd * MEM_HEAD_DIM, (hd + 1) * MEM_HEAD_DIM)
        k_ref[:, sl] = _rms(kv[:, sl], kg_ref[...]).astype(BF16)
    v_ref[...] = kv[:, MEM_WIDTH:].astype(BF16)


def _memkv(mem, norm_mem, wkv, k_gain):
    B, N, D = mem.shape
    L = wkv.shape[0]
    out = jax.ShapeDtypeStruct((L, B, N, MEM_WIDTH), BF16)
    return pl.pallas_call(
        _memkv_kernel,
        grid=(L, B),
        in_specs=[
            pl.BlockSpec((None, N, D), lambda l, b: (b, 0, 0)),
            pl.BlockSpec((None, 1, D), lambda l, b: (l, 0, 0)),
            pl.BlockSpec((None, D, 2 * MEM_WIDTH), lambda l, b: (l, 0, 0)),
            pl.BlockSpec((None, 1, MEM_HEAD_DIM), lambda l, b: (l, 0, 0)),
        ],
        out_specs=[pl.BlockSpec((None, None, N, MEM_WIDTH), lambda l, b: (l, b, 0, 0))] * 2,
        out_shape=[out, out],
        compiler_params=_params("parallel", "parallel"),
        name="memkv",
    )(mem, norm_mem, wkv, k_gain)


def _xattn_kernel(x_ref, gx_ref, wq_ref, qg_ref, k_ref, v_ref, wo_ref, o_ref):
    x = x_ref[...]
    h = _rms(x, gx_ref[...]).astype(BF16)
    q = _dot(h, wq_ref[...])
    outs = []
    for hd in range(MEM_HEADS):
        sl = slice(hd * MEM_HEAD_DIM, (hd + 1) * MEM_HEAD_DIM)
        qn = _rms(q[:, sl], qg_ref[...]).astype(BF16)
        s = _dot_nt(qn, k_ref[:, sl]) * (MEM_HEAD_DIM ** -0.5)
        p = jnp.exp(s - jnp.max(s, axis=-1, keepdims=True))
        den = jnp.sum(p, axis=-1, keepdims=True)
        outs.append((_dot(p.astype(BF16), v_ref[:, sl]) / den).astype(BF16))
    o_ref[...] = x + _dot(jnp.concatenate(outs, axis=1), wo_ref[...])


def _xattn(x3, norm_x, wq, q_gain, kmem, vmem, wo, l):
    B, S, D = x3.shape
    N = kmem.shape[2]
    tm = min(TOKEN_TILE, S)
    return pl.pallas_call(
        _xattn_kernel,
        grid=(B, S // tm),
        in_specs=[
            pl.BlockSpec((None, tm, D), lambda b, i: (b, i, 0)),
            pl.BlockSpec((None, 1, D), lambda b, i: (l, 0, 0)),
            pl.BlockSpec((None, D, MEM_WIDTH), lambda b, i: (l, 0, 0)),
            pl.BlockSpec((None, 1, MEM_HEAD_DIM), lambda b, i: (l, 0, 0)),
            pl.BlockSpec((None, None, N, MEM_WIDTH), lambda b, i: (l, b, 0, 0)),
            pl.BlockSpec((None, None, N, MEM_WIDTH), lambda b, i: (l, b, 0, 0)),
            pl.BlockSpec((None, MEM_WIDTH, D), lambda b, i: (l, 0, 0)),
        ],
        out_specs=pl.BlockSpec((None, tm, D), lambda b, i: (b, i, 0)),
        out_shape=jax.ShapeDtypeStruct((B, S, D), F32),
        compiler_params=_params("parallel", "parallel"),
        name="xattn",
    )(x3, norm_x, wq, q_gain, kmem, vmem, wo)


def _swiglu_step(h, wg_ref, wu_ref, wd_ref):
    a = _dot(h, wg_ref[...])
    u = _dot(h, wu_ref[...])
    act = a / (1.0 + jnp.exp(-a)) * u
    return _dot(act.astype(BF16), wd_ref[...])


def _ffn_kernel(x_ref, g_ref, wg_ref, wu_ref, wd_ref, o_ref, h_ref):
    @pl.when(pl.program_id(1) == 0)
    def _():
        x = x_ref[...]
        h_ref[...] = _rms(x, g_ref[...]).astype(BF16)
        o_ref[...] = x

    o_ref[...] += _swiglu_step(h_ref[...], wg_ref, wu_ref, wd_ref)


def _ffn_dense(x2, g, wg, wu, wd, l, i_dense):
    T, D = x2.shape
    F = wg.shape[-1]
    tm, tf = min(FFN_TOKEN_TILE, T), FFN_COL_TILE
    return pl.pallas_call(
        _ffn_kernel,
        grid=(T // tm, F // tf),
        in_specs=[
            pl.BlockSpec((tm, D), lambda i, f: (i, 0)),
            pl.BlockSpec((None, 1, D), lambda i, f: (l, 0, 0)),
            pl.BlockSpec((None, D, tf), lambda i, f: (i_dense, 0, f)),
            pl.BlockSpec((None, D, tf), lambda i, f: (i_dense, 0, f)),
            pl.BlockSpec((None, tf, D), lambda i, f: (i_dense, f, 0)),
        ],
        out_specs=pl.BlockSpec((tm, D), lambda i, f: (i, 0)),
        out_shape=jax.ShapeDtypeStruct((T, D), F32),
        scratch_shapes=[pltpu.VMEM((tm, D), BF16)],
        compiler_params=_params("parallel", "arbitrary"),
        name="ffn_dense",
    )(x2, g, wg, wu, wd)


def _moe_ffn_kernel(be_ref, xs_ref, wg_ref, wu_ref, wd_ref, o_ref, h_ref):
    del be_ref

    @pl.when(pl.program_id(1) == 0)
    def _():
        h_ref[...] = xs_ref[...].astype(BF16)
        o_ref[...] = jnp.zeros_like(o_ref)

    o_ref[...] += _swiglu_step(h_ref[...], wg_ref, wu_ref, wd_ref)


def _moe_ffn(blk_e, xs, wg, wu, wd, i_moe):
    P, D = xs.shape
    F = wg.shape[-1]
    tm, tf = MOE_ROW_TILE, FFN_COL_TILE
    return pl.pallas_call(
        _moe_ffn_kernel,
        grid_spec=pltpu.PrefetchScalarGridSpec(
            num_scalar_prefetch=1,
            grid=(P // tm, F // tf),
            in_specs=[
                pl.BlockSpec((tm, D), lambda i, f, be: (i, 0)),
                pl.BlockSpec((None, None, D, tf), lambda i, f, be: (i_moe, be[i], 0, f)),
                pl.BlockSpec((None, None, D, tf), lambda i, f, be: (i_moe, be[i], 0, f)),
                pl.BlockSpec((None, None, tf, D), lambda i, f, be: (i_moe, be[i], f, 0)),
            ],
            out_specs=pl.BlockSpec((tm, D), lambda i, f, be: (i, 0)),
            scratch_shapes=[pltpu.VMEM((tm, D), BF16)],
        ),
        out_shape=jax.ShapeDtypeStruct((P, D), F32),
        compiler_params=_params("parallel", "arbitrary"),
        name="moe_ffn",
    )(blk_e, xs, wg, wu, wd)


def _router_kernel(x_ref, g_ref, rw_ref, rb_ref, hf_ref, ei_ref, gt_ref, cnt_ref, carry_ref):
    tm = x_ref.shape[0]

    @pl.when(pl.program_id(0) == 0)
    def _():
        carry_ref[...] = jnp.zeros_like(carry_ref)

    hf = _rms(x_ref[...], g_ref[...])
    hf_ref[...] = hf
    lane = lax.broadcasted_iota(jnp.int32, (tm, LANES), 1)
    logits = jnp.dot(hf, rw_ref[...], precision=lax.Precision.HIGHEST, preferred_element_type=F32) + rb_ref[...]
    logits = jnp.where(lane < N_EXPERTS, logits, -jnp.inf)
    m1 = jnp.max(logits, axis=-1, keepdims=True)
    i1 = jnp.min(jnp.where(logits == m1, lane, LANES), axis=-1, keepdims=True)
    rest = jnp.where(lane == i1, -jnp.inf, logits)
    m2 = jnp.max(rest, axis=-1, keepdims=True)
    i2 = jnp.min(jnp.where(rest == m2, lane, LANES), axis=-1, keepdims=True)
    e = jnp.exp(m2 - m1)
    g1 = 1.0 / (1.0 + e)
    g2 = e / (1.0 + e)
    oh1 = lane == i1
    oh2 = lane == i2
    oh = (oh1 | oh2).astype(F32)
    row = lax.broadcasted_iota(jnp.int32, (tm, tm), 0)
    col = lax.broadcasted_iota(jnp.int32, (tm, tm), 1)
    earlier = (row > col).astype(BF16)
    before = _dot(earlier, oh.astype(BF16)) + carry_ref[...]
    r1 = jnp.sum(jnp.where(oh1, before, 0.0), axis=-1, keepdims=True).astype(jnp.int32)
    r2 = jnp.sum(jnp.where(oh2, before, 0.0), axis=-1, keepdims=True).astype(jnp.int32)
    carry_ref[...] += jnp.sum(oh, axis=0, keepdims=True)
    cnt_ref[...] = carry_ref[...]
    ei = jnp.where(lane == 0, i1, jnp.where(lane == 1, i2, jnp.where(lane == 2, r1, jnp.where(lane == 3, r2, 0))))
    ei_ref[...] = ei[:, :8]
    gt = jnp.where(lane == 0, g1, jnp.where(lane == 1, g2, 0.0))
    gt_ref[...] = gt[:, :8]


def _router(x2, g, rw, rb, l, i_moe):
    T, D = x2.shape
    tm = TOKEN_TILE
    return pl.pallas_call(
        _router_kernel,
        grid=(T // tm,),
        in_specs=[
            pl.BlockSpec((tm, D), lambda i: (i, 0)),
            pl.BlockSpec((None, 1, D), lambda i: (l, 0, 0)),
            pl.BlockSpec((None, D, LANES), lambda i: (i_moe, 0, 0)),
            pl.BlockSpec((None, 1, LANES), lambda i: (i_moe, 0, 0)),
        ],
        out_specs=[
            pl.BlockSpec((tm, D), lambda i: (i, 0)),
            pl.BlockSpec((tm, 8), lambda i: (i, 0)),
            pl.BlockSpec((tm, 8), lambda i: (i, 0)),
            pl.BlockSpec((1, LANES), lambda i: (0, 0)),
        ],
        out_shape=[
            jax.ShapeDtypeStruct((T, D), F32),
            jax.ShapeDtypeStruct((T, 8), jnp.int32),
            jax.ShapeDtypeStruct((T, 8), F32),
            jax.ShapeDtypeStruct((1, LANES), F32),
        ],
        scratch_shapes=[pltpu.VMEM((1, LANES), F32)],
        compiler_params=_params("arbitrary"),
        name="router",
    )(x2, g, rw, rb)


def _dispatch_kernel(dest_ref, hf_ref, xs_in_ref, xs_ref, sem):
    del xs_in_ref
    tm = hf_ref.shape[0]

    def row_copy(r, d):
        return pltpu.make_async_copy(hf_ref.at[pl.ds(r, 1)], xs_ref.at[pl.ds(d, 1)], sem)

    def issue(r, _):
        row_copy(r, dest_ref[2 * r]).start()
        row_copy(r, dest_ref[2 * r + 1]).start()
        return 0

    lax.fori_loop(0, tm, issue, 0, unroll=DMA_ISSUE_UNROLL)
    for _ in range(2):
        pltpu.make_async_copy(hf_ref, xs_ref.at[pl.ds(0, tm)], sem).wait()


def _dispatch(dest, hf, xs_zero):
    T, D = hf.shape
    tm = TOKEN_TILE
    return pl.pallas_call(
        _dispatch_kernel,
        grid=(T // tm,),
        in_specs=[
            pl.BlockSpec((2 * tm,), lambda i: (i,), memory_space=pltpu.SMEM),
            pl.BlockSpec((tm, D), lambda i: (i, 0)),
            pl.BlockSpec(memory_space=pl.ANY),
        ],
        out_specs=pl.BlockSpec(memory_space=pl.ANY),
        out_shape=jax.ShapeDtypeStruct(xs_zero.shape, xs_zero.dtype),
        scratch_shapes=[pltpu.SemaphoreType.DMA(())],
        input_output_aliases={2: 0},
        compiler_params=_params("arbitrary"),
        name="moe_dispatch",
    )(dest, hf, xs_zero)


def _combine_kernel(dest_ref, x_ref, gt_ref, y_ref, o_ref, buf_ref, sem):
    tm = x_ref.shape[0]

    def row_copy(slot, r, d):
        return pltpu.make_async_copy(y_ref.at[pl.ds(d, 1)], buf_ref.at[slot, pl.ds(r, 1)], sem)

    def issue(r, _):
        row_copy(0, r, dest_ref[2 * r]).start()
        row_copy(1, r, dest_ref[2 * r + 1]).start()
        return 0

    lax.fori_loop(0, tm, issue, 0, unroll=DMA_ISSUE_UNROLL)
    for slot in range(2):
        pltpu.make_async_copy(y_ref.at[pl.ds(0, tm)], buf_ref.at[slot], sem).wait()
    gt = gt_ref[...]
    o_ref[...] = x_ref[...] + gt[:, 0:1] * buf_ref[0] + gt[:, 1:2] * buf_ref[1]


def _combine(dest, x2, gates, y):
    T, D = x2.shape
    tm = TOKEN_TILE
    return pl.pallas_call(
        _combine_kernel,
        grid=(T // tm,),
        in_specs=[
            pl.BlockSpec((2 * tm,), lambda i: (i,), memory_space=pltpu.SMEM),
            pl.BlockSpec((tm, D), lambda i: (i, 0)),
            pl.BlockSpec((tm, 8), lambda i: (i, 0)),
            pl.BlockSpec(memory_space=pl.ANY),
        ],
        out_specs=pl.BlockSpec((tm, D), lambda i: (i, 0)),
        out_shape=jax.ShapeDtypeStruct((T, D), F32),
        scratch_shapes=[pltpu.VMEM((2, tm, D), F32), pltpu.SemaphoreType.DMA(())],
        compiler_params=_params("arbitrary"),
        name="moe_combine",
    )(dest, x2, gates, y)


def _moe(x2, g, rw, rb, wg, wu, wd, l, i_moe):
    T, D = x2.shape
    tmm = MOE_ROW_TILE
    P = 2 * T + N_EXPERTS * tmm
    hf, ei, gates, cnt = _router(x2, g, rw, rb, l, i_moe)
    counts = cnt[0, :N_EXPERTS].astype(jnp.int32)
    pcounts = (counts + tmm - 1) // tmm * tmm
    pends = jnp.cumsum(pcounts)
    pstarts = pends - pcounts
    dest = (jnp.take(pstarts, ei[:, 0:2]) + ei[:, 2:4]).reshape(-1)
    blk_e = jnp.minimum(jnp.searchsorted(pends, jnp.arange(P // tmm, dtype=jnp.int32) * tmm, side="right"),
                        N_EXPERTS - 1).astype(jnp.int32)
    xs = _dispatch(dest, hf, jnp.zeros((P, D), F32))
    y = _moe_ffn(blk_e, xs, wg, wu, wd, i_moe)
    return _combine(dest, x2, gates, y)


def _t5_buckets(dist):
    n = np.maximum(dist, 0)
    max_exact = N_BUCKETS // 2
    large = max_exact + (np.log(np.maximum(n, 1) / max_exact) / np.log(MAX_DISTANCE / max_exact)
                         * (N_BUCKETS - max_exact)).astype(np.int32)
    large = np.minimum(large, N_BUCKETS - 1)
    return np.where(n < max_exact, n, large).astype(np.int32)


def kernel(x, mem, norm_mix, w_in, sb_out_gain, swa_q_gain, swa_k_gain, swa_sinks, swa_out_gain, rel_bias, w_out, norm_xattn, norm_mem, xattn_wq, xattn_wkv, xattn_q_gain, xattn_k_gain, xattn_wo, norm_ffn, dense_w_gate, dense_w_up, dense_w_down, router_w, router_b, exp_w_gate, exp_w_up, exp_w_down):
    B, S, D = x.shape
    depth = w_in.shape[0]
    T = B * S
    row3 = lambda a: a.reshape(a.shape[0], 1, a.shape[1])
    bf = lambda a: a.astype(BF16)

    dist = WINDOW + np.arange(WINDOW)[:, None] - np.arange(2 * WINDOW)[None, :]
    swa_bias = jnp.transpose(rel_bias[_t5_buckets(dist)], (2, 0, 1))
    band = (dist >= 0) & (dist < WINDOW)
    band = np.stack([band & (np.arange(2 * WINDOW)[None, :] >= WINDOW), band])
    swa_bias = jnp.where(band[:, None], swa_bias[None], -jnp.inf)
    router_w_p = jnp.pad(router_w, ((0, 0), (0, 0), (0, LANES - N_EXPERTS)))
    router_b_p = row3(jnp.pad(router_b, ((0, 0), (0, LANES - N_EXPERTS))))

    w_in_b, w_out_b = bf(w_in), bf(w_out)
    wq_b, wkv_b, wo_b = bf(xattn_wq), bf(xattn_wkv), bf(xattn_wo)
    dg_b, du_b, dd_b = bf(dense_w_gate), bf(dense_w_up), bf(dense_w_down)
    eg_b, eu_b, ed_b = bf(exp_w_gate), bf(exp_w_up), bf(exp_w_down)
    norm_mix3, norm_x3, norm_f3 = row3(norm_mix), row3(norm_xattn), row3(norm_ffn)
    sb_g3, sw_g3 = row3(sb_out_gain), row3(swa_out_gain)
    xq_g3, xk_g3 = row3(xattn_q_gain), row3(xattn_k_gain)

    kmem, vmem = _memkv(mem, row3(norm_mem), wkv_b, xk_g3)

    x2 = x.reshape(T, D)
    for l in range(depth):
        proj = _inproj(x2, norm_mix3, w_in_b, l).reshape(B, S, -1)
        sb_o = _sb_attention(proj)
        sw_o = _swa_attention(proj, swa_sinks[l], jnp.tile(swa_q_gain[l], SWA_Q_HEADS)[None, :],
                              jnp.tile(swa_k_gain[l], SWA_KV_HEADS)[None, :], swa_bias)
        x2 = _outproj(sb_o.reshape(T, -1), sw_o.reshape(T, -1), sb_g3, sw_g3, w_out_b, x2, l)
        x2 = _xattn(x2.reshape(B, S, D), norm_x3, wq_b, xq_g3, kmem, vmem, wo_b, l).reshape(T, D)
        if l % 2 == 0:
            x2 = _ffn_dense(x2, norm_f3, dg_b, du_b, dd_b, l, l // 2)
        else:
            x2 = _moe(x2, norm_f3, router_w_p, router_b_p, eg_b, eu_b, ed_b, l, l // 2)
    return x2.reshape(B, S, D)
```

```python
import functools

import numpy as np
import jax
import jax.numpy as jnp
from jax import lax
from jax.experimental import pallas as pl
from jax.experimental.pallas import tpu as pltpu

F32 = jnp.float32
BF16 = jnp.bfloat16

HEAD_DIM = 64
SB_WIDTH = 512
SWA_WIDTH = 512
SWA_Q_HEADS = 8
SWA_GROUP = 4
SWA_KV_HEADS = 2
WINDOW = 128
N_BUCKETS = 32
MAX_DISTANCE = 128
MEM_HEADS = 4
MEM_HEAD_DIM = 128
MEM_WIDTH = 512
N_EXPERTS = 8
EPS = 1e-6
LANES = 128

VMEM_LIMIT = 56 * 1024 * 1024

TOKEN_TILE = 512
FFN_TOKEN_TILE = 1024
FFN_COL_TILE = 512
SB_TILE = 256
MOE_ROW_TILE = 1024
DMA_ISSUE_UNROLL = 8
SB_LOG_UNDERFLOW = -104.0


def _params(*sem):
    return pltpu.CompilerParams(dimension_semantics=("arbitrary",) * len(sem), vmem_limit_bytes=VMEM_LIMIT)


def _rms(x, g):
    return x * lax.rsqrt(jnp.mean(x * x, axis=-1, keepdims=True) + EPS) * g


def _dot(a, b):
    return jnp.dot(a, b, preferred_element_type=F32)


def _dot_nt(a, b):
    return lax.dot_general(a, b, (((1,), (1,)), ((), ())), preferred_element_type=F32)


def _inproj_kernel(x_ref, g_ref, w_ref, o_ref, *, col_tile):
    h = _rms(x_ref[...], g_ref[...]).astype(BF16)
    for c in range(w_ref.shape[1] // col_tile):
        sl = slice(c * col_tile, (c + 1) * col_tile)
        o_ref[:, sl] = _dot(h, w_ref[:, sl]).astype(BF16)


def _inproj(x2, g, w, l):
    T, D = x2.shape
    N = w.shape[-1]
    tm = TOKEN_TILE
    return pl.pallas_call(
        functools.partial(_inproj_kernel, col_tile=N // 3),
        grid=(T // tm,),
        in_specs=[
            pl.BlockSpec((tm, D), lambda i: (i, 0)),
            pl.BlockSpec((None, 1, D), lambda i: (l, 0, 0)),
            pl.BlockSpec((None, D, N), lambda i: (l, 0, 0)),
        ],
        out_specs=pl.BlockSpec((tm, N), lambda i: (i, 0)),
        out_shape=jax.ShapeDtypeStruct((T, N), BF16),
        compiler_params=_params("parallel"),
        name="inproj",
    )(x2, g, w)


def _sb_kernel(q_ref, k_ref, v_ref, suf_ref, o_ref, kt_ref, z_ref, acc_ref, c_ref, *, tile):
    S = q_ref.shape[0]
    lane = lax.broadcasted_iota(jnp.int32, (1, LANES), 1)
    lo_half = lane < HEAD_DIM
    row = lax.broadcasted_iota(jnp.int32, (tile, tile), 0)
    col = lax.broadcasted_iota(jnp.int32, (tile, tile), 1)
    strict = col < row
    suffix = suf_ref[...]

    for j in range(S // tile):
        kt_ref[j] = k_ref[j * tile:(j + 1) * tile, :].T

    heads = range(2)
    sl = [slice(h * tile, (h + 1) * tile) for h in heads]

    def log_terms(z, masked):
        n = range(len(z))
        log_beta = [jnp.minimum(z[u], 0.0) - jnp.log(1.0 + jnp.exp(-jnp.abs(z[u]))) for u in n]
        log_om = [log_beta[u] - z[u] for u in n]
        log_om = [jnp.where(strict, log_om[u], 0.0) if masked[u] else log_om[u] for u in n]
        hi = [log_om[u].astype(BF16) for u in n]
        lo = [(log_om[u] - hi[u].astype(F32)).astype(BF16) for u in n]
        tail = [_dot(jnp.concatenate([hi[u], lo[u]], axis=0), suffix) for u in n]
        tail = [tail[u][:tile] + tail[u][tile:] for u in n]
        row_sum = [jnp.sum(log_om[u], axis=-1, keepdims=True) for u in n]
        return log_beta, tail, row_sum

    def q_block(i, _):
        q0 = pl.multiple_of(i * tile, tile)
        q2 = q_ref[pl.ds(q0, tile), :] * jnp.asarray(HEAD_DIM ** -0.5, BF16)
        zero = jnp.zeros_like(q2)
        qs = [jnp.where(lo_half, q2, zero), jnp.where(lo_half, zero, q2)]

        def key_rows(j):
            return pl.ds(pl.multiple_of(j * tile, tile), tile)

        def prefetch_scores(j, slot):
            kt = kt_ref[jnp.maximum(j, 0)]
            for h in heads:
                z_ref[slot, sl[h]] = _dot(qs[h], kt)

        def first_block():
            kt = kt_ref[0]
            z = [_dot(qs[h], kt) for h in heads]
            log_beta, tail, row_sum = log_terms(z, [True, True])
            a = [jnp.where(strict, jnp.exp(log_beta[h] + tail[h]), 0.0) for h in heads]
            v_d = v_ref[0:tile, :]
            for h in heads:
                acc_ref[sl[h]] = _dot(a[h].astype(BF16), v_d)
                c_ref[sl[h]] = row_sum[h]
            return jnp.float32(0.0)

        def later_block():
            kt_d, kt_o = kt_ref[i], kt_ref[i - 1]
            z = [_dot(qs[h], kt_d) for h in heads] + [_dot(qs[h], kt_o) for h in heads]
            prefetch_scores(i - 2, 0)
            log_beta, tail, row_sum = log_terms(z, [True, True, False, False])
            a_d = [jnp.where(strict, jnp.exp(log_beta[h] + tail[h]), 0.0) for h in heads]
            a_o = [jnp.exp(log_beta[2 + h] + tail[2 + h] + row_sum[h]) for h in heads]
            v_d, v_o = v_ref[key_rows(i), :], v_ref[key_rows(i - 1), :]
            c = [row_sum[h] + row_sum[2 + h] for h in heads]
            for h in heads:
                acc_ref[sl[h]] = _dot(a_d[h].astype(BF16), v_d) + _dot(a_o[h].astype(BF16), v_o)
                c_ref[sl[h]] = c[h]
            return jnp.maximum(jnp.max(c[0]), jnp.max(c[1]))

        c_max = lax.cond(i == 0, first_block, later_block)

        def key_tile(state):
            j, slot, _ = state
            z = [z_ref[slot, sl[h]] for h in heads]
            prefetch_scores(j - 1, 1 - slot)
            log_beta, tail, row_sum = log_terms(z, [False, False])
            c = [c_ref[sl[h]] for h in heads]
            a = [jnp.exp(log_beta[h] + tail[h] + c[h]) for h in heads]
            c = [c[h] + row_sum[h] for h in heads]
            v_j = v_ref[key_rows(j), :]
            for h in heads:
                acc_ref[sl[h]] += _dot(a[h].astype(BF16), v_j)
                c_ref[sl[h]] = c[h]
            return j - 1, 1 - slot, jnp.maximum(jnp.max(c[0]), jnp.max(c[1]))

        lax.while_loop(lambda s: (s[0] >= 0) & (s[2] > SB_LOG_UNDERFLOW), key_tile,
                       (i - 2, jnp.int32(0), c_max))
        o_ref[pl.ds(q0, tile), :] = jnp.where(lo_half, acc_ref[:tile], acc_ref[tile:])
        return 0

    lax.fori_loop(0, S // tile, q_block, 0)


def _sb_attention(proj3):
    B, S, _ = proj3.shape
    pairs = SB_WIDTH // LANES
    tile = min(SB_TILE, S)
    suf = jnp.asarray(np.arange(tile)[:, None] > np.arange(tile)[None, :], BF16)
    spec = lambda off: pl.BlockSpec((None, S, LANES), lambda b, p: (b, 0, off + p))
    return pl.pallas_call(
        functools.partial(_sb_kernel, tile=tile),
        grid=(B, pairs),
        in_specs=[spec(0), spec(pairs), spec(2 * pairs),
                  pl.BlockSpec((tile, tile), lambda b, p: (0, 0))],
        out_specs=pl.BlockSpec((None, S, LANES), lambda b, p: (b, 0, p)),
        out_shape=jax.ShapeDtypeStruct((B, S, SB_WIDTH), F32),
        scratch_shapes=[pltpu.VMEM((S // tile, LANES, tile), BF16),
                        pltpu.VMEM((2, 2 * tile, tile), F32),
                        pltpu.VMEM((2 * tile, LANES), F32), pltpu.VMEM((2 * tile, 1), F32)],
        compiler_params=_params("parallel", "parallel"),
        name="sb_attention",
    )(proj3, proj3, proj3, suf)


def _group_mean_sq(x, ones_ref):
    sq = x * x
    hi = sq.astype(BF16)
    lo = (sq - hi.astype(F32)).astype(BF16)
    return _dot(hi, ones_ref[...]) + _dot(lo, ones_ref[...])


def _swa_kernel(sink_ref, q_ref, kp_ref, kc_ref, vp_ref, vc_ref, qg_ref, kg_ref, gq_ref, gk_ref, sel_ref,
                bias_ref, o_ref):
    q = q_ref[...].astype(F32)
    k = jnp.concatenate([kp_ref[...], kc_ref[...]], axis=0).astype(F32)
    v = jnp.concatenate([vp_ref[...], vc_ref[...]], axis=0)
    half = SWA_WIDTH // 2
    q_ms = jnp.concatenate([_group_mean_sq(q[:, :half], gq_ref), _group_mean_sq(q[:, half:], gq_ref)], axis=1)
    qn = (q * lax.rsqrt(q_ms + EPS) * qg_ref[...] * (HEAD_DIM ** -0.5)).astype(BF16)
    kn_t = (k * lax.rsqrt(_group_mean_sq(k, gk_ref) + EPS) * kg_ref[...]).astype(BF16).T
    lane = lax.broadcasted_iota(jnp.int32, (1, LANES), 1)
    lo_half = lane < HEAD_DIM
    zero = jnp.zeros((WINDOW, LANES), BF16)
    W = WINDOW
    head_row = lax.broadcasted_iota(jnp.int32, (SWA_GROUP * W, 1), 0) // W
    groups = range(SWA_KV_HEADS)
    kt_g = [kn_t[g * HEAD_DIM:(g + 1) * HEAD_DIM, :] for g in groups]
    kt_dup = [jnp.concatenate([kt_g[g], kt_g[g]], axis=0) for g in groups]
    v_dup = [_dot(v, sel_ref[g]).astype(BF16) for g in groups]
    lhs = []
    for g in groups:
        rows = []
        for t in range(g * SWA_GROUP // 2, (g + 1) * SWA_GROUP // 2):
            qt = qn[:, t * LANES:(t + 1) * LANES]
            rows += [jnp.where(lo_half, qt, zero), jnp.where(lo_half, zero, qt)]
        lhs.append(jnp.concatenate(rows, axis=0))
    s = [_dot(lhs[g], kt_dup[g]) for g in groups]
    s = [s[g] + bias_ref[g * SWA_GROUP:(g + 1) * SWA_GROUP].reshape(SWA_GROUP * W, 2 * W) for g in groups]
    sink = []
    for g in groups:
        sk = jnp.zeros((SWA_GROUP * W, 1), F32)
        for hh in range(SWA_GROUP):
            sk = jnp.where(head_row == hh, sink_ref[g * SWA_GROUP + hh], sk)
        sink.append(sk)
    m = [jnp.maximum(jnp.max(s[g], axis=-1, keepdims=True), sink[g]) for g in groups]
    p = [jnp.exp(s[g] - m[g]) for g in groups]
    den = [jnp.sum(p[g], axis=-1, keepdims=True) + jnp.exp(sink[g] - m[g]) for g in groups]
    r = [_dot(p[g].astype(BF16), v_dup[g]) / den[g] for g in groups]
    for g in groups:
        for tt in range(SWA_GROUP // 2):
            t = g * SWA_GROUP // 2 + tt
            o_ref[:, t * LANES:(t + 1) * LANES] = jnp.where(
                lo_half, r[g][2 * tt * W:(2 * tt + 1) * W], r[g][(2 * tt + 1) * W:(2 * tt + 2) * W])


def _swa_attention(proj3, sinks, q_gain, k_gain, bias):
    B, S, _ = proj3.shape
    W = WINDOW
    q_blk = (3 * SB_WIDTH) // SWA_WIDTH
    k_blk = (3 * SB_WIDTH + SWA_WIDTH) // LANES
    v_blk = k_blk + 1
    group = lambda n: (np.arange(n)[:, None] // HEAD_DIM == np.arange(n)[None, :] // HEAD_DIM) / HEAD_DIM
    ones_q = jnp.asarray(group(SWA_WIDTH // 2), BF16)
    ones_k = jnp.asarray(group(LANES), BF16)
    sel = jnp.asarray(np.stack([np.arange(LANES)[:, None] == g * HEAD_DIM + np.arange(LANES)[None, :] % HEAD_DIM
                                for g in range(SWA_KV_HEADS)]), BF16)
    prev = lambda c: pl.BlockSpec((None, W, LANES), lambda b, n: (b, jnp.maximum(n - 1, 0), c))
    cur = lambda c: pl.BlockSpec((None, W, LANES), lambda b, n: (b, n, c))
    const = lambda *shape: pl.BlockSpec(shape, lambda b, n: (0,) * len(shape))
    return pl.pallas_call(
        _swa_kernel,
        grid=(B, S // W),
        in_specs=[
            pl.BlockSpec(memory_space=pltpu.SMEM),
            pl.BlockSpec((None, W, SWA_WIDTH), lambda b, n: (b, n, q_blk)),
            prev(k_blk), cur(k_blk), prev(v_blk), cur(v_blk),
            const(1, SWA_WIDTH), const(1, LANES),
            const(SWA_WIDTH // 2, SWA_WIDTH // 2), const(LANES, LANES), const(SWA_KV_HEADS, LANES, LANES),
            pl.BlockSpec((None, SWA_Q_HEADS, W, 2 * W), lambda b, n: (jnp.minimum(n, 1), 0, 0, 0)),
        ],
        out_specs=pl.BlockSpec((None, W, SWA_WIDTH), lambda b, n: (b, n, 0)),
        out_shape=jax.ShapeDtypeStruct((B, S, SWA_WIDTH), F32),
        compiler_params=_params("arbitrary", "arbitrary"),
        name="swa_attention",
    )(sinks, proj3, proj3, proj3, proj3, proj3, q_gain, k_gain, ones_q, ones_k, sel, bias)


def _outproj_kernel(sb_ref, sw_ref, gsb_ref, gsw_ref, w_ref, x_ref, o_ref):
    a = _rms(sb_ref[...], gsb_ref[...]).astype(BF16)
    b = _rms(sw_ref[...], gsw_ref[...]).astype(BF16)
    o_ref[...] = x_ref[...] + _dot(a, w_ref[:SB_WIDTH, :]) + _dot(b, w_ref[SB_WIDTH:, :])


def _outproj(sb_o, sw_o, g_sb, g_sw, w, x2, l):
    T, D = x2.shape
    tm = TOKEN_TILE
    return pl.pallas_call(
        _outproj_kernel,
        grid=(T // tm,),
        in_specs=[
            pl.BlockSpec((tm, SB_WIDTH), lambda i: (i, 0)),
            pl.BlockSpec((tm, SWA_WIDTH), lambda i: (i, 0)),
            pl.BlockSpec((None, 1, SB_WIDTH), lambda i: (l, 0, 0)),
            pl.BlockSpec((None, 1, SWA_WIDTH), lambda i: (l, 0, 0)),
            pl.BlockSpec((None, SB_WIDTH + SWA_WIDTH, D), lambda i: (l, 0, 0)),
            pl.BlockSpec((tm, D), lambda i: (i, 0)),
        ],
        out_specs=pl.BlockSpec((tm, D), lambda i: (i, 0)),
        out_shape=jax.ShapeDtypeStruct((T, D), F32),
        compiler_params=_params("parallel"),
        name="outproj",
    )(sb_o, sw_o, g_sb, g_sw, w, x2)


def _memkv_kernel(m_ref, g_ref, w_ref, kg_ref, k_ref, v_ref):
    h = _rms(m_ref[...], g_ref[...]).astype(BF16)
    kv = _dot(h, w_ref[...])
    for hd in range(MEM_HEADS):
        sl = slice(hd * MEM_HEAD_DIM, (hd + 1) * MEM_HEAD_DIM)
        k_ref[:, sl] = _rms(kv[:, sl], kg_ref[...]).astype(BF16)
    v_ref[...] = kv[:, MEM_WIDTH:].astype(BF16)


def _memkv(mem, norm_mem, wkv, k_gain):
    B, N, D = mem.shape
    L = wkv.shape[0]
    out = jax.ShapeDtypeStruct((L, B, N, MEM_WIDTH), BF16)
    return pl.pallas_call(
        _memkv_kernel,
        grid=(L, B),
        in_specs=[
            pl.BlockSpec((None, N, D), lambda l, b: (b, 0, 0)),
            pl.BlockSpec((None, 1, D), lambda l, b: (l, 0, 0)),
            pl.BlockSpec((None, D, 2 * MEM_WIDTH), lambda l, b: (l, 0, 0)),
            pl.BlockSpec((None, 1, MEM_HEAD_DIM), lambda l, b: (l, 0, 0)),
        ],
        out_specs=[pl.BlockSpec((None, None, N, MEM_WIDTH), lambda l, b: (l, b, 0, 0))] * 2,
        out_shape=[out, out],
        compiler_params=_params("parallel", "parallel"),
        name="memkv",
    )(mem, norm_mem, wkv, k_gain)


def _xattn_kernel(x_ref, gx_ref, wq_ref, qg_ref, k_ref, v_ref, wo_ref, o_ref):
    x = x_ref[...]
    h = _rms(x, gx_ref[...]).astype(BF16)
    q = _dot(h, wq_ref[...])
    outs = []
    for hd in range(MEM_HEADS):
        sl = slice(hd * MEM_HEAD_DIM, (hd + 1) * MEM_HEAD_DIM)
        qn = _rms(q[:, sl], qg_ref[...]).astype(BF16)
        s = _dot_nt(qn, k_ref[:, sl]) * (MEM_HEAD_DIM ** -0.5)
        p = jnp.exp(s - jnp.max(s, axis=-1, keepdims=True))
        den = jnp.sum(p, axis=-1, keepdims=True)
        outs.append((_dot(p.astype(BF16), v_ref[:, sl]) / den).astype(BF16))
    o_ref[...] = x + _dot(jnp.concatenate(outs, axis=1), wo_ref[...])


def _xattn(x3, norm_x, wq, q_gain, kmem, vmem, wo, l):
    B, S, D = x3.shape
    N = kmem.shape[2]
    tm = min(TOKEN_TILE, S)
    return pl.pallas_call(
        _xattn_kernel,
        grid=(B, S // tm),
        in_specs=[
            pl.BlockSpec((None, tm, D), lambda b, i: (b, i, 0)),
            pl.BlockSpec((None, 1, D), lambda b, i: (l, 0, 0)),
            pl.BlockSpec((None, D, MEM_WIDTH), lambda b, i: (l, 0, 0)),
            pl.BlockSpec((None, 1, MEM_HEAD_DIM), lambda b, i: (l, 0, 0)),
            pl.BlockSpec((None, None, N, MEM_WIDTH), lambda b, i: (l, b, 0, 0)),
            pl.BlockSpec((None, None, N, MEM_WIDTH), lambda b, i: (l, b, 0, 0)),
            pl.BlockSpec((None, MEM_WIDTH, D), lambda b, i: (l, 0, 0)),
        ],
        out_specs=pl.BlockSpec((None, tm, D), lambda b, i: (b, i, 0)),
        out_shape=jax.ShapeDtypeStruct((B, S, D), F32),
        compiler_params=_params("parallel", "parallel"),
        name="xattn",
    )(x3, norm_x, wq, q_gain, kmem, vmem, wo)


def _swiglu_step(h, wg_ref, wu_ref, wd_ref, between=None):
    a = _dot(h, wg_ref[...])
    u = _dot(h, wu_ref[...])
    if between is not None:
        between()
    act = a / (1.0 + jnp.exp(-a)) * u
    return _dot(act.astype(BF16), wd_ref[...])


def _ffn_kernel(x_ref, g_ref, wg_ref, wu_ref, wd_ref, o_ref, h_ref):
    @pl.when(pl.program_id(1) == 0)
    def _():
        x = x_ref[...]
        h_ref[...] = _rms(x, g_ref[...]).astype(BF16)
        o_ref[...] = x

    o_ref[...] += _swiglu_step(h_ref[...], wg_ref, wu_ref, wd_ref)


def _ffn_dense(x2, g, wg, wu, wd, l, i_dense):
    T, D = x2.shape
    F = wg.shape[-1]
    tm, tf = min(FFN_TOKEN_TILE, T), FFN_COL_TILE
    return pl.pallas_call(
        _ffn_kernel,
        grid=(T // tm, F // tf),
        in_specs=[
            pl.BlockSpec((tm, D), lambda i, f: (i, 0)),
            pl.BlockSpec((None, 1, D), lambda i, f: (l, 0, 0)),
            pl.BlockSpec((None, D, tf), lambda i, f: (i_dense, 0, f)),
            pl.BlockSpec((None, D, tf), lambda i, f: (i_dense, 0, f)),
            pl.BlockSpec((None, tf, D), lambda i, f: (i_dense, f, 0)),
        ],
        out_specs=pl.BlockSpec((tm, D), lambda i, f: (i, 0)),
        out_shape=jax.ShapeDtypeStruct((T, D), F32),
        scratch_shapes=[pltpu.VMEM((tm, D), BF16)],
        compiler_params=_params("parallel", "arbitrary"),
        name="ffn_dense",
    )(x2, g, wg, wu, wd)


def _moe_ffn_kernel(be_ref, src0_ref, src_next_ref, hf_ref, wg_ref, wu_ref, wd_ref, o_ref, h_ref, x_ref, sem,
                    *, rows_per_step, n_col_steps):
    del be_ref
    i, f = pl.program_id(0), pl.program_id(1)
    tm = h_ref.shape[0]
    copies_per_block = n_col_steps * rows_per_step
    slot = i % 2

    def issue_rows(src_ref, step, into):
        base = pl.multiple_of(step * rows_per_step, 8)
        for u in range(rows_per_step):
            pltpu.make_async_copy(hf_ref.at[pl.ds(src_ref[base + u], 1)],
                                  x_ref.at[into, pl.ds(base + u, 1)], sem.at[into]).start()

    def wait_block(into):
        rows = pl.ds(0, copies_per_block)
        pltpu.make_async_copy(hf_ref.at[rows], x_ref.at[into, rows], sem.at[into]).wait()

    @pl.when((i == 0) & (f == 0))
    def _():
        def issue(step, _):
            issue_rows(src0_ref, step, 0)
            return 0

        lax.fori_loop(0, n_col_steps, issue, 0)

    @pl.when(f == 0)
    def _():
        wait_block(slot)
        h_ref[...] = x_ref[slot, 0:tm].astype(BF16)
        o_ref[...] = jnp.zeros_like(o_ref)

    o_ref[...] += _swiglu_step(h_ref[...], wg_ref, wu_ref, wd_ref,
                               between=functools.partial(issue_rows, src_next_ref, f, 1 - slot))

    @pl.when((i == pl.num_programs(0) - 1) & (f == n_col_steps - 1))
    def _():
        wait_block(1 - slot)


def _moe_ffn(blk_e, row_src, hf, wg, wu, wd, i_moe):
    P = row_src.shape[0]
    D = hf.shape[1]
    F = wg.shape[-1]
    tm, tf = MOE_ROW_TILE, FFN_COL_TILE
    nb, nf = P // tm, F // tf
    rows_per_step = -(-tm // (8 * nf)) * 8
    copies = rows_per_step * nf
    idx_block = -(-copies // 1024) * 1024
    pad_cols = np.minimum(np.arange(idx_block), tm - 1)
    row_src = row_src.reshape(nb, tm)[:, pad_cols].reshape(-1)
    return pl.pallas_call(
        functools.partial(_moe_ffn_kernel, rows_per_step=rows_per_step, n_col_steps=nf),
        grid_spec=pltpu.PrefetchScalarGridSpec(
            num_scalar_prefetch=1,
            grid=(nb, nf),
            in_specs=[
                pl.BlockSpec((idx_block,), lambda i, f, be: (0,), memory_space=pltpu.SMEM),
                pl.BlockSpec((idx_block,), lambda i, f, be: (jnp.minimum(i + 1, nb - 1),),
                             memory_space=pltpu.SMEM),
                pl.BlockSpec(memory_space=pl.ANY),
                pl.BlockSpec((None, None, D, tf), lambda i, f, be: (i_moe, be[i], 0, f)),
                pl.BlockSpec((None, None, D, tf), lambda i, f, be: (i_moe, be[i], 0, f)),
                pl.BlockSpec((None, None, tf, D), lambda i, f, be: (i_moe, be[i], f, 0)),
            ],
            out_specs=pl.BlockSpec((tm, D), lambda i, f, be: (i, 0)),
            scratch_shapes=[pltpu.VMEM((tm, D), BF16), pltpu.VMEM((2, copies, D), F32),
                            pltpu.SemaphoreType.DMA((2,))],
        ),
        out_shape=jax.ShapeDtypeStruct((P, D), F32),
        compiler_params=_params("arbitrary", "arbitrary"),
        name="moe_ffn",
    )(blk_e, row_src, row_src, hf, wg, wu, wd)


def _router_kernel(x_ref, g_ref, rw_ref, rb_ref, hf_ref, ei_ref, gt_ref, cnt_ref, carry_ref):
    tm = x_ref.shape[0]

    @pl.when(pl.program_id(0) == 0)
    def _():
        carry_ref[...] = jnp.zeros_like(carry_ref)

    hf = _rms(x_ref[...], g_ref[...])
    hf_ref[...] = hf
    lane = lax.broadcasted_iota(jnp.int32, (tm, LANES), 1)
    logits = jnp.dot(hf, rw_ref[...], precision=lax.Precision.HIGHEST, preferred_element_type=F32) + rb_ref[...]
    logits = jnp.where(lane < N_EXPERTS, logits, -jnp.inf)
    m1 = jnp.max(logits, axis=-1, keepdims=True)
    i1 = jnp.min(jnp.where(logits == m1, lane, LANES), axis=-1, keepdims=True)
    rest = jnp.where(lane == i1, -jnp.inf, logits)
    m2 = jnp.max(rest, axis=-1, keepdims=True)
    i2 = jnp.min(jnp.where(rest == m2, lane, LANES), axis=-1, keepdims=True)
    e = jnp.exp(m2 - m1)
    g1 = 1.0 / (1.0 + e)
    g2 = e / (1.0 + e)
    oh1 = lane == i1
    oh2 = lane == i2
    oh = (oh1 | oh2).astype(F32)
    row = lax.broadcasted_iota(jnp.int32, (tm, tm), 0)
    col = lax.broadcasted_iota(jnp.int32, (tm, tm), 1)
    earlier = (row > col).astype(BF16)
    before = _dot(earlier, oh.astype(BF16)) + carry_ref[...]
    r1 = jnp.sum(jnp.where(oh1, before, 0.0), axis=-1, keepdims=True).astype(jnp.int32)
    r2 = jnp.sum(jnp.where(oh2, before, 0.0), axis=-1, keepdims=True).astype(jnp.int32)
    carry_ref[...] += jnp.sum(oh, axis=0, keepdims=True)
    cnt_ref[...] = carry_ref[...]
    ei = jnp.where(lane == 0, i1, jnp.where(lane == 1, i2, jnp.where(lane == 2, r1, jnp.where(lane == 3, r2, 0))))
    ei_ref[...] = ei[:, :8]
    gt = jnp.where(lane == 0, g1, jnp.where(lane == 1, g2, 0.0))
    gt_ref[...] = gt[:, :8]


def _router(x2, g, rw, rb, l, i_moe):
    T, D = x2.shape
    tm = TOKEN_TILE
    return pl.pallas_call(
        _router_kernel,
        grid=(T // tm,),
        in_specs=[
            pl.BlockSpec((tm, D), lambda i: (i, 0)),
            pl.BlockSpec((None, 1, D), lambda i: (l, 0, 0)),
            pl.BlockSpec((None, D, LANES), lambda i: (i_moe, 0, 0)),
            pl.BlockSpec((None, 1, LANES), lambda i: (i_moe, 0, 0)),
        ],
        out_specs=[
            pl.BlockSpec((tm, D), lambda i: (i, 0)),
            pl.BlockSpec((tm, 8), lambda i: (i, 0)),
            pl.BlockSpec((tm, 8), lambda i: (i, 0)),
            pl.BlockSpec((1, LANES), lambda i: (0, 0)),
        ],
        out_shape=[
            jax.ShapeDtypeStruct((T, D), F32),
            jax.ShapeDtypeStruct((T, 8), jnp.int32),
            jax.ShapeDtypeStruct((T, 8), F32),
            jax.ShapeDtypeStruct((1, LANES), F32),
        ],
        scratch_shapes=[pltpu.VMEM((1, LANES), F32)],
        compiler_params=_params("arbitrary"),
        name="router",
    )(x2, g, rw, rb)


def _combine_kernel(dest_ref, x_ref, gt_ref, y_ref, o_ref, buf_ref, sem):
    tm = x_ref.shape[0]

    def row_copy(slot, r, d):
        return pltpu.make_async_copy(y_ref.at[pl.ds(d, 1)], buf_ref.at[slot, pl.ds(r, 1)], sem)

    def issue(r, _):
        row_copy(0, r, dest_ref[2 * r]).start()
        row_copy(1, r, dest_ref[2 * r + 1]).start()
        return 0

    lax.fori_loop(0, tm, issue, 0, unroll=DMA_ISSUE_UNROLL)
    for slot in range(2):
        pltpu.make_async_copy(y_ref.at[pl.ds(0, tm)], buf_ref.at[slot], sem).wait()
    gt = gt_ref[...]
    o_ref[...] = x_ref[...] + gt[:, 0:1] * buf_ref[0] + gt[:, 1:2] * buf_ref[1]


def _combine(dest, x2, gates, y):
    T, D = x2.shape
    tm = TOKEN_TILE
    return pl.pallas_call(
        _combine_kernel,
        grid=(T // tm,),
        in_specs=[
            pl.BlockSpec((2 * tm,), lambda i: (i,), memory_space=pltpu.SMEM),
            pl.BlockSpec((tm, D), lambda i: (i, 0)),
            pl.BlockSpec((tm, 8), lambda i: (i, 0)),
            pl.BlockSpec(memory_space=pl.ANY),
        ],
        out_specs=pl.BlockSpec((tm, D), lambda i: (i, 0)),
        out_shape=jax.ShapeDtypeStruct((T, D), F32),
        scratch_shapes=[pltpu.VMEM((2, tm, D), F32), pltpu.SemaphoreType.DMA(())],
        compiler_params=_params("arbitrary"),
        name="moe_combine",
    )(dest, x2, gates, y)


def _moe(x2, g, rw, rb, wg, wu, wd, l, i_moe):
    T, D = x2.shape
    tmm = MOE_ROW_TILE
    P = 2 * T + N_EXPERTS * tmm
    hf, ei, gates, cnt = _router(x2, g, rw, rb, l, i_moe)
    counts = cnt[0, :N_EXPERTS].astype(jnp.int32)
    pcounts = (counts + tmm - 1) // tmm * tmm
    pends = jnp.cumsum(pcounts)
    pstarts = pends - pcounts
    dest = (jnp.take(pstarts, ei[:, 0:2]) + ei[:, 2:4]).reshape(-1)
    blk_e = jnp.minimum(jnp.searchsorted(pends, jnp.arange(P // tmm, dtype=jnp.int32) * tmm, side="right"),
                        N_EXPERTS - 1).astype(jnp.int32)
    row_src = jnp.zeros((P,), jnp.int32).at[dest].set(jnp.arange(2 * T, dtype=jnp.int32) // 2)
    y = _moe_ffn(blk_e, row_src, hf, wg, wu, wd, i_moe)
    return _combine(dest, x2, gates, y)


def _t5_buckets(dist):
    n = np.maximum(dist, 0)
    max_exact = N_BUCKETS // 2
    large = max_exact + (np.log(np.maximum(n, 1) / max_exact) / np.log(MAX_DISTANCE / max_exact)
                         * (N_BUCKETS - max_exact)).astype(np.int32)
    large = np.minimum(large, N_BUCKETS - 1)
    return np.where(n < max_exact, n, large).astype(np.int32)


def kernel(x, mem, norm_mix, w_in, sb_out_gain, swa_q_gain, swa_k_gain, swa_sinks, swa_out_gain, rel_bias, w_out, norm_xattn, norm_mem, xattn_wq, xattn_wkv, xattn_q_gain, xattn_k_gain, xattn_wo, norm_ffn, dense_w_gate, dense_w_up, dense_w_down, router_w, router_b, exp_w_gate, exp_w_up, exp_w_down):
    B, S, D = x.shape
    depth = w_in.shape[0]
    T = B * S
    row3 = lambda a: a.reshape(a.shape[0], 1, a.shape[1])
    bf = lambda a: a.astype(BF16)

    dist = WINDOW + np.arange(WINDOW)[:, None] - np.arange(2 * WINDOW)[None, :]
    swa_bias = jnp.transpose(rel_bias[_t5_buckets(dist)], (2, 0, 1))
    band = (dist >= 0) & (dist < WINDOW)
    band = np.stack([band & (np.arange(2 * WINDOW)[None, :] >= WINDOW), band])
    swa_bias = jnp.where(band[:, None], swa_bias[None], -jnp.inf)
    router_w_p = jnp.pad(router_w, ((0, 0), (0, 0), (0, LANES - N_EXPERTS)))
    router_b_p = row3(jnp.pad(router_b, ((0, 0), (0, LANES - N_EXPERTS))))

    w_in_b, w_out_b = bf(w_in), bf(w_out)
    wq_b, wkv_b, wo_b = bf(xattn_wq), bf(xattn_wkv), bf(xattn_wo)
    dg_b, du_b, dd_b = bf(dense_w_gate), bf(dense_w_up), bf(dense_w_down)
    eg_b, eu_b, ed_b = bf(exp_w_gate), bf(exp_w_up), bf(exp_w_down)
    norm_mix3, norm_x3, norm_f3 = row3(norm_mix), row3(norm_xattn), row3(norm_ffn)
    sb_g3, sw_g3 = row3(sb_out_gain), row3(swa_out_gain)
    xq_g3, xk_g3 = row3(xattn_q_gain), row3(xattn_k_gain)

    kmem, vmem = _memkv(mem, row3(norm_mem), wkv_b, xk_g3)

    x2 = x.reshape(T, D)
    for l in range(depth):
        proj = _inproj(x2, norm_mix3, w_in_b, l).reshape(B, S, -1)
        sb_o = _sb_attention(proj)
        sw_o = _swa_attention(proj, swa_sinks[l], jnp.tile(swa_q_gain[l], SWA_Q_HEADS)[None, :],
                              jnp.tile(swa_k_gain[l], SWA_KV_HEADS)[None, :], swa_bias)
        x2 = _outproj(sb_o.reshape(T, -1), sw_o.reshape(T, -1), sb_g3, sw_g3, w_out_b, x2, l)
        x2 = _xattn(x2.reshape(B, S, D), norm_x3, wq_b, xq_g3, kmem, vmem, wo_b, l).reshape(T, D)
        if l % 2 == 0:
            x2 = _ffn_dense(x2, norm_f3, dg_b, du_b, dd_b, l, l // 2)
        else:
            x2 = _moe(x2, norm_f3, router_w_p, router_b_p, eg_b, eu_b, ed_b, l, l // 2)
    return x2.reshape(B, S, D)
```

```python
import functools

import numpy as np
import jax
import jax.numpy as jnp
from jax import lax
from jax.experimental import pallas as pl
from jax.experimental.pallas import tpu as pltpu

F32 = jnp.float32
BF16 = jnp.bfloat16

HEAD_DIM = 64
SB_WIDTH = 512
SWA_WIDTH = 512
SWA_Q_HEADS = 8
SWA_GROUP = 4
SWA_KV_HEADS = 2
WINDOW = 128
N_BUCKETS = 32
MAX_DISTANCE = 128
MEM_HEADS = 4
MEM_HEAD_DIM = 128
MEM_WIDTH = 512
N_EXPERTS = 8
EPS = 1e-6
LANES = 128

VMEM_LIMIT = 56 * 1024 * 1024

TOKEN_TILE = 512
FFN_TOKEN_TILE = 1024
FFN_COL_TILE = 512
SB_TILE = 256
MOE_ROW_TILE = 1024
DMA_ISSUE_UNROLL = 8
SB_LOG_UNDERFLOW = -104.0


def _params(*sem):
    return pltpu.CompilerParams(dimension_semantics=("arbitrary",) * len(sem), vmem_limit_bytes=VMEM_LIMIT)


def _rms(x, g):
    return x * lax.rsqrt(jnp.mean(x * x, axis=-1, keepdims=True) + EPS) * g


def _dot(a, b):
    return jnp.dot(a, b, preferred_element_type=F32)


def _dot_nt(a, b):
    return lax.dot_general(a, b, (((1,), (1,)), ((), ())), preferred_element_type=F32)


def _inproj_kernel(x_ref, g_ref, w_ref, o_ref, *, col_tile):
    h = _rms(x_ref[...], g_ref[...]).astype(BF16)
    for c in range(w_ref.shape[1] // col_tile):
        sl = slice(c * col_tile, (c + 1) * col_tile)
        o_ref[:, sl] = _dot(h, w_ref[:, sl]).astype(BF16)


def _inproj(x2, g, w, l):
    T, D = x2.shape
    N = w.shape[-1]
    tm = TOKEN_TILE
    return pl.pallas_call(
        functools.partial(_inproj_kernel, col_tile=N // 3),
        grid=(T // tm,),
        in_specs=[
            pl.BlockSpec((tm, D), lambda i: (i, 0)),
            pl.BlockSpec((None, 1, D), lambda i: (l, 0, 0)),
            pl.BlockSpec((None, D, N), lambda i: (l, 0, 0)),
        ],
        out_specs=pl.BlockSpec((tm, N), lambda i: (i, 0)),
        out_shape=jax.ShapeDtypeStruct((T, N), BF16),
        compiler_params=_params("parallel"),
        name="inproj",
    )(x2, g, w)


def _sb_kernel(q_ref, k_ref, v_ref, suf_ref, o_ref, kt_ref, z_ref, acc_ref, c_ref, *, tile):
    S = q_ref.shape[0]
    lane = lax.broadcasted_iota(jnp.int32, (1, LANES), 1)
    lo_half = lane < HEAD_DIM
    row = lax.broadcasted_iota(jnp.int32, (tile, tile), 0)
    col = lax.broadcasted_iota(jnp.int32, (tile, tile), 1)
    strict = col < row
    suffix = suf_ref[...]

    for j in range(S // tile):
        kt_ref[j] = k_ref[j * tile:(j + 1) * tile, :].T

    heads = range(2)
    sl = [slice(h * tile, (h + 1) * tile) for h in heads]

    def log_terms(z, masked):
        n = range(len(z))
        log_beta = [jnp.minimum(z[u], 0.0) - jnp.log(1.0 + jnp.exp(-jnp.abs(z[u]))) for u in n]
        log_om = [log_beta[u] - z[u] for u in n]
        log_om = [jnp.where(strict, log_om[u], 0.0) if masked[u] else log_om[u] for u in n]
        hi = [log_om[u].astype(BF16) for u in n]
        lo = [(log_om[u] - hi[u].astype(F32)).astype(BF16) for u in n]
        tail = [_dot(jnp.concatenate([hi[u], lo[u]], axis=0), suffix) for u in n]
        tail = [tail[u][:tile] + tail[u][tile:] for u in n]
        row_sum = [jnp.sum(log_om[u], axis=-1, keepdims=True) for u in n]
        return log_beta, tail, row_sum

    def q_block(i, _):
        q0 = pl.multiple_of(i * tile, tile)
        q2 = q_ref[pl.ds(q0, tile), :] * jnp.asarray(HEAD_DIM ** -0.5, BF16)
        zero = jnp.zeros_like(q2)
        qs = [jnp.where(lo_half, q2, zero), jnp.where(lo_half, zero, q2)]

        def key_rows(j):
            return pl.ds(pl.multiple_of(j * tile, tile), tile)

        def prefetch_scores(j, slot):
            kt = kt_ref[jnp.maximum(j, 0)]
            for h in heads:
                z_ref[slot, sl[h]] = _dot(qs[h], kt)

        def first_block():
            kt = kt_ref[0]
            z = [_dot(qs[h], kt) for h in heads]
            log_beta, tail, row_sum = log_terms(z, [True, True])
            a = [jnp.where(strict, jnp.exp(log_beta[h] + tail[h]), 0.0) for h in heads]
            v_d = v_ref[0:tile, :]
            for h in heads:
                acc_ref[sl[h]] = _dot(a[h].astype(BF16), v_d)
                c_ref[sl[h]] = row_sum[h]
            return jnp.float32(0.0)

        def later_block():
            kt_d, kt_o = kt_ref[i], kt_ref[i - 1]
            z = [_dot(qs[h], kt_d) for h in heads] + [_dot(qs[h], kt_o) for h in heads]
            prefetch_scores(i - 2, 0)
            log_beta, tail, row_sum = log_terms(z, [True, True, False, False])
            a_d = [jnp.where(strict, jnp.exp(log_beta[h] + tail[h]), 0.0) for h in heads]
            a_o = [jnp.exp(log_beta[2 + h] + tail[2 + h] + row_sum[h]) for h in heads]
            v_d, v_o = v_ref[key_rows(i), :], v_ref[key_rows(i - 1), :]
            c = [row_sum[h] + row_sum[2 + h] for h in heads]
            for h in heads:
                acc_ref[sl[h]] = _dot(a_d[h].astype(BF16), v_d) + _dot(a_o[h].astype(BF16), v_o)
                c_ref[sl[h]] = c[h]
            return jnp.maximum(jnp.max(c[0]), jnp.max(c[1]))

        c_max = lax.cond(i == 0, first_block, later_block)

        def key_tile(state):
            j, slot, _ = state
            z = [z_ref[slot, sl[h]] for h in heads]
            prefetch_scores(j - 1, 1 - slot)
            log_beta, tail, row_sum = log_terms(z, [False, False])
            c = [c_ref[sl[h]] for h in heads]
            a = [jnp.exp(log_beta[h] + tail[h] + c[h]) for h in heads]
            c = [c[h] + row_sum[h] for h in heads]
            v_j = v_ref[key_rows(j), :]
            for h in heads:
                acc_ref[sl[h]] += _dot(a[h].astype(BF16), v_j)
                c_ref[sl[h]] = c[h]
            return j - 1, 1 - slot, jnp.maximum(jnp.max(c[0]), jnp.max(c[1]))

        lax.while_loop(lambda s: (s[0] >= 0) & (s[2] > SB_LOG_UNDERFLOW), key_tile,
                       (i - 2, jnp.int32(0), c_max))
        o_ref[pl.ds(q0, tile), :] = jnp.where(lo_half, acc_ref[:tile], acc_ref[tile:])
        return 0

    lax.fori_loop(0, S // tile, q_block, 0)


def _sb_attention(proj3):
    B, S, _ = proj3.shape
    pairs = SB_WIDTH // LANES
    tile = min(SB_TILE, S)
    suf = jnp.asarray(np.arange(tile)[:, None] > np.arange(tile)[None, :], BF16)
    spec = lambda off: pl.BlockSpec((None, S, LANES), lambda b, p: (b, 0, off + p))
    return pl.pallas_call(
        functools.partial(_sb_kernel, tile=tile),
        grid=(B, pairs),
        in_specs=[spec(0), spec(pairs), spec(2 * pairs),
                  pl.BlockSpec((tile, tile), lambda b, p: (0, 0))],
        out_specs=pl.BlockSpec((None, S, LANES), lambda b, p: (b, 0, p)),
        out_shape=jax.ShapeDtypeStruct((B, S, SB_WIDTH), F32),
        scratch_shapes=[pltpu.VMEM((S // tile, LANES, tile), BF16),
                        pltpu.VMEM((2, 2 * tile, tile), F32),
                        pltpu.VMEM((2 * tile, LANES), F32), pltpu.VMEM((2 * tile, 1), F32)],
        compiler_params=_params("parallel", "parallel"),
        name="sb_attention",
    )(proj3, proj3, proj3, suf)


def _group_mean_sq(x, ones_ref):
    sq = x * x
    hi = sq.astype(BF16)
    lo = (sq - hi.astype(F32)).astype(BF16)
    return _dot(hi, ones_ref[...]) + _dot(lo, ones_ref[...])


def _swa_kernel(sink_ref, q_ref, kp_ref, kc_ref, vp_ref, vc_ref, qg_ref, kg_ref, gq_ref, gk_ref, sel_ref,
                bias_ref, o_ref):
    q = q_ref[...].astype(F32)
    k = jnp.concatenate([kp_ref[...], kc_ref[...]], axis=0).astype(F32)
    v = jnp.concatenate([vp_ref[...], vc_ref[...]], axis=0)
    half = SWA_WIDTH // 2
    q_ms = jnp.concatenate([_group_mean_sq(q[:, :half], gq_ref), _group_mean_sq(q[:, half:], gq_ref)], axis=1)
    qn = (q * lax.rsqrt(q_ms + EPS) * qg_ref[...] * (HEAD_DIM ** -0.5)).astype(BF16)
    kn_t = (k * lax.rsqrt(_group_mean_sq(k, gk_ref) + EPS) * kg_ref[...]).astype(BF16).T
    lane = lax.broadcasted_iota(jnp.int32, (1, LANES), 1)
    lo_half = lane < HEAD_DIM
    zero = jnp.zeros((WINDOW, LANES), BF16)
    W = WINDOW
    head_row = lax.broadcasted_iota(jnp.int32, (SWA_GROUP * W, 1), 0) // W
    groups = range(SWA_KV_HEADS)
    kt_g = [kn_t[g * HEAD_DIM:(g + 1) * HEAD_DIM, :] for g in groups]
    kt_dup = [jnp.concatenate([kt_g[g], kt_g[g]], axis=0) for g in groups]
    v_dup = [_dot(v, sel_ref[g]).astype(BF16) for g in groups]
    lhs = []
    for g in groups:
        rows = []
        for t in range(g * SWA_GROUP // 2, (g + 1) * SWA_GROUP // 2):
            qt = qn[:, t * LANES:(t + 1) * LANES]
            rows += [jnp.where(lo_half, qt, zero), jnp.where(lo_half, zero, qt)]
        lhs.append(jnp.concatenate(rows, axis=0))
    s = [_dot(lhs[g], kt_dup[g]) for g in groups]
    s = [s[g] + bias_ref[g * SWA_GROUP:(g + 1) * SWA_GROUP].reshape(SWA_GROUP * W, 2 * W) for g in groups]
    sink = []
    for g in groups:
        sk = jnp.zeros((SWA_GROUP * W, 1), F32)
        for hh in range(SWA_GROUP):
            sk = jnp.where(head_row == hh, sink_ref[g * SWA_GROUP + hh], sk)
        sink.append(sk)
    m = [jnp.maximum(jnp.max(s[g], axis=-1, keepdims=True), sink[g]) for g in groups]
    p = [jnp.exp(s[g] - m[g]) for g in groups]
    den = [jnp.sum(p[g], axis=-1, keepdims=True) + jnp.exp(sink[g] - m[g]) for g in groups]
    r = [_dot(p[g].astype(BF16), v_dup[g]) / den[g] for g in groups]
    for g in groups:
        for tt in range(SWA_GROUP // 2):
            t = g * SWA_GROUP // 2 + tt
            o_ref[:, t * LANES:(t + 1) * LANES] = jnp.where(
                lo_half, r[g][2 * tt * W:(2 * tt + 1) * W], r[g][(2 * tt + 1) * W:(2 * tt + 2) * W])


def _swa_attention(proj3, sinks, q_gain, k_gain, bias):
    B, S, _ = proj3.shape
    W = WINDOW
    q_blk = (3 * SB_WIDTH) // SWA_WIDTH
    k_blk = (3 * SB_WIDTH + SWA_WIDTH) // LANES
    v_blk = k_blk + 1
    group = lambda n: (np.arange(n)[:, None] // HEAD_DIM == np.arange(n)[None, :] // HEAD_DIM) / HEAD_DIM
    ones_q = jnp.asarray(group(SWA_WIDTH // 2), BF16)
    ones_k = jnp.asarray(group(LANES), BF16)
    sel = jnp.asarray(np.stack([np.arange(LANES)[:, None] == g * HEAD_DIM + np.arange(LANES)[None, :] % HEAD_DIM
                                for g in range(SWA_KV_HEADS)]), BF16)
    prev = lambda c: pl.BlockSpec((None, W, LANES), lambda b, n: (b, jnp.maximum(n - 1, 0), c))
    cur = lambda c: pl.BlockSpec((None, W, LANES), lambda b, n: (b, n, c))
    const = lambda *shape: pl.BlockSpec(shape, lambda b, n: (0,) * len(shape))
    return pl.pallas_call(
        _swa_kernel,
        grid=(B, S // W),
        in_specs=[
            pl.BlockSpec(memory_space=pltpu.SMEM),
            pl.BlockSpec((None, W, SWA_WIDTH), lambda b, n: (b, n, q_blk)),
            prev(k_blk), cur(k_blk), prev(v_blk), cur(v_blk),
            const(1, SWA_WIDTH), const(1, LANES),
            const(SWA_WIDTH // 2, SWA_WIDTH // 2), const(LANES, LANES), const(SWA_KV_HEADS, LANES, LANES),
            pl.BlockSpec((None, SWA_Q_HEADS, W, 2 * W), lambda b, n: (jnp.minimum(n, 1), 0, 0, 0)),
        ],
        out_specs=pl.BlockSpec((None, W, SWA_WIDTH), lambda b, n: (b, n, 0)),
        out_shape=jax.ShapeDtypeStruct((B, S, SWA_WIDTH), F32),
        compiler_params=_params("arbitrary", "arbitrary"),
        name="swa_attention",
    )(sinks, proj3, proj3, proj3, proj3, proj3, q_gain, k_gain, ones_q, ones_k, sel, bias)


def _outproj_kernel(sb_ref, sw_ref, gsb_ref, gsw_ref, w_ref, x_ref, o_ref):
    a = _rms(sb_ref[...], gsb_ref[...]).astype(BF16)
    b = _rms(sw_ref[...], gsw_ref[...]).astype(BF16)
    o_ref[...] = x_ref[...] + _dot(a, w_ref[:SB_WIDTH, :]) + _dot(b, w_ref[SB_WIDTH:, :])


def _outproj(sb_o, sw_o, g_sb, g_sw, w, x2, l):
    T, D = x2.shape
    tm = TOKEN_TILE
    return pl.pallas_call(
        _outproj_kernel,
        grid=(T // tm,),
        in_specs=[
            pl.BlockSpec((tm, SB_WIDTH), lambda i: (i, 0)),
            pl.BlockSpec((tm, SWA_WIDTH), lambda i: (i, 0)),
            pl.BlockSpec((None, 1, SB_WIDTH), lambda i: (l, 0, 0)),
            pl.BlockSpec((None, 1, SWA_WIDTH), lambda i: (l, 0, 0)),
            pl.BlockSpec((None, SB_WIDTH + SWA_WIDTH, D), lambda i: (l, 0, 0)),
            pl.BlockSpec((tm, D), lambda i: (i, 0)),
        ],
        out_specs=pl.BlockSpec((tm, D), lambda i: (i, 0)),
        out_shape=jax.ShapeDtypeStruct((T, D), F32),
        compiler_params=_params("parallel"),
        name="outproj",
    )(sb_o, sw_o, g_sb, g_sw, w, x2)


def _memkv_kernel(m_ref, g_ref, w_ref, kg_ref, k_ref, v_ref):
    h = _rms(m_ref[...], g_ref[...]).astype(BF16)
    kv = _dot(h, w_ref[...])
    for hd in range(MEM_HEADS):
        sl = slice(hd * MEM_HEAD_DIM, (hd + 1) * MEM_HEAD_DIM)
        k_ref[:, sl] = _rms(kv[:, sl], kg_ref[...]).astype(BF16)
    v_ref[...] = kv[:, MEM_WIDTH:].astype(BF16)


def _memkv(mem, norm_mem, wkv, k_gain):
    B, N, D = mem.shape
    L = wkv.shape[0]
    out = jax.ShapeDtypeStruct((L, B, N, MEM_WIDTH), BF16)
    return pl.pallas_call(
        _memkv_kernel,
        grid=(L, B),
        in_specs=[
            pl.BlockSpec((None, N, D), lambda l, b: (b, 0, 0)),
            pl.BlockSpec((None, 1, D), lambda l, b: (l, 0, 0)),
            pl.BlockSpec((None, D, 2 * MEM_WIDTH), lambda l, b: (l, 0, 0)),
            pl.BlockSpec((None, 1, MEM_HEAD_DIM), lambda l, b: (l, 0, 0)),
        ],
        out_specs=[pl.BlockSpec((None, None, N, MEM_WIDTH), lambda l, b: (l, b, 0, 0))] * 2,
        out_shape=[out, out],
        compiler_params=_params("parallel", "parallel"),
        name="memkv",
    )(mem, norm_mem, wkv, k_gain)


def _xattn_kernel(x_ref, gx_ref, wq_ref, qg_ref, k_ref, v_ref, wo_ref, o_ref):
    x = x_ref[...]
    h = _rms(x, gx_ref[...]).astype(BF16)
    q = _dot(h, wq_ref[...])
    outs = []
    for hd in range(MEM_HEADS):
        sl = slice(hd * MEM_HEAD_DIM, (hd + 1) * MEM_HEAD_DIM)
        qn = _rms(q[:, sl], qg_ref[...]).astype(BF16)
        s = _dot_nt(qn, k_ref[:, sl]) * (MEM_HEAD_DIM ** -0.5)
        p = jnp.exp(s - jnp.max(s, axis=-1, keepdims=True))
        den = jnp.sum(p, axis=-1, keepdims=True)
        outs.append((_dot(p.astype(BF16), v_ref[:, sl]) / den).astype(BF16))
    o_ref[...] = x + _dot(jnp.concatenate(outs, axis=1), wo_ref[...])


def _xattn(x3, norm_x, wq, q_gain, kmem, vmem, wo, l):
    B, S, D = x3.shape
    N = kmem.shape[2]
    tm = min(TOKEN_TILE, S)
    return pl.pallas_call(
        _xattn_kernel,
        grid=(B, S // tm),
        in_specs=[
            pl.BlockSpec((None, tm, D), lambda b, i: (b, i, 0)),
            pl.BlockSpec((None, 1, D), lambda b, i: (l, 0, 0)),
            pl.BlockSpec((None, D, MEM_WIDTH), lambda b, i: (l, 0, 0)),
            pl.BlockSpec((None, 1, MEM_HEAD_DIM), lambda b, i: (l, 0, 0)),
            pl.BlockSpec((None, None, N, MEM_WIDTH), lambda b, i: (l, b, 0, 0)),
            pl.BlockSpec((None, None, N, MEM_WIDTH), lambda b, i: (l, b, 0, 0)),
            pl.BlockSpec((None, MEM_WIDTH, D), lambda b, i: (l, 0, 0)),
        ],
        out_specs=pl.BlockSpec((None, tm, D), lambda b, i: (b, i, 0)),
        out_shape=jax.ShapeDtypeStruct((B, S, D), F32),
        compiler_params=_params("parallel", "parallel"),
        name="xattn",
    )(x3, norm_x, wq, q_gain, kmem, vmem, wo)


def _swiglu_step(h, wg_ref, wu_ref, wd_ref):
    a = _dot(h, wg_ref[...])
    u = _dot(h, wu_ref[...])
    act = a / (1.0 + jnp.exp(-a)) * u
    return _dot(act.astype(BF16), wd_ref[...])


def _ffn_kernel(x_ref, g_ref, wg_ref, wu_ref, wd_ref, o_ref, h_ref):
    @pl.when(pl.program_id(1) == 0)
    def _():
        x = x_ref[...]
        h_ref[...] = _rms(x, g_ref[...]).astype(BF16)
        o_ref[...] = x

    o_ref[...] += _swiglu_step(h_ref[...], wg_ref, wu_ref, wd_ref)


def _ffn_dense(x2, g, wg, wu, wd, l, i_dense):
    T, D = x2.shape
    F = wg.shape[-1]
    tm, tf = min(FFN_TOKEN_TILE, T), FFN_COL_TILE
    return pl.pallas_call(
        _ffn_kernel,
        grid=(T // tm, F // tf),
        in_specs=[
            pl.BlockSpec((tm, D), lambda i, f: (i, 0)),
            pl.BlockSpec((None, 1, D), lambda i, f: (l, 0, 0)),
            pl.BlockSpec((None, D, tf), lambda i, f: (i_dense, 0, f)),
            pl.BlockSpec((None, D, tf), lambda i, f: (i_dense, 0, f)),
            pl.BlockSpec((None, tf, D), lambda i, f: (i_dense, f, 0)),
        ],
        out_specs=pl.BlockSpec((tm, D), lambda i, f: (i, 0)),
        out_shape=jax.ShapeDtypeStruct((T, D), F32),
        scratch_shapes=[pltpu.VMEM((tm, D), BF16)],
        compiler_params=_params("parallel", "arbitrary"),
        name="ffn_dense",
    )(x2, g, wg, wu, wd)


def _moe_ffn_kernel(be_ref, nu_ref, xs_ref, wg_ref, wu_ref, wd_ref, o_ref, h_ref):
    del be_ref
    used = pl.program_id(0) < nu_ref[0]

    @pl.when(pl.program_id(1) == 0)
    def _():
        o_ref[...] = jnp.zeros_like(o_ref)

    @pl.when(used & (pl.program_id(1) == 0))
    def _():
        h_ref[...] = xs_ref[...].astype(BF16)

    @pl.when(used)
    def _():
        o_ref[...] += _swiglu_step(h_ref[...], wg_ref, wu_ref, wd_ref)


def _moe_ffn(blk_e, n_used, xs, wg, wu, wd, i_moe):
    P, D = xs.shape
    F = wg.shape[-1]
    tm, tf = MOE_ROW_TILE, FFN_COL_TILE
    nf = F // tf
    row_blk = lambda i, nu: jnp.minimum(i, nu[0] - 1)
    col_blk = lambda i, f, nu: jnp.where(i < nu[0], f, nf - 1)
    return pl.pallas_call(
        _moe_ffn_kernel,
        grid_spec=pltpu.PrefetchScalarGridSpec(
            num_scalar_prefetch=2,
            grid=(P // tm, nf),
            in_specs=[
                pl.BlockSpec((tm, D), lambda i, f, be, nu: (row_blk(i, nu), 0)),
                pl.BlockSpec((None, None, D, tf), lambda i, f, be, nu: (i_moe, be[i], 0, col_blk(i, f, nu))),
                pl.BlockSpec((None, None, D, tf), lambda i, f, be, nu: (i_moe, be[i], 0, col_blk(i, f, nu))),
                pl.BlockSpec((None, None, tf, D), lambda i, f, be, nu: (i_moe, be[i], col_blk(i, f, nu), 0)),
            ],
            out_specs=pl.BlockSpec((tm, D), lambda i, f, be, nu: (i, 0)),
            scratch_shapes=[pltpu.VMEM((tm, D), BF16)],
        ),
        out_shape=jax.ShapeDtypeStruct((P, D), F32),
        compiler_params=_params("arbitrary", "arbitrary"),
        name="moe_ffn",
    )(blk_e, n_used, xs, wg, wu, wd)


def _router_kernel(x_ref, g_ref, rw_ref, rb_ref, hf_ref, ei_ref, gt_ref, cnt_ref, carry_ref):
    tm = x_ref.shape[0]

    @pl.when(pl.program_id(0) == 0)
    def _():
        carry_ref[...] = jnp.zeros_like(carry_ref)

    hf = _rms(x_ref[...], g_ref[...])
    hf_ref[...] = hf
    lane = lax.broadcasted_iota(jnp.int32, (tm, LANES), 1)
    logits = jnp.dot(hf, rw_ref[...], precision=lax.Precision.HIGHEST, preferred_element_type=F32) + rb_ref[...]
    logits = jnp.where(lane < N_EXPERTS, logits, -jnp.inf)
    m1 = jnp.max(logits, axis=-1, keepdims=True)
    i1 = jnp.min(jnp.where(logits == m1, lane, LANES), axis=-1, keepdims=True)
    rest = jnp.where(lane == i1, -jnp.inf, logits)
    m2 = jnp.max(rest, axis=-1, keepdims=True)
    i2 = jnp.min(jnp.where(rest == m2, lane, LANES), axis=-1, keepdims=True)
    e = jnp.exp(m2 - m1)
    g1 = 1.0 / (1.0 + e)
    g2 = e / (1.0 + e)
    oh1 = lane == i1
    oh2 = lane == i2
    oh = (oh1 | oh2).astype(F32)
    row = lax.broadcasted_iota(jnp.int32, (tm, tm), 0)
    col = lax.broadcasted_iota(jnp.int32, (tm, tm), 1)
    earlier = (row > col).astype(BF16)
    before = _dot(earlier, oh.astype(BF16)) + carry_ref[...]
    r1 = jnp.sum(jnp.where(oh1, before, 0.0), axis=-1, keepdims=True).astype(jnp.int32)
    r2 = jnp.sum(jnp.where(oh2, before, 0.0), axis=-1, keepdims=True).astype(jnp.int32)
    carry_ref[...] += jnp.sum(oh, axis=0, keepdims=True)
    cnt_ref[...] = carry_ref[...]
    ei = jnp.where(lane == 0, i1, jnp.where(lane == 1, i2, jnp.where(lane == 2, r1, jnp.where(lane == 3, r2, 0))))
    ei_ref[...] = ei[:, :8]
    gt = jnp.where(lane == 0, g1, jnp.where(lane == 1, g2, 0.0))
    gt_ref[...] = gt[:, :8]


def _router(x2, g, rw, rb, l, i_moe):
    T, D = x2.shape
    tm = TOKEN_TILE
    return pl.pallas_call(
        _router_kernel,
        grid=(T // tm,),
        in_specs=[
            pl.BlockSpec((tm, D), lambda i: (i, 0)),
            pl.BlockSpec((None, 1, D), lambda i: (l, 0, 0)),
            pl.BlockSpec((None, D, LANES), lambda i: (i_moe, 0, 0)),
            pl.BlockSpec((None, 1, LANES), lambda i: (i_moe, 0, 0)),
        ],
        out_specs=[
            pl.BlockSpec((tm, D), lambda i: (i, 0)),
            pl.BlockSpec((tm, 8), lambda i: (i, 0)),
            pl.BlockSpec((tm, 8), lambda i: (i, 0)),
            pl.BlockSpec((1, LANES), lambda i: (0, 0)),
        ],
        out_shape=[
            jax.ShapeDtypeStruct((T, D), F32),
            jax.ShapeDtypeStruct((T, 8), jnp.int32),
            jax.ShapeDtypeStruct((T, 8), F32),
            jax.ShapeDtypeStruct((1, LANES), F32),
        ],
        scratch_shapes=[pltpu.VMEM((1, LANES), F32)],
        compiler_params=_params("arbitrary"),
        name="router",
    )(x2, g, rw, rb)


def _dispatch_kernel(last_blk_ref, dest_ref, hf_ref, xs_ref, zero_ref, sem):
    tm = hf_ref.shape[0]

    @pl.when(pl.program_id(0) == 0)
    def _():
        zero_ref[...] = jnp.zeros_like(zero_ref)
        blk_rows = zero_ref.shape[0]

        def fill(e):
            start = pl.multiple_of(last_blk_ref[e] * blk_rows, blk_rows)
            return pltpu.make_async_copy(zero_ref, xs_ref.at[pl.ds(start, blk_rows)], sem)

        for e in range(N_EXPERTS):
            fill(e).start()
        for e in range(N_EXPERTS):
            fill(e).wait()

    def row_copy(r, d):
        return pltpu.make_async_copy(hf_ref.at[pl.ds(r, 1)], xs_ref.at[pl.ds(d, 1)], sem)

    def issue(r, _):
        row_copy(r, dest_ref[2 * r]).start()
        row_copy(r, dest_ref[2 * r + 1]).start()
        return 0

    lax.fori_loop(0, tm, issue, 0, unroll=DMA_ISSUE_UNROLL)
    for _ in range(2):
        pltpu.make_async_copy(hf_ref, xs_ref.at[pl.ds(0, tm)], sem).wait()


def _dispatch(last_blk, dest, hf, n_rows):
    T, D = hf.shape
    tm = TOKEN_TILE
    return pl.pallas_call(
        _dispatch_kernel,
        grid_spec=pltpu.PrefetchScalarGridSpec(
            num_scalar_prefetch=1,
            grid=(T // tm,),
            in_specs=[
                pl.BlockSpec((2 * tm,), lambda i, lb: (i,), memory_space=pltpu.SMEM),
                pl.BlockSpec((tm, D), lambda i, lb: (i, 0)),
            ],
            out_specs=pl.BlockSpec(memory_space=pl.ANY),
            scratch_shapes=[pltpu.VMEM((MOE_ROW_TILE, D), F32), pltpu.SemaphoreType.DMA(())],
        ),
        out_shape=jax.ShapeDtypeStruct((n_rows, D), F32),
        compiler_params=_params("arbitrary"),
        name="moe_dispatch",
    )(last_blk, dest, hf)


def _combine_kernel(dest_ref, x_ref, gt_ref, y_ref, o_ref, buf_ref, sem):
    tm = x_ref.shape[0]

    def row_copy(slot, r, d):
        return pltpu.make_async_copy(y_ref.at[pl.ds(d, 1)], buf_ref.at[slot, pl.ds(r, 1)], sem)

    def issue(r, _):
        row_copy(0, r, dest_ref[2 * r]).start()
        row_copy(1, r, dest_ref[2 * r + 1]).start()
        return 0

    lax.fori_loop(0, tm, issue, 0, unroll=DMA_ISSUE_UNROLL)
    for slot in range(2):
        pltpu.make_async_copy(y_ref.at[pl.ds(0, tm)], buf_ref.at[slot], sem).wait()
    gt = gt_ref[...]
    o_ref[...] = x_ref[...] + gt[:, 0:1] * buf_ref[0] + gt[:, 1:2] * buf_ref[1]


def _combine(dest, x2, gates, y):
    T, D = x2.shape
    tm = TOKEN_TILE
    return pl.pallas_call(
        _combine_kernel,
        grid=(T // tm,),
        in_specs=[
            pl.BlockSpec((2 * tm,), lambda i: (i,), memory_space=pltpu.SMEM),
            pl.BlockSpec((tm, D), lambda i: (i, 0)),
            pl.BlockSpec((tm, 8), lambda i: (i, 0)),
            pl.BlockSpec(memory_space=pl.ANY),
        ],
        out_specs=pl.BlockSpec((tm, D), lambda i: (i, 0)),
        out_shape=jax.ShapeDtypeStruct((T, D), F32),
        scratch_shapes=[pltpu.VMEM((2, tm, D), F32), pltpu.SemaphoreType.DMA(())],
        compiler_params=_params("arbitrary"),
        name="moe_combine",
    )(dest, x2, gates, y)


def _moe(x2, g, rw, rb, wg, wu, wd, l, i_moe):
    T, D = x2.shape
    tmm = MOE_ROW_TILE
    P = 2 * T + N_EXPERTS * tmm
    hf, ei, gates, cnt = _router(x2, g, rw, rb, l, i_moe)
    counts = cnt[0, :N_EXPERTS].astype(jnp.int32)
    pcounts = (counts + tmm - 1) // tmm * tmm
    pends = jnp.cumsum(pcounts)
    pstarts = pends - pcounts
    experts = ei[:, 0:2]
    start_of = sum(jnp.where(experts == e, pstarts[e], 0) for e in range(N_EXPERTS))
    dest = (start_of + ei[:, 2:4]).reshape(-1)
    blk_start = jnp.arange(P // tmm, dtype=jnp.int32) * tmm
    blk_e = jnp.minimum(jnp.sum(blk_start[:, None] >= pends[None, :], axis=1), N_EXPERTS - 1).astype(jnp.int32)
    n_used = (pends[N_EXPERTS - 1:] // tmm).astype(jnp.int32)
    last_blk = jnp.maximum(pends // tmm - 1, 0).astype(jnp.int32)
    xs = _dispatch(last_blk, dest, hf, P)
    y = _moe_ffn(blk_e, n_used, xs, wg, wu, wd, i_moe)
    return _combine(dest, x2, gates, y)


def _t5_buckets(dist):
    n = np.maximum(dist, 0)
    max_exact = N_BUCKETS // 2
    large = max_exact + (np.log(np.maximum(n, 1) / max_exact) / np.log(MAX_DISTANCE / max_exact)
                         * (N_BUCKETS - max_exact)).astype(np.int32)
    large = np.minimum(large, N_BUCKETS - 1)
    return np.where(n < max_exact, n, large).astype(np.int32)


def kernel(x, mem, norm_mix, w_in, sb_out_gain, swa_q_gain, swa_k_gain, swa_sinks, swa_out_gain, rel_bias, w_out, norm_xattn, norm_mem, xattn_wq, xattn_wkv, xattn_q_gain, xattn_k_gain, xattn_wo, norm_ffn, dense_w_gate, dense_w_up, dense_w_down, router_w, router_b, exp_w_gate, exp_w_up, exp_w_down):
    B, S, D = x.shape
    depth = w_in.shape[0]
    T = B * S
    row3 = lambda a: a.reshape(a.shape[0], 1, a.shape[1])
    bf = lambda a: a.astype(BF16)

    dist = WINDOW + np.arange(WINDOW)[:, None] - np.arange(2 * WINDOW)[None, :]
    swa_bias = jnp.transpose(rel_bias[_t5_buckets(dist)], (2, 0, 1))
    band = (dist >= 0) & (dist < WINDOW)
    band = np.stack([band & (np.arange(2 * WINDOW)[None, :] >= WINDOW), band])
    swa_bias = jnp.where(band[:, None], swa_bias[None], -jnp.inf)
    router_w_p = jnp.pad(router_w, ((0, 0), (0, 0), (0, LANES - N_EXPERTS)))
    router_b_p = row3(jnp.pad(router_b, ((0, 0), (0, LANES - N_EXPERTS))))

    w_in_b, w_out_b = bf(w_in), bf(w_out)
    wq_b, wkv_b, wo_b = bf(xattn_wq), bf(xattn_wkv), bf(xattn_wo)
    dg_b, du_b, dd_b = bf(dense_w_gate), bf(dense_w_up), bf(dense_w_down)
    eg_b, eu_b, ed_b = bf(exp_w_gate), bf(exp_w_up), bf(exp_w_down)
    norm_mix3, norm_x3, norm_f3 = row3(norm_mix), row3(norm_xattn), row3(norm_ffn)
    sb_g3, sw_g3 = row3(sb_out_gain), row3(swa_out_gain)
    xq_g3, xk_g3 = row3(xattn_q_gain), row3(xattn_k_gain)

    kmem, vmem = _memkv(mem, row3(norm_mem), wkv_b, xk_g3)

    x2 = x.reshape(T, D)
    for l in range(depth):
        proj = _inproj(x2, norm_mix3, w_in_b, l).reshape(B, S, -1)
        sb_o = _sb_attention(proj)
        sw_o = _swa_attention(proj, swa_sinks[l], jnp.tile(swa_q_gain[l], SWA_Q_HEADS)[None, :],
                              jnp.tile(swa_k_gain[l], SWA_KV_HEADS)[None, :], swa_bias)
        x2 = _outproj(sb_o.reshape(T, -1), sw_o.reshape(T, -1), sb_g3, sw_g3, w_out_b, x2, l)
        x2 = _xattn(x2.reshape(B, S, D), norm_x3, wq_b, xq_g3, kmem, vmem, wo_b, l).reshape(T, D)
        if l % 2 == 0:
            x2 = _ffn_dense(x2, norm_f3, dg_b, du_b, dd_b, l, l // 2)
        else:
            x2 = _moe(x2, norm_f3, router_w_p, router_b_p, eg_b, eu_b, ed_b, l, l // 2)
    return x2.reshape(B, S, D)
```

```python
import functools

import numpy as np
import jax
import jax.numpy as jnp
from jax import lax
from jax.experimental import pallas as pl
from jax.experimental.pallas import tpu as pltpu

F32 = jnp.float32
BF16 = jnp.bfloat16

HEAD_DIM = 64
SB_WIDTH = 512
SWA_WIDTH = 512
SWA_Q_HEADS = 8
SWA_GROUP = 4
SWA_KV_HEADS = 2
WINDOW = 128
N_BUCKETS = 32
MAX_DISTANCE = 128
MEM_HEADS = 4
MEM_HEAD_DIM = 128
MEM_WIDTH = 512
N_EXPERTS = 8
EPS = 1e-6
LANES = 128

VMEM_LIMIT = 56 * 1024 * 1024

TOKEN_TILE = 512
FFN_TOKEN_TILE = 1024
FFN_COL_TILE = 512
SB_TILE = 256
MOE_ROW_TILE = 1024
DMA_ISSUE_UNROLL = 8
SB_LOG_UNDERFLOW = -104.0


def _params(*sem):
    return pltpu.CompilerParams(dimension_semantics=("arbitrary",) * len(sem), vmem_limit_bytes=VMEM_LIMIT)


def _rms(x, g):
    return x * lax.rsqrt(jnp.mean(x * x, axis=-1, keepdims=True) + EPS) * g


def _dot(a, b):
    return jnp.dot(a, b, preferred_element_type=F32)


def _dot_nt(a, b):
    return lax.dot_general(a, b, (((1,), (1,)), ((), ())), preferred_element_type=F32)


def _group_mean_sq(x, ones_ref):
    sq = x * x
    hi = sq.astype(BF16)
    lo = (sq - hi.astype(F32)).astype(BF16)
    return _dot(hi, ones_ref[...]) + _dot(lo, ones_ref[...])


def _inproj_kernel(x_ref, g_ref, w_ref, hg_ref, ones_q_ref, ones_k_ref, o_ref):
    h = _rms(x_ref[...], g_ref[...]).astype(BF16)
    sb_cols = 3 * SB_WIDTH
    for c in range(3):
        sl = slice(c * SB_WIDTH, (c + 1) * SB_WIDTH)
        o_ref[:, sl] = _dot(h, w_ref[:, sl]).astype(BF16)
    y = _dot(h, w_ref[:, sb_cols:])
    half = SWA_WIDTH // 2
    qk = SWA_WIDTH + SWA_KV_HEADS * HEAD_DIM
    ms = jnp.concatenate([_group_mean_sq(y[:, :half], ones_q_ref),
                          _group_mean_sq(y[:, half:SWA_WIDTH], ones_q_ref),
                          _group_mean_sq(y[:, SWA_WIDTH:qk], ones_k_ref)], axis=1)
    o_ref[:, sb_cols:sb_cols + qk] = (y[:, :qk] * lax.rsqrt(ms + EPS) * hg_ref[...]).astype(BF16)
    o_ref[:, sb_cols + qk:] = y[:, qk:].astype(BF16)


def _head_group_ones(n):
    return jnp.asarray((np.arange(n)[:, None] // HEAD_DIM == np.arange(n)[None, :] // HEAD_DIM) / HEAD_DIM, BF16)


def _inproj(x2, g, w, head_gain, l):
    T, D = x2.shape
    N = w.shape[-1]
    tm = TOKEN_TILE
    half, kv = SWA_WIDTH // 2, SWA_KV_HEADS * HEAD_DIM
    const = lambda *shape: pl.BlockSpec(shape, lambda i: (0,) * len(shape))
    return pl.pallas_call(
        _inproj_kernel,
        grid=(T // tm,),
        in_specs=[
            pl.BlockSpec((tm, D), lambda i: (i, 0)),
            pl.BlockSpec((None, 1, D), lambda i: (l, 0, 0)),
            pl.BlockSpec((None, D, N), lambda i: (l, 0, 0)),
            const(1, SWA_WIDTH + kv), const(half, half), const(kv, kv),
        ],
        out_specs=pl.BlockSpec((tm, N), lambda i: (i, 0)),
        out_shape=jax.ShapeDtypeStruct((T, N), BF16),
        compiler_params=_params("arbitrary"),
        name="inproj",
    )(x2, g, w, head_gain, _head_group_ones(half), _head_group_ones(kv))


def _sb_kernel(q_ref, k_ref, v_ref, suf_ref, o_ref, kt_ref, z_ref, acc_ref, c_ref, *, tile):
    S = q_ref.shape[0]
    lane = lax.broadcasted_iota(jnp.int32, (1, LANES), 1)
    lo_half = lane < HEAD_DIM
    row = lax.broadcasted_iota(jnp.int32, (tile, tile), 0)
    col = lax.broadcasted_iota(jnp.int32, (tile, tile), 1)
    strict = col < row
    suffix = suf_ref[...]

    for j in range(S // tile):
        kt_ref[j] = k_ref[j * tile:(j + 1) * tile, :].T

    heads = range(2)
    sl = [slice(h * tile, (h + 1) * tile) for h in heads]

    def log_terms(z, masked):
        n = range(len(z))
        log_beta = [jnp.minimum(z[u], 0.0) - jnp.log(1.0 + jnp.exp(-jnp.abs(z[u]))) for u in n]
        log_om = [log_beta[u] - z[u] for u in n]
        log_om = [jnp.where(strict, log_om[u], 0.0) if masked[u] else log_om[u] for u in n]
        hi = [log_om[u].astype(BF16) for u in n]
        lo = [(log_om[u] - hi[u].astype(F32)).astype(BF16) for u in n]
        tail = [_dot(jnp.concatenate([hi[u], lo[u]], axis=0), suffix) for u in n]
        tail = [tail[u][:tile] + tail[u][tile:] for u in n]
        row_sum = [jnp.sum(log_om[u], axis=-1, keepdims=True) for u in n]
        return log_beta, tail, row_sum

    def q_block(i, _):
        q0 = pl.multiple_of(i * tile, tile)
        q2 = q_ref[pl.ds(q0, tile), :] * jnp.asarray(HEAD_DIM ** -0.5, BF16)
        zero = jnp.zeros_like(q2)
        qs = [jnp.where(lo_half, q2, zero), jnp.where(lo_half, zero, q2)]

        def key_rows(j):
            return pl.ds(pl.multiple_of(j * tile, tile), tile)

        def prefetch_scores(j, slot):
            kt = kt_ref[jnp.maximum(j, 0)]
            for h in heads:
                z_ref[slot, sl[h]] = _dot(qs[h], kt)

        def first_block():
            kt = kt_ref[0]
            z = [_dot(qs[h], kt) for h in heads]
            log_beta, tail, row_sum = log_terms(z, [True, True])
            a = [jnp.where(strict, jnp.exp(log_beta[h] + tail[h]), 0.0) for h in heads]
            v_d = v_ref[0:tile, :]
            for h in heads:
                acc_ref[sl[h]] = _dot(a[h].astype(BF16), v_d)
                c_ref[sl[h]] = row_sum[h]
            return jnp.float32(0.0)

        def later_block():
            kt_d, kt_o = kt_ref[i], kt_ref[i - 1]
            z = [_dot(qs[h], kt_d) for h in heads] + [_dot(qs[h], kt_o) for h in heads]
            prefetch_scores(i - 2, 0)
            log_beta, tail, row_sum = log_terms(z, [True, True, False, False])
            a_d = [jnp.where(strict, jnp.exp(log_beta[h] + tail[h]), 0.0) for h in heads]
            a_o = [jnp.exp(log_beta[2 + h] + tail[2 + h] + row_sum[h]) for h in heads]
            v_d, v_o = v_ref[key_rows(i), :], v_ref[key_rows(i - 1), :]
            c = [row_sum[h] + row_sum[2 + h] for h in heads]
            for h in heads:
                acc_ref[sl[h]] = _dot(a_d[h].astype(BF16), v_d) + _dot(a_o[h].astype(BF16), v_o)
                c_ref[sl[h]] = c[h]
            return jnp.maximum(jnp.max(c[0]), jnp.max(c[1]))

        c_max = lax.cond(i == 0, first_block, later_block)

        def key_tile(state):
            j, slot, _ = state
            z = [z_ref[slot, sl[h]] for h in heads]
            prefetch_scores(j - 1, 1 - slot)
            log_beta, tail, row_sum = log_terms(z, [False, False])
            c = [c_ref[sl[h]] for h in heads]
            a = [jnp.exp(log_beta[h] + tail[h] + c[h]) for h in heads]
            c = [c[h] + row_sum[h] for h in heads]
            v_j = v_ref[key_rows(j), :]
            for h in heads:
                acc_ref[sl[h]] += _dot(a[h].astype(BF16), v_j)
                c_ref[sl[h]] = c[h]
            return j - 1, 1 - slot, jnp.maximum(jnp.max(c[0]), jnp.max(c[1]))

        lax.while_loop(lambda s: (s[0] >= 0) & (s[2] > SB_LOG_UNDERFLOW), key_tile,
                       (i - 2, jnp.int32(0), c_max))
        o_ref[pl.ds(q0, tile), :] = jnp.where(lo_half, acc_ref[:tile], acc_ref[tile:]).astype(BF16)
        return 0

    lax.fori_loop(0, S // tile, q_block, 0)


def _sb_attention(proj3):
    B, S, _ = proj3.shape
    pairs = SB_WIDTH // LANES
    tile = min(SB_TILE, S)
    suf = jnp.asarray(np.arange(tile)[:, None] > np.arange(tile)[None, :], BF16)
    spec = lambda off: pl.BlockSpec((None, S, LANES), lambda b, p: (b, 0, off + p))
    return pl.pallas_call(
        functools.partial(_sb_kernel, tile=tile),
        grid=(B, pairs),
        in_specs=[spec(0), spec(pairs), spec(2 * pairs),
                  pl.BlockSpec((tile, tile), lambda b, p: (0, 0))],
        out_specs=pl.BlockSpec((None, S, LANES), lambda b, p: (b, 0, p)),
        out_shape=jax.ShapeDtypeStruct((B, S, SB_WIDTH), BF16),
        scratch_shapes=[pltpu.VMEM((S // tile, LANES, tile), BF16),
                        pltpu.VMEM((2, 2 * tile, tile), F32),
                        pltpu.VMEM((2 * tile, LANES), F32), pltpu.VMEM((2 * tile, 1), F32)],
        compiler_params=_params("parallel", "parallel"),
        name="sb_attention",
    )(proj3, proj3, proj3, suf)


def _swa_kernel(sink_ref, q_ref, kp_ref, kc_ref, vp_ref, vc_ref, sel_ref, bias_ref, o_ref):
    qn = q_ref[...]
    kn_t = jnp.concatenate([kp_ref[...], kc_ref[...]], axis=0).T
    v = jnp.concatenate([vp_ref[...], vc_ref[...]], axis=0)
    lane = lax.broadcasted_iota(jnp.int32, (1, LANES), 1)
    lo_half = lane < HEAD_DIM
    zero = jnp.zeros((WINDOW, LANES), BF16)
    W = WINDOW
    head_row = lax.broadcasted_iota(jnp.int32, (SWA_GROUP * W, 1), 0) // W
    groups = range(SWA_KV_HEADS)
    kt_g = [kn_t[g * HEAD_DIM:(g + 1) * HEAD_DIM, :] for g in groups]
    kt_dup = [jnp.concatenate([kt_g[g], kt_g[g]], axis=0) for g in groups]
    v_dup = [_dot(v, sel_ref[g]).astype(BF16) for g in groups]
    lhs = []
    for g in groups:
        rows = []
        for t in range(g * SWA_GROUP // 2, (g + 1) * SWA_GROUP // 2):
            qt = qn[:, t * LANES:(t + 1) * LANES]
            rows += [jnp.where(lo_half, qt, zero), jnp.where(lo_half, zero, qt)]
        lhs.append(jnp.concatenate(rows, axis=0))
    s = [_dot(lhs[g], kt_dup[g]) for g in groups]
    s = [s[g] + bias_ref[g * SWA_GROUP:(g + 1) * SWA_GROUP].reshape(SWA_GROUP * W, 2 * W) for g in groups]
    sink = []
    for g in groups:
        sk = jnp.zeros((SWA_GROUP * W, 1), F32)
        for hh in range(SWA_GROUP):
            sk = jnp.where(head_row == hh, sink_ref[g * SWA_GROUP + hh], sk)
        sink.append(sk)
    m = [jnp.maximum(jnp.max(s[g], axis=-1, keepdims=True), sink[g]) for g in groups]
    p = [jnp.exp(s[g] - m[g]) for g in groups]
    den = [jnp.sum(p[g], axis=-1, keepdims=True) + jnp.exp(sink[g] - m[g]) for g in groups]
    r = [_dot(p[g].astype(BF16), v_dup[g]) / den[g] for g in groups]
    for g in groups:
        for tt in range(SWA_GROUP // 2):
            t = g * SWA_GROUP // 2 + tt
            o_ref[:, t * LANES:(t + 1) * LANES] = jnp.where(
                lo_half, r[g][2 * tt * W:(2 * tt + 1) * W], r[g][(2 * tt + 1) * W:(2 * tt + 2) * W]).astype(BF16)


def _swa_attention(proj3, sinks, bias):
    B, S, _ = proj3.shape
    W = WINDOW
    q_blk = (3 * SB_WIDTH) // SWA_WIDTH
    k_blk = (3 * SB_WIDTH + SWA_WIDTH) // LANES
    v_blk = k_blk + 1
    sel = jnp.asarray(np.stack([np.arange(LANES)[:, None] == g * HEAD_DIM + np.arange(LANES)[None, :] % HEAD_DIM
                                for g in range(SWA_KV_HEADS)]), BF16)
    prev = lambda c: pl.BlockSpec((None, W, LANES), lambda b, n: (b, jnp.maximum(n - 1, 0), c))
    cur = lambda c: pl.BlockSpec((None, W, LANES), lambda b, n: (b, n, c))
    const = lambda *shape: pl.BlockSpec(shape, lambda b, n: (0,) * len(shape))
    return pl.pallas_call(
        _swa_kernel,
        grid=(B, S // W),
        in_specs=[
            pl.BlockSpec(memory_space=pltpu.SMEM),
            pl.BlockSpec((None, W, SWA_WIDTH), lambda b, n: (b, n, q_blk)),
            prev(k_blk), cur(k_blk), prev(v_blk), cur(v_blk),
            const(SWA_KV_HEADS, LANES, LANES),
            pl.BlockSpec((None, SWA_Q_HEADS, W, 2 * W), lambda b, n: (jnp.minimum(n, 1), 0, 0, 0)),
        ],
        out_specs=pl.BlockSpec((None, W, SWA_WIDTH), lambda b, n: (b, n, 0)),
        out_shape=jax.ShapeDtypeStruct((B, S, SWA_WIDTH), BF16),
        compiler_params=_params("arbitrary", "arbitrary"),
        name="swa_attention",
    )(sinks, proj3, proj3, proj3, proj3, proj3, sel, bias)


def _outproj(sb_ref, sw_ref, gsb_ref, gsw_ref, w_ref, x):
    a = _rms(sb_ref[...].astype(F32), gsb_ref[...]).astype(BF16)
    b = _rms(sw_ref[...].astype(F32), gsw_ref[...]).astype(BF16)
    return x + _dot(a, w_ref[:SB_WIDTH, :]) + _dot(b, w_ref[SB_WIDTH:, :])


def _memkv_kernel(m_ref, g_ref, w_ref, kg_ref, k_ref, v_ref):
    h = _rms(m_ref[...], g_ref[...]).astype(BF16)
    kv = _dot(h, w_ref[...])
    for hd in range(MEM_HEADS):
        sl = slice(hd * MEM_HEAD_DIM, (hd + 1) * MEM_HEAD_DIM)
        k_ref[:, sl] = _rms(kv[:, sl], kg_ref[...]).astype(BF16)
    v_ref[...] = kv[:, MEM_WIDTH:].astype(BF16)


def _memkv(mem, norm_mem, wkv, k_gain):
    B, N, D = mem.shape
    L = wkv.shape[0]
    out = jax.ShapeDtypeStruct((L, B, N, MEM_WIDTH), BF16)
    return pl.pallas_call(
        _memkv_kernel,
        grid=(L, B),
        in_specs=[
            pl.BlockSpec((None, N, D), lambda l, b: (b, 0, 0)),
            pl.BlockSpec((None, 1, D), lambda l, b: (l, 0, 0)),
            pl.BlockSpec((None, D, 2 * MEM_WIDTH), lambda l, b: (l, 0, 0)),
            pl.BlockSpec((None, 1, MEM_HEAD_DIM), lambda l, b: (l, 0, 0)),
        ],
        out_specs=[pl.BlockSpec((None, None, N, MEM_WIDTH), lambda l, b: (l, b, 0, 0))] * 2,
        out_shape=[out, out],
        compiler_params=_params("parallel", "parallel"),
        name="memkv",
    )(mem, norm_mem, wkv, k_gain)


def _xattn_kernel(sb_ref, sw_ref, gsb_ref, gsw_ref, wout_ref, x_ref, gx_ref, wq_ref, qg_ref, k_ref, v_ref, wo_ref,
                  o_ref):
    x = _outproj(sb_ref, sw_ref, gsb_ref, gsw_ref, wout_ref, x_ref[...])
    h = _rms(x, gx_ref[...]).astype(BF16)
    q = _dot(h, wq_ref[...])
    outs = []
    for hd in range(MEM_HEADS):
        sl = slice(hd * MEM_HEAD_DIM, (hd + 1) * MEM_HEAD_DIM)
        qn = _rms(q[:, sl], qg_ref[...]).astype(BF16)
        s = _dot_nt(qn, k_ref[:, sl]) * (MEM_HEAD_DIM ** -0.5)
        p = jnp.exp(s - jnp.max(s, axis=-1, keepdims=True))
        den = jnp.sum(p, axis=-1, keepdims=True)
        outs.append((_dot(p.astype(BF16), v_ref[:, sl]) / den).astype(BF16))
    o_ref[...] = x + _dot(jnp.concatenate(outs, axis=1), wo_ref[...])


def _xattn(sb_o, sw_o, g_sb, g_sw, w_out, x3, norm_x, wq, q_gain, kmem, vmem, wo, l):
    B, S, D = x3.shape
    N = kmem.shape[2]
    tm = min(TOKEN_TILE, S)
    return pl.pallas_call(
        _xattn_kernel,
        grid=(B, S // tm),
        in_specs=[
            pl.BlockSpec((None, tm, SB_WIDTH), lambda b, i: (b, i, 0)),
            pl.BlockSpec((None, tm, SWA_WIDTH), lambda b, i: (b, i, 0)),
            pl.BlockSpec((None, 1, SB_WIDTH), lambda b, i: (l, 0, 0)),
            pl.BlockSpec((None, 1, SWA_WIDTH), lambda b, i: (l, 0, 0)),
            pl.BlockSpec((None, SB_WIDTH + SWA_WIDTH, D), lambda b, i: (l, 0, 0)),
            pl.BlockSpec((None, tm, D), lambda b, i: (b, i, 0)),
            pl.BlockSpec((None, 1, D), lambda b, i: (l, 0, 0)),
            pl.BlockSpec((None, D, MEM_WIDTH), lambda b, i: (l, 0, 0)),
            pl.BlockSpec((None, 1, MEM_HEAD_DIM), lambda b, i: (l, 0, 0)),
            pl.BlockSpec((None, None, N, MEM_WIDTH), lambda b, i: (l, b, 0, 0)),
            pl.BlockSpec((None, None, N, MEM_WIDTH), lambda b, i: (l, b, 0, 0)),
            pl.BlockSpec((None, MEM_WIDTH, D), lambda b, i: (l, 0, 0)),
        ],
        out_specs=pl.BlockSpec((None, tm, D), lambda b, i: (b, i, 0)),
        out_shape=jax.ShapeDtypeStruct((B, S, D), F32),
        compiler_params=_params("arbitrary", "arbitrary"),
        name="xattn",
    )(sb_o, sw_o, g_sb, g_sw, w_out, x3, norm_x, wq, q_gain, kmem, vmem, wo)


def _swiglu_step(h, wg_ref, wu_ref, wd_ref):
    a = _dot(h, wg_ref[...])
    u = _dot(h, wu_ref[...])
    act = a / (1.0 + jnp.exp(-a)) * u
    return _dot(act.astype(BF16), wd_ref[...])


def _ffn_kernel(x_ref, g_ref, wg_ref, wu_ref, wd_ref, o_ref, h_ref):
    @pl.when(pl.program_id(1) == 0)
    def _():
        x = x_ref[...]
        h_ref[...] = _rms(x, g_ref[...]).astype(BF16)
        o_ref[...] = x

    o_ref[...] += _swiglu_step(h_ref[...], wg_ref, wu_ref, wd_ref)


def _ffn_dense(x2, g, wg, wu, wd, l, i_dense):
    T, D = x2.shape
    F = wg.shape[-1]
    tm, tf = min(FFN_TOKEN_TILE, T), FFN_COL_TILE
    return pl.pallas_call(
        _ffn_kernel,
        grid=(T // tm, F // tf),
        in_specs=[
            pl.BlockSpec((tm, D), lambda i, f: (i, 0)),
            pl.BlockSpec((None, 1, D), lambda i, f: (l, 0, 0)),
            pl.BlockSpec((None, D, tf), lambda i, f: (i_dense, 0, f)),
            pl.BlockSpec((None, D, tf), lambda i, f: (i_dense, 0, f)),
            pl.BlockSpec((None, tf, D), lambda i, f: (i_dense, f, 0)),
        ],
        out_specs=pl.BlockSpec((tm, D), lambda i, f: (i, 0)),
        out_shape=jax.ShapeDtypeStruct((T, D), F32),
        scratch_shapes=[pltpu.VMEM((tm, D), BF16)],
        compiler_params=_params("parallel", "arbitrary"),
        name="ffn_dense",
    )(x2, g, wg, wu, wd)


def _moe_ffn_kernel(be_ref, nu_ref, xs_ref, wg_ref, wu_ref, wd_ref, o_ref, h_ref):
    del be_ref
    used = pl.program_id(0) < nu_ref[0]

    @pl.when(pl.program_id(1) == 0)
    def _():
        o_ref[...] = jnp.zeros_like(o_ref)

    @pl.when(used & (pl.program_id(1) == 0))
    def _():
        h_ref[...] = xs_ref[...].astype(BF16)

    @pl.when(used)
    def _():
        o_ref[...] += _swiglu_step(h_ref[...], wg_ref, wu_ref, wd_ref)


def _moe_ffn(blk_e, n_used, xs, wg, wu, wd, i_moe):
    P, D = xs.shape
    F = wg.shape[-1]
    tm, tf = MOE_ROW_TILE, FFN_COL_TILE
    nf = F // tf
    row_blk = lambda i, nu: jnp.minimum(i, nu[0] - 1)
    col_blk = lambda i, f, nu: jnp.where(i < nu[0], f, nf - 1)
    return pl.pallas_call(
        _moe_ffn_kernel,
        grid_spec=pltpu.PrefetchScalarGridSpec(
            num_scalar_prefetch=2,
            grid=(P // tm, nf),
            in_specs=[
                pl.BlockSpec((tm, D), lambda i, f, be, nu: (row_blk(i, nu), 0)),
                pl.BlockSpec((None, None, D, tf), lambda i, f, be, nu: (i_moe, be[i], 0, col_blk(i, f, nu))),
                pl.BlockSpec((None, None, D, tf), lambda i, f, be, nu: (i_moe, be[i], 0, col_blk(i, f, nu))),
                pl.BlockSpec((None, None, tf, D), lambda i, f, be, nu: (i_moe, be[i], col_blk(i, f, nu), 0)),
            ],
            out_specs=pl.BlockSpec((tm, D), lambda i, f, be, nu: (i, 0)),
            scratch_shapes=[pltpu.VMEM((tm, D), BF16)],
        ),
        out_shape=jax.ShapeDtypeStruct((P, D), F32),
        compiler_params=_params("arbitrary", "arbitrary"),
        name="moe_ffn",
    )(blk_e, n_used, xs, wg, wu, wd)


def _router_kernel(x_ref, g_ref, rw_ref, rb_ref, hf_ref, ei_ref, gt_ref, cnt_ref, carry_ref):
    tm = x_ref.shape[0]

    @pl.when(pl.program_id(0) == 0)
    def _():
        carry_ref[...] = jnp.zeros_like(carry_ref)

    hf = _rms(x_ref[...], g_ref[...])
    hf_ref[...] = hf
    lane = lax.broadcasted_iota(jnp.int32, (tm, LANES), 1)
    logits = jnp.dot(hf, rw_ref[...], precision=lax.Precision.HIGHEST, preferred_element_type=F32) + rb_ref[...]
    logits = jnp.where(lane < N_EXPERTS, logits, -jnp.inf)
    m1 = jnp.max(logits, axis=-1, keepdims=True)
    i1 = jnp.min(jnp.where(logits == m1, lane, LANES), axis=-1, keepdims=True)
    rest = jnp.where(lane == i1, -jnp.inf, logits)
    m2 = jnp.max(rest, axis=-1, keepdims=True)
    i2 = jnp.min(jnp.where(rest == m2, lane, LANES), axis=-1, keepdims=True)
    e = jnp.exp(m2 - m1)
    g1 = 1.0 / (1.0 + e)
    g2 = e / (1.0 + e)
    oh1 = lane == i1
    oh2 = lane == i2
    oh = (oh1 | oh2).astype(F32)
    row = lax.broadcasted_iota(jnp.int32, (tm, tm), 0)
    col = lax.broadcasted_iota(jnp.int32, (tm, tm), 1)
    earlier = (row > col).astype(BF16)
    before = _dot(earlier, oh.astype(BF16)) + carry_ref[...]
    r1 = jnp.sum(jnp.where(oh1, before, 0.0), axis=-1, keepdims=True).astype(jnp.int32)
    r2 = jnp.sum(jnp.where(oh2, before, 0.0), axis=-1, keepdims=True).astype(jnp.int32)
    carry_ref[...] += jnp.sum(oh, axis=0, keepdims=True)
    cnt_ref[...] = carry_ref[...]
    ei = jnp.where(lane == 0, i1, jnp.where(lane == 1, i2, jnp.where(lane == 2, r1, jnp.where(lane == 3, r2, 0))))
    ei_ref[...] = ei[:, :8]
    gt = jnp.where(lane == 0, g1, jnp.where(lane == 1, g2, 0.0))
    gt_ref[...] = gt[:, :8]


def _router(x2, g, rw, rb, l, i_moe):
    T, D = x2.shape
    tm = TOKEN_TILE
    return pl.pallas_call(
        _router_kernel,
        grid=(T // tm,),
        in_specs=[
            pl.BlockSpec((tm, D), lambda i: (i, 0)),
            pl.BlockSpec((None, 1, D), lambda i: (l, 0, 0)),
            pl.BlockSpec((None, D, LANES), lambda i: (i_moe, 0, 0)),
            pl.BlockSpec((None, 1, LANES), lambda i: (i_moe, 0, 0)),
        ],
        out_specs=[
            pl.BlockSpec((tm, D), lambda i: (i, 0)),
            pl.BlockSpec((tm, 8), lambda i: (i, 0)),
            pl.BlockSpec((tm, 8), lambda i: (i, 0)),
            pl.BlockSpec((1, LANES), lambda i: (0, 0)),
        ],
        out_shape=[
            jax.ShapeDtypeStruct((T, D), F32),
            jax.ShapeDtypeStruct((T, 8), jnp.int32),
            jax.ShapeDtypeStruct((T, 8), F32),
            jax.ShapeDtypeStruct((1, LANES), F32),
        ],
        scratch_shapes=[pltpu.VMEM((1, LANES), F32)],
        compiler_params=_params("arbitrary"),
        name="router",
    )(x2, g, rw, rb)


def _dispatch_kernel(last_blk_ref, dest_ref, hf_ref, xs_ref, zero_ref, sem):
    tm = hf_ref.shape[0]

    @pl.when(pl.program_id(0) == 0)
    def _():
        zero_ref[...] = jnp.zeros_like(zero_ref)
        blk_rows = zero_ref.shape[0]

        def fill(e):
            start = pl.multiple_of(last_blk_ref[e] * blk_rows, blk_rows)
            return pltpu.make_async_copy(zero_ref, xs_ref.at[pl.ds(start, blk_rows)], sem)

        for e in range(N_EXPERTS):
            fill(e).start()
        for e in range(N_EXPERTS):
            fill(e).wait()

    def row_copy(r, d):
        return pltpu.make_async_copy(hf_ref.at[pl.ds(r, 1)], xs_ref.at[pl.ds(d, 1)], sem)

    def issue(r, _):
        row_copy(r, dest_ref[2 * r]).start()
        row_copy(r, dest_ref[2 * r + 1]).start()
        return 0

    lax.fori_loop(0, tm, issue, 0, unroll=DMA_ISSUE_UNROLL)
    for _ in range(2):
        pltpu.make_async_copy(hf_ref, xs_ref.at[pl.ds(0, tm)], sem).wait()


def _dispatch(last_blk, dest, hf, n_rows):
    T, D = hf.shape
    tm = TOKEN_TILE
    return pl.pallas_call(
        _dispatch_kernel,
        grid_spec=pltpu.PrefetchScalarGridSpec(
            num_scalar_prefetch=1,
            grid=(T // tm,),
            in_specs=[
                pl.BlockSpec((2 * tm,), lambda i, lb: (i,), memory_space=pltpu.SMEM),
                pl.BlockSpec((tm, D), lambda i, lb: (i, 0)),
            ],
            out_specs=pl.BlockSpec(memory_space=pl.ANY),
            scratch_shapes=[pltpu.VMEM((MOE_ROW_TILE, D), F32), pltpu.SemaphoreType.DMA(())],
        ),
        out_shape=jax.ShapeDtypeStruct((n_rows, D), F32),
        compiler_params=_params("arbitrary"),
        name="moe_dispatch",
    )(last_blk, dest, hf)


def _combine_kernel(dest_ref, x_ref, gt_ref, y_ref, o_ref, buf_ref, sem):
    tm = x_ref.shape[0]

    def row_copy(slot, r, d):
        return pltpu.make_async_copy(y_ref.at[pl.ds(d, 1)], buf_ref.at[slot, pl.ds(r, 1)], sem)

    def issue(r, _):
        row_copy(0, r, dest_ref[2 * r]).start()
        row_copy(1, r, dest_ref[2 * r + 1]).start()
        return 0

    lax.fori_loop(0, tm, issue, 0, unroll=DMA_ISSUE_UNROLL)
    for slot in range(2):
        pltpu.make_async_copy(y_ref.at[pl.ds(0, tm)], buf_ref.at[slot], sem).wait()
    gt = gt_ref[...]
    o_ref[...] = x_ref[...] + gt[:, 0:1] * buf_ref[0] + gt[:, 1:2] * buf_ref[1]


def _combine(dest, x2, gates, y):
    T, D = x2.shape
    tm = TOKEN_TILE
    return pl.pallas_call(
        _combine_kernel,
        grid=(T // tm,),
        in_specs=[
            pl.BlockSpec((2 * tm,), lambda i: (i,), memory_space=pltpu.SMEM),
            pl.BlockSpec((tm, D), lambda i: (i, 0)),
            pl.BlockSpec((tm, 8), lambda i: (i, 0)),
            pl.BlockSpec(memory_space=pl.ANY),
        ],
        out_specs=pl.BlockSpec((tm, D), lambda i: (i, 0)),
        out_shape=jax.ShapeDtypeStruct((T, D), F32),
        scratch_shapes=[pltpu.VMEM((2, tm, D), F32), pltpu.SemaphoreType.DMA(())],
        compiler_params=_params("arbitrary"),
        name="moe_combine",
    )(dest, x2, gates, y)


def _moe(x2, g, rw, rb, wg, wu, wd, l, i_moe):
    T, D = x2.shape
    tmm = MOE_ROW_TILE
    P = 2 * T + N_EXPERTS * tmm
    hf, ei, gates, cnt = _router(x2, g, rw, rb, l, i_moe)
    counts = cnt[0, :N_EXPERTS].astype(jnp.int32)
    pcounts = (counts + tmm - 1) // tmm * tmm
    pends = jnp.cumsum(pcounts)
    pstarts = pends - pcounts
    experts = ei[:, 0:2]
    start_of = sum(jnp.where(experts == e, pstarts[e], 0) for e in range(N_EXPERTS))
    dest = (start_of + ei[:, 2:4]).reshape(-1)
    blk_start = jnp.arange(P // tmm, dtype=jnp.int32) * tmm
    blk_e = jnp.minimum(jnp.sum(blk_start[:, None] >= pends[None, :], axis=1), N_EXPERTS - 1).astype(jnp.int32)
    n_used = (pends[N_EXPERTS - 1:] // tmm).astype(jnp.int32)
    last_blk = jnp.maximum(pends // tmm - 1, 0).astype(jnp.int32)
    xs = _dispatch(last_blk, dest, hf, P)
    y = _moe_ffn(blk_e, n_used, xs, wg, wu, wd, i_moe)
    return _combine(dest, x2, gates, y)


def _t5_buckets(dist):
    n = np.maximum(dist, 0)
    max_exact = N_BUCKETS // 2
    large = max_exact + (np.log(np.maximum(n, 1) / max_exact) / np.log(MAX_DISTANCE / max_exact)
                         * (N_BUCKETS - max_exact)).astype(np.int32)
    large = np.minimum(large, N_BUCKETS - 1)
    return np.where(n < max_exact, n, large).astype(np.int32)


def kernel(x, mem, norm_mix, w_in, sb_out_gain, swa_q_gain, swa_k_gain, swa_sinks, swa_out_gain, rel_bias, w_out, norm_xattn, norm_mem, xattn_wq, xattn_wkv, xattn_q_gain, xattn_k_gain, xattn_wo, norm_ffn, dense_w_gate, dense_w_up, dense_w_down, router_w, router_b, exp_w_gate, exp_w_up, exp_w_down):
    B, S, D = x.shape
    depth = w_in.shape[0]
    T = B * S
    row3 = lambda a: a.reshape(a.shape[0], 1, a.shape[1])
    bf = lambda a: a.astype(BF16)

    dist = WINDOW + np.arange(WINDOW)[:, None] - np.arange(2 * WINDOW)[None, :]
    swa_bias = jnp.transpose(rel_bias[_t5_buckets(dist)], (2, 0, 1))
    band = (dist >= 0) & (dist < WINDOW)
    band = np.stack([band & (np.arange(2 * WINDOW)[None, :] >= WINDOW), band])
    swa_bias = jnp.where(band[:, None], swa_bias[None], -jnp.inf)
    router_w_p = jnp.pad(router_w, ((0, 0), (0, 0), (0, LANES - N_EXPERTS)))
    router_b_p = row3(jnp.pad(router_b, ((0, 0), (0, LANES - N_EXPERTS))))

    w_in_b, w_out_b = bf(w_in), bf(w_out)
    wq_b, wkv_b, wo_b = bf(xattn_wq), bf(xattn_wkv), bf(xattn_wo)
    dg_b, du_b, dd_b = bf(dense_w_gate), bf(dense_w_up), bf(dense_w_down)
    eg_b, eu_b, ed_b = bf(exp_w_gate), bf(exp_w_up), bf(exp_w_down)
    norm_mix3, norm_x3, norm_f3 = row3(norm_mix), row3(norm_xattn), row3(norm_ffn)
    sb_g3, sw_g3 = row3(sb_out_gain), row3(swa_out_gain)
    xq_g3, xk_g3 = row3(xattn_q_gain), row3(xattn_k_gain)

    kmem, vmem = _memkv(mem, row3(norm_mem), wkv_b, xk_g3)

    x2 = x.reshape(T, D)
    for l in range(depth):
        head_gain = jnp.concatenate([jnp.tile(swa_q_gain[l], SWA_Q_HEADS) * (HEAD_DIM ** -0.5),
                                     jnp.tile(swa_k_gain[l], SWA_KV_HEADS)])[None, :]
        proj = _inproj(x2, norm_mix3, w_in_b, head_gain, l).reshape(B, S, -1)
        sb_o = _sb_attention(proj)
        sw_o = _swa_attention(proj, swa_sinks[l], swa_bias)
        x2 = _xattn(sb_o, sw_o, sb_g3, sw_g3, w_out_b, x2.reshape(B, S, D), norm_x3, wq_b, xq_g3, kmem, vmem,
                    wo_b, l).reshape(T, D)
        if l % 2 == 0:
            x2 = _ffn_dense(x2, norm_f3, dg_b, du_b, dd_b, l, l // 2)
        else:
            x2 = _moe(x2, norm_f3, router_w_p, router_b_p, eg_b, eu_b, ed_b, l, l // 2)
    return x2.reshape(B, S, D)
```

```python
import functools

import numpy as np
import jax
import jax.numpy as jnp
from jax import lax
from jax.experimental import pallas as pl
from jax.experimental.pallas import tpu as pltpu

F32 = jnp.float32
BF16 = jnp.bfloat16

HEAD_DIM = 64
SB_WIDTH = 512
SWA_WIDTH = 512
SWA_Q_HEADS = 8
SWA_GROUP = 4
SWA_KV_HEADS = 2
WINDOW = 128
N_BUCKETS = 32
MAX_DISTANCE = 128
MEM_HEADS = 4
MEM_HEAD_DIM = 128
MEM_WIDTH = 512
N_EXPERTS = 8
EPS = 1e-6
LANES = 128

VMEM_LIMIT = 56 * 1024 * 1024

TOKEN_TILE = 512
FFN_TOKEN_TILE = 1024
FFN_COL_TILE = 512
SB_TILE = 256
MOE_ROW_TILE = 1024
DMA_ISSUE_UNROLL = 8
SB_LOG_UNDERFLOW = -104.0


def _params(*sem):
    return pltpu.CompilerParams(dimension_semantics=("arbitrary",) * len(sem), vmem_limit_bytes=VMEM_LIMIT)


def _rms(x, g):
    return x * lax.rsqrt(jnp.mean(x * x, axis=-1, keepdims=True) + EPS) * g


def _dot(a, b):
    return jnp.dot(a, b, preferred_element_type=F32)


def _dot_nt(a, b):
    return lax.dot_general(a, b, (((1,), (1,)), ((), ())), preferred_element_type=F32)


def _group_mean_sq(x, ones_ref):
    sq = x * x
    hi = sq.astype(BF16)
    lo = (sq - hi.astype(F32)).astype(BF16)
    return _dot(hi, ones_ref[...]) + _dot(lo, ones_ref[...])


def _inproj_kernel(x_ref, g_ref, w_ref, hg_ref, ones_q_ref, ones_k_ref, o_ref):
    h = _rms(x_ref[...], g_ref[...]).astype(BF16)
    sb_cols = 3 * SB_WIDTH
    for c in range(3):
        sl = slice(c * SB_WIDTH, (c + 1) * SB_WIDTH)
        o_ref[:, sl] = _dot(h, w_ref[:, sl]).astype(BF16)
    y = _dot(h, w_ref[:, sb_cols:])
    half = SWA_WIDTH // 2
    qk = SWA_WIDTH + SWA_KV_HEADS * HEAD_DIM
    ms = jnp.concatenate([_group_mean_sq(y[:, :half], ones_q_ref),
                          _group_mean_sq(y[:, half:SWA_WIDTH], ones_q_ref),
                          _group_mean_sq(y[:, SWA_WIDTH:qk], ones_k_ref)], axis=1)
    o_ref[:, sb_cols:sb_cols + qk] = (y[:, :qk] * lax.rsqrt(ms + EPS) * hg_ref[...]).astype(BF16)
    o_ref[:, sb_cols + qk:] = y[:, qk:].astype(BF16)


def _head_group_ones(n):
    return jnp.asarray((np.arange(n)[:, None] // HEAD_DIM == np.arange(n)[None, :] // HEAD_DIM) / HEAD_DIM, BF16)


def _inproj(x2, g, w, head_gain, l):
    T, D = x2.shape
    N = w.shape[-1]
    tm = TOKEN_TILE
    half, kv = SWA_WIDTH // 2, SWA_KV_HEADS * HEAD_DIM
    const = lambda *shape: pl.BlockSpec(shape, lambda i: (0,) * len(shape))
    return pl.pallas_call(
        _inproj_kernel,
        grid=(T // tm,),
        in_specs=[
            pl.BlockSpec((tm, D), lambda i: (i, 0)),
            pl.BlockSpec((None, 1, D), lambda i: (l, 0, 0)),
            pl.BlockSpec((None, D, N), lambda i: (l, 0, 0)),
            const(1, SWA_WIDTH + kv), const(half, half), const(kv, kv),
        ],
        out_specs=pl.BlockSpec((tm, N), lambda i: (i, 0)),
        out_shape=jax.ShapeDtypeStruct((T, N), BF16),
        compiler_params=_params("arbitrary"),
        name="inproj",
    )(x2, g, w, head_gain, _head_group_ones(half), _head_group_ones(kv))


def _sb_kernel(q_ref, k_ref, v_ref, suf_ref, o_ref, kt_ref, z_ref, zn_ref, acc_ref, c_ref, *, tile):
    S = q_ref.shape[0]
    lane = lax.broadcasted_iota(jnp.int32, (1, LANES), 1)
    lo_half = lane < HEAD_DIM
    row = lax.broadcasted_iota(jnp.int32, (tile, tile), 0)
    col = lax.broadcasted_iota(jnp.int32, (tile, tile), 1)
    strict = col < row
    suffix = suf_ref[...]

    for j in range(S // tile):
        kt_ref[j] = k_ref[j * tile:(j + 1) * tile, :].T

    heads = range(2)
    sl = [slice(h * tile, (h + 1) * tile) for h in heads]

    def log_terms(z, masked):
        n = range(len(z))
        log_beta = [jnp.minimum(z[u], 0.0) - jnp.log(1.0 + jnp.exp(-jnp.abs(z[u]))) for u in n]
        log_om = [log_beta[u] - z[u] for u in n]
        log_om = [jnp.where(strict, log_om[u], 0.0) if masked[u] else log_om[u] for u in n]
        hi = [log_om[u].astype(BF16) for u in n]
        lo = [(log_om[u] - hi[u].astype(F32)).astype(BF16) for u in n]
        tail = [_dot(jnp.concatenate([hi[u], lo[u]], axis=0), suffix) for u in n]
        tail = [tail[u][:tile] + tail[u][tile:] for u in n]
        row_sum = [jnp.sum(log_om[u], axis=-1, keepdims=True) for u in n]
        return log_beta, tail, row_sum

    n_q = S // tile

    def key_rows(j):
        return pl.ds(pl.multiple_of(j * tile, tile), tile)

    def head_queries(i):
        q2 = q_ref[key_rows(i), :] * jnp.asarray(HEAD_DIM ** -0.5, BF16)
        zero = jnp.zeros_like(q2)
        return [jnp.where(lo_half, q2, zero), jnp.where(lo_half, zero, q2)]

    def next_tile_scores(i, into):
        if n_q > 1:
            nxt = jnp.minimum(i + 1, n_q - 1)
            qn = head_queries(nxt)
            kt_d, kt_o = kt_ref[nxt], kt_ref[nxt - 1]
            for h in heads:
                zn_ref[into, h] = _dot(qn[h], kt_d)
                zn_ref[into, 2 + h] = _dot(qn[h], kt_o)

    def q_block(i, parity):
        q0 = pl.multiple_of(i * tile, tile)
        qs = head_queries(i)

        def prefetch_scores(j, slot):
            kt = kt_ref[jnp.maximum(j, 0)]
            for h in heads:
                z_ref[slot, sl[h]] = _dot(qs[h], kt)

        def first_block():
            kt = kt_ref[0]
            z = [_dot(qs[h], kt) for h in heads]
            next_tile_scores(i, 1 - parity)
            log_beta, tail, row_sum = log_terms(z, [True, True])
            a = [jnp.where(strict, jnp.exp(log_beta[h] + tail[h]), 0.0) for h in heads]
            v_d = v_ref[0:tile, :]
            for h in heads:
                acc_ref[sl[h]] = _dot(a[h].astype(BF16), v_d)
                c_ref[sl[h]] = row_sum[h]
            return jnp.float32(0.0)

        def later_block():
            z = [zn_ref[parity, u] for u in range(4)]
            next_tile_scores(i, 1 - parity)
            prefetch_scores(i - 2, 0)
            log_beta, tail, row_sum = log_terms(z, [True, True, False, False])
            a_d = [jnp.where(strict, jnp.exp(log_beta[h] + tail[h]), 0.0) for h in heads]
            a_o = [jnp.exp(log_beta[2 + h] + tail[2 + h] + row_sum[h]) for h in heads]
            v_d, v_o = v_ref[key_rows(i), :], v_ref[key_rows(i - 1), :]
            c = [row_sum[h] + row_sum[2 + h] for h in heads]
            for h in heads:
                acc_ref[sl[h]] = _dot(a_d[h].astype(BF16), v_d) + _dot(a_o[h].astype(BF16), v_o)
                c_ref[sl[h]] = c[h]
            return jnp.maximum(jnp.max(c[0]), jnp.max(c[1]))

        c_max = lax.cond(i == 0, first_block, later_block) if parity == 0 else later_block()

        def key_tile(state):
            j, slot, _ = state
            z = [z_ref[slot, sl[h]] for h in heads]
            prefetch_scores(j - 1, 1 - slot)
            log_beta, tail, row_sum = log_terms(z, [False, False])
            c = [c_ref[sl[h]] for h in heads]
            a = [jnp.exp(log_beta[h] + tail[h] + c[h]) for h in heads]
            c = [c[h] + row_sum[h] for h in heads]
            v_j = v_ref[key_rows(j), :]
            for h in heads:
                acc_ref[sl[h]] += _dot(a[h].astype(BF16), v_j)
                c_ref[sl[h]] = c[h]
            return j - 1, 1 - slot, jnp.maximum(jnp.max(c[0]), jnp.max(c[1]))

        lax.while_loop(lambda s: (s[0] >= 0) & (s[2] > SB_LOG_UNDERFLOW), key_tile,
                       (i - 2, jnp.int32(0), c_max))
        o_ref[pl.ds(q0, tile), :] = jnp.where(lo_half, acc_ref[:tile], acc_ref[tile:]).astype(BF16)

    if n_q == 1:
        q_block(0, 0)
    else:
        def q_block_pair(p, _):
            q_block(2 * p, 0)
            q_block(2 * p + 1, 1)
            return 0

        lax.fori_loop(0, n_q // 2, q_block_pair, 0)


def _sb_attention(proj3):
    B, S, _ = proj3.shape
    pairs = SB_WIDTH // LANES
    tile = min(SB_TILE, S)
    assert S % tile == 0 and (S // tile == 1 or (S // tile) % 2 == 0)
    suf = jnp.asarray(np.arange(tile)[:, None] > np.arange(tile)[None, :], BF16)
    spec = lambda off: pl.BlockSpec((None, S, LANES), lambda b, p: (b, 0, off + p))
    return pl.pallas_call(
        functools.partial(_sb_kernel, tile=tile),
        grid=(B, pairs),
        in_specs=[spec(0), spec(pairs), spec(2 * pairs),
                  pl.BlockSpec((tile, tile), lambda b, p: (0, 0))],
        out_specs=pl.BlockSpec((None, S, LANES), lambda b, p: (b, 0, p)),
        out_shape=jax.ShapeDtypeStruct((B, S, SB_WIDTH), BF16),
        scratch_shapes=[pltpu.VMEM((S // tile, LANES, tile), BF16),
                        pltpu.VMEM((2, 2 * tile, tile), F32), pltpu.VMEM((2, 4, tile, tile), F32),
                        pltpu.VMEM((2 * tile, LANES), F32), pltpu.VMEM((2 * tile, 1), F32)],
        compiler_params=_params("parallel", "parallel"),
        name="sb_attention",
    )(proj3, proj3, proj3, suf)


def _swa_kernel(sink_ref, q_ref, kp_ref, kc_ref, vp_ref, vc_ref, sel_ref, bias_ref, o_ref):
    W = WINDOW
    qn = q_ref[...]
    kn_t = jnp.concatenate([kp_ref[...], kc_ref[...]], axis=0).T
    v = jnp.concatenate([vp_ref[...], vc_ref[...]], axis=0)
    lane = lax.broadcasted_iota(jnp.int32, (1, LANES), 1)
    lo_half = lane < HEAD_DIM
    zero = jnp.zeros((W, LANES), BF16)
    head_row = lax.broadcasted_iota(jnp.int32, (SWA_GROUP * W, 1), 0) // W
    groups = range(SWA_KV_HEADS)
    kt_g = [kn_t[g * HEAD_DIM:(g + 1) * HEAD_DIM, :] for g in groups]
    kt_dup = [jnp.concatenate([kt_g[g], kt_g[g]], axis=0) for g in groups]
    v_dup = [_dot(v, sel_ref[g]).astype(BF16) for g in groups]
    first_slab = jnp.minimum(pl.program_id(1), 1)
    chains = [(blk, g) for blk in range(2) for g in groups]
    lhs = []
    for blk, g in chains:
        rows = []
        for t in range(g * SWA_GROUP // 2, (g + 1) * SWA_GROUP // 2):
            qt = qn[blk * W:(blk + 1) * W, t * LANES:(t + 1) * LANES]
            rows += [jnp.where(lo_half, qt, zero), jnp.where(lo_half, zero, qt)]
        lhs.append(jnp.concatenate(rows, axis=0))
    s = [_dot(lhs[c], kt_dup[g][:, blk * W:(blk + 2) * W]) for c, (blk, g) in enumerate(chains)]
    bias = [bias_ref[first_slab], bias_ref[1]]
    s = [s[c] + bias[blk][g * SWA_GROUP:(g + 1) * SWA_GROUP].reshape(SWA_GROUP * W, 2 * W)
         for c, (blk, g) in enumerate(chains)]
    sink = []
    for g in groups:
        sk = jnp.zeros((SWA_GROUP * W, 1), F32)
        for hh in range(SWA_GROUP):
            sk = jnp.where(head_row == hh, sink_ref[g * SWA_GROUP + hh], sk)
        sink.append(sk)
    m = [jnp.maximum(jnp.max(s[c], axis=-1, keepdims=True), sink[g]) for c, (blk, g) in enumerate(chains)]
    p = [jnp.exp(s[c] - m[c]) for c in range(len(chains))]
    den = [jnp.sum(p[c], axis=-1, keepdims=True) + jnp.exp(sink[g] - m[c]) for c, (blk, g) in enumerate(chains)]
    r = [_dot(p[c].astype(BF16), v_dup[g][blk * W:(blk + 2) * W]) / den[c]
         for c, (blk, g) in enumerate(chains)]
    for c, (blk, g) in enumerate(chains):
        for tt in range(SWA_GROUP // 2):
            t = g * SWA_GROUP // 2 + tt
            o_ref[blk * W:(blk + 1) * W, t * LANES:(t + 1) * LANES] = jnp.where(
                lo_half, r[c][2 * tt * W:(2 * tt + 1) * W], r[c][(2 * tt + 1) * W:(2 * tt + 2) * W]).astype(BF16)


def _swa_attention(proj3, sinks, bias):
    B, S, _ = proj3.shape
    W = WINDOW
    q_blk = (3 * SB_WIDTH) // SWA_WIDTH
    k_blk = (3 * SB_WIDTH + SWA_WIDTH) // LANES
    v_blk = k_blk + 1
    sel = jnp.asarray(np.stack([np.arange(LANES)[:, None] == g * HEAD_DIM + np.arange(LANES)[None, :] % HEAD_DIM
                                for g in range(SWA_KV_HEADS)]), BF16)
    prev = lambda c: pl.BlockSpec((None, W, LANES), lambda b, n: (b, jnp.maximum(2 * n - 1, 0), c))
    cur = lambda c: pl.BlockSpec((None, 2 * W, LANES), lambda b, n: (b, n, c))
    const = lambda *shape: pl.BlockSpec(shape, lambda b, n: (0,) * len(shape))
    return pl.pallas_call(
        _swa_kernel,
        grid=(B, S // (2 * W)),
        in_specs=[
            pl.BlockSpec(memory_space=pltpu.SMEM),
            pl.BlockSpec((None, 2 * W, SWA_WIDTH), lambda b, n: (b, n, q_blk)),
            prev(k_blk), cur(k_blk), prev(v_blk), cur(v_blk),
            const(SWA_KV_HEADS, LANES, LANES),
            const(2, SWA_Q_HEADS, W, 2 * W),
        ],
        out_specs=pl.BlockSpec((None, 2 * W, SWA_WIDTH), lambda b, n: (b, n, 0)),
        out_shape=jax.ShapeDtypeStruct((B, S, SWA_WIDTH), BF16),
        compiler_params=_params("arbitrary", "arbitrary"),
        name="swa_attention",
    )(sinks, proj3, proj3, proj3, proj3, proj3, sel, bias)


def _outproj(sb_ref, sw_ref, gsb_ref, gsw_ref, w_ref, x):
    a = _rms(sb_ref[...].astype(F32), gsb_ref[...]).astype(BF16)
    b = _rms(sw_ref[...].astype(F32), gsw_ref[...]).astype(BF16)
    return x + _dot(a, w_ref[:SB_WIDTH, :]) + _dot(b, w_ref[SB_WIDTH:, :])


def _memkv_kernel(m_ref, g_ref, w_ref, kg_ref, k_ref, v_ref):
    h = _rms(m_ref[...], g_ref[...]).astype(BF16)
    kv = _dot(h, w_ref[...])
    for hd in range(MEM_HEADS):
        sl = slice(hd * MEM_HEAD_DIM, (hd + 1) * MEM_HEAD_DIM)
        k_ref[:, sl] = _rms(kv[:, sl], kg_ref[...]).astype(BF16)
    v_ref[...] = kv[:, MEM_WIDTH:].astype(BF16)


def _memkv(mem, norm_mem, wkv, k_gain):
    B, N, D = mem.shape
    L = wkv.shape[0]
    out = jax.ShapeDtypeStruct((L, B, N, MEM_WIDTH), BF16)
    return pl.pallas_call(
        _memkv_kernel,
        grid=(L, B),
        in_specs=[
            pl.BlockSpec((None, N, D), lambda l, b: (b, 0, 0)),
            pl.BlockSpec((None, 1, D), lambda l, b: (l, 0, 0)),
            pl.BlockSpec((None, D, 2 * MEM_WIDTH), lambda l, b: (l, 0, 0)),
            pl.BlockSpec((None, 1, MEM_HEAD_DIM), lambda l, b: (l, 0, 0)),
        ],
        out_specs=[pl.BlockSpec((None, None, N, MEM_WIDTH), lambda l, b: (l, b, 0, 0))] * 2,
        out_shape=[out, out],
        compiler_params=_params("parallel", "parallel"),
        name="memkv",
    )(mem, norm_mem, wkv, k_gain)


def _xattn_kernel(sb_ref, sw_ref, gsb_ref, gsw_ref, wout_ref, x_ref, gx_ref, wq_ref, qg_ref, k_ref, v_ref, wo_ref,
                  o_ref):
    x = _outproj(sb_ref, sw_ref, gsb_ref, gsw_ref, wout_ref, x_ref[...])
    h = _rms(x, gx_ref[...]).astype(BF16)
    q = _dot(h, wq_ref[...])
    outs = []
    for hd in range(MEM_HEADS):
        sl = slice(hd * MEM_HEAD_DIM, (hd + 1) * MEM_HEAD_DIM)
        qn = _rms(q[:, sl], qg_ref[...]).astype(BF16)
        s = _dot_nt(qn, k_ref[:, sl]) * (MEM_HEAD_DIM ** -0.5)
        p = jnp.exp(s - jnp.max(s, axis=-1, keepdims=True))
        den = jnp.sum(p, axis=-1, keepdims=True)
        outs.append((_dot(p.astype(BF16), v_ref[:, sl]) / den).astype(BF16))
    o_ref[...] = x + _dot(jnp.concatenate(outs, axis=1), wo_ref[...])


def _xattn(sb_o, sw_o, g_sb, g_sw, w_out, x3, norm_x, wq, q_gain, kmem, vmem, wo, l):
    B, S, D = x3.shape
    N = kmem.shape[2]
    tm = min(TOKEN_TILE, S)
    return pl.pallas_call(
        _xattn_kernel,
        grid=(B, S // tm),
        in_specs=[
            pl.BlockSpec((None, tm, SB_WIDTH), lambda b, i: (b, i, 0)),
            pl.BlockSpec((None, tm, SWA_WIDTH), lambda b, i: (b, i, 0)),
            pl.BlockSpec((None, 1, SB_WIDTH), lambda b, i: (l, 0, 0)),
            pl.BlockSpec((None, 1, SWA_WIDTH), lambda b, i: (l, 0, 0)),
            pl.BlockSpec((None, SB_WIDTH + SWA_WIDTH, D), lambda b, i: (l, 0, 0)),
            pl.BlockSpec((None, tm, D), lambda b, i: (b, i, 0)),
            pl.BlockSpec((None, 1, D), lambda b, i: (l, 0, 0)),
            pl.BlockSpec((None, D, MEM_WIDTH), lambda b, i: (l, 0, 0)),
            pl.BlockSpec((None, 1, MEM_HEAD_DIM), lambda b, i: (l, 0, 0)),
            pl.BlockSpec((None, None, N, MEM_WIDTH), lambda b, i: (l, b, 0, 0)),
            pl.BlockSpec((None, None, N, MEM_WIDTH), lambda b, i: (l, b, 0, 0)),
            pl.BlockSpec((None, MEM_WIDTH, D), lambda b, i: (l, 0, 0)),
        ],
        out_specs=pl.BlockSpec((None, tm, D), lambda b, i: (b, i, 0)),
        out_shape=jax.ShapeDtypeStruct((B, S, D), F32),
        compiler_params=_params("arbitrary", "arbitrary"),
        name="xattn",
    )(sb_o, sw_o, g_sb, g_sw, w_out, x3, norm_x, wq, q_gain, kmem, vmem, wo)


def _swiglu_step(h, wg_ref, wu_ref, wd_ref):
    a = _dot(h, wg_ref[...])
    u = _dot(h, wu_ref[...])
    act = a / (1.0 + jnp.exp(-a)) * u
    return _dot(act.astype(BF16), wd_ref[...])


def _ffn_kernel(x_ref, g_ref, wg_ref, wu_ref, wd_ref, o_ref, h_ref):
    @pl.when(pl.program_id(1) == 0)
    def _():
        x = x_ref[...]
        h_ref[...] = _rms(x, g_ref[...]).astype(BF16)
        o_ref[...] = x

    o_ref[...] += _swiglu_step(h_ref[...], wg_ref, wu_ref, wd_ref)


def _ffn_dense(x2, g, wg, wu, wd, l, i_dense):
    T, D = x2.shape
    F = wg.shape[-1]
    tm, tf = min(FFN_TOKEN_TILE, T), FFN_COL_TILE
    return pl.pallas_call(
        _ffn_kernel,
        grid=(T // tm, F // tf),
        in_specs=[
            pl.BlockSpec((tm, D), lambda i, f: (i, 0)),
            pl.BlockSpec((None, 1, D), lambda i, f: (l, 0, 0)),
            pl.BlockSpec((None, D, tf), lambda i, f: (i_dense, 0, f)),
            pl.BlockSpec((None, D, tf), lambda i, f: (i_dense, 0, f)),
            pl.BlockSpec((None, tf, D), lambda i, f: (i_dense, f, 0)),
        ],
        out_specs=pl.BlockSpec((tm, D), lambda i, f: (i, 0)),
        out_shape=jax.ShapeDtypeStruct((T, D), F32),
        scratch_shapes=[pltpu.VMEM((tm, D), BF16)],
        compiler_params=_params("parallel", "arbitrary"),
        name="ffn_dense",
    )(x2, g, wg, wu, wd)


def _moe_ffn_kernel(be_ref, nu_ref, xs_ref, wg_ref, wu_ref, wd_ref, o_ref, h_ref):
    del be_ref
    used = pl.program_id(0) < nu_ref[0]

    @pl.when(pl.program_id(1) == 0)
    def _():
        o_ref[...] = jnp.zeros_like(o_ref)

    @pl.when(used & (pl.program_id(1) == 0))
    def _():
        h_ref[...] = xs_ref[...].astype(BF16)

    @pl.when(used)
    def _():
        o_ref[...] += _swiglu_step(h_ref[...], wg_ref, wu_ref, wd_ref)


def _moe_ffn(blk_e, n_used, xs, wg, wu, wd, i_moe):
    P, D = xs.shape
    F = wg.shape[-1]
    tm, tf = MOE_ROW_TILE, FFN_COL_TILE
    nf = F // tf
    row_blk = lambda i, nu: jnp.minimum(i, nu[0] - 1)
    col_blk = lambda i, f, nu: jnp.where(i < nu[0], f, nf - 1)
    return pl.pallas_call(
        _moe_ffn_kernel,
        grid_spec=pltpu.PrefetchScalarGridSpec(
            num_scalar_prefetch=2,
            grid=(P // tm, nf),
            in_specs=[
                pl.BlockSpec((tm, D), lambda i, f, be, nu: (row_blk(i, nu), 0)),
                pl.BlockSpec((None, None, D, tf), lambda i, f, be, nu: (i_moe, be[i], 0, col_blk(i, f, nu))),
                pl.BlockSpec((None, None, D, tf), lambda i, f, be, nu: (i_moe, be[i], 0, col_blk(i, f, nu))),
                pl.BlockSpec((None, None, tf, D), lambda i, f, be, nu: (i_moe, be[i], col_blk(i, f, nu), 0)),
            ],
            out_specs=pl.BlockSpec((tm, D), lambda i, f, be, nu: (i, 0)),
            scratch_shapes=[pltpu.VMEM((tm, D), BF16)],
        ),
        out_shape=jax.ShapeDtypeStruct((P, D), F32),
        compiler_params=_params("arbitrary", "arbitrary"),
        name="moe_ffn",
    )(blk_e, n_used, xs, wg, wu, wd)


def _router_kernel(x_ref, g_ref, rw_ref, rb_ref, hf_ref, ei_ref, gt_ref, cnt_ref, carry_ref):
    tm = x_ref.shape[0]

    @pl.when(pl.program_id(0) == 0)
    def _():
        carry_ref[...] = jnp.zeros_like(carry_ref)

    hf = _rms(x_ref[...], g_ref[...])
    hf_ref[...] = hf
    lane = lax.broadcasted_iota(jnp.int32, (tm, LANES), 1)
    logits = jnp.dot(hf, rw_ref[...], precision=lax.Precision.HIGHEST, preferred_element_type=F32) + rb_ref[...]
    logits = jnp.where(lane < N_EXPERTS, logits, -jnp.inf)
    m1 = jnp.max(logits, axis=-1, keepdims=True)
    i1 = jnp.min(jnp.where(logits == m1, lane, LANES), axis=-1, keepdims=True)
    rest = jnp.where(lane == i1, -jnp.inf, logits)
    m2 = jnp.max(rest, axis=-1, keepdims=True)
    i2 = jnp.min(jnp.where(rest == m2, lane, LANES), axis=-1, keepdims=True)
    e = jnp.exp(m2 - m1)
    g1 = 1.0 / (1.0 + e)
    g2 = e / (1.0 + e)
    oh1 = lane == i1
    oh2 = lane == i2
    oh = (oh1 | oh2).astype(F32)
    row = lax.broadcasted_iota(jnp.int32, (tm, tm), 0)
    col = lax.broadcasted_iota(jnp.int32, (tm, tm), 1)
    earlier = (row > col).astype(BF16)
    before = _dot(earlier, oh.astype(BF16)) + carry_ref[...]
    r1 = jnp.sum(jnp.where(oh1, before, 0.0), axis=-1, keepdims=True).astype(jnp.int32)
    r2 = jnp.sum(jnp.where(oh2, before, 0.0), axis=-1, keepdims=True).astype(jnp.int32)
    carry_ref[...] += jnp.sum(oh, axis=0, keepdims=True)
    cnt_ref[...] = carry_ref[...]
    ei = jnp.where(lane == 0, i1, jnp.where(lane == 1, i2, jnp.where(lane == 2, r1, jnp.where(lane == 3, r2, 0))))
    ei_ref[...] = ei[:, :8]
    gt = jnp.where(lane == 0, g1, jnp.where(lane == 1, g2, 0.0))
    gt_ref[...] = gt[:, :8]


def _router(x2, g, rw, rb, l, i_moe):
    T, D = x2.shape
    tm = TOKEN_TILE
    return pl.pallas_call(
        _router_kernel,
        grid=(T // tm,),
        in_specs=[
            pl.BlockSpec((tm, D), lambda i: (i, 0)),
            pl.BlockSpec((None, 1, D), lambda i: (l, 0, 0)),
            pl.BlockSpec((None, D, LANES), lambda i: (i_moe, 0, 0)),
            pl.BlockSpec((None, 1, LANES), lambda i: (i_moe, 0, 0)),
        ],
        out_specs=[
            pl.BlockSpec((tm, D), lambda i: (i, 0)),
            pl.BlockSpec((tm, 8), lambda i: (i, 0)),
            pl.BlockSpec((tm, 8), lambda i: (i, 0)),
            pl.BlockSpec((1, LANES), lambda i: (0, 0)),
        ],
        out_shape=[
            jax.ShapeDtypeStruct((T, D), F32),
            jax.ShapeDtypeStruct((T, 8), jnp.int32),
            jax.ShapeDtypeStruct((T, 8), F32),
            jax.ShapeDtypeStruct((1, LANES), F32),
        ],
        scratch_shapes=[pltpu.VMEM((1, LANES), F32)],
        compiler_params=_params("arbitrary"),
        name="router",
    )(x2, g, rw, rb)


def _dispatch_kernel(last_blk_ref, dest_ref, hf_ref, xs_ref, zero_ref, sem):
    tm = hf_ref.shape[0]

    @pl.when(pl.program_id(0) == 0)
    def _():
        zero_ref[...] = jnp.zeros_like(zero_ref)
        blk_rows = zero_ref.shape[0]

        def fill(e):
            start = pl.multiple_of(last_blk_ref[e] * blk_rows, blk_rows)
            return pltpu.make_async_copy(zero_ref, xs_ref.at[pl.ds(start, blk_rows)], sem)

        for e in range(N_EXPERTS):
            fill(e).start()
        for e in range(N_EXPERTS):
            fill(e).wait()

    def row_copy(r, d):
        return pltpu.make_async_copy(hf_ref.at[pl.ds(r, 1)], xs_ref.at[pl.ds(d, 1)], sem)

    def issue(r, _):
        row_copy(r, dest_ref[2 * r]).start()
        row_copy(r, dest_ref[2 * r + 1]).start()
        return 0

    lax.fori_loop(0, tm, issue, 0, unroll=DMA_ISSUE_UNROLL)
    for _ in range(2):
        pltpu.make_async_copy(hf_ref, xs_ref.at[pl.ds(0, tm)], sem).wait()


def _dispatch(last_blk, dest, hf, n_rows):
    T, D = hf.shape
    tm = TOKEN_TILE
    return pl.pallas_call(
        _dispatch_kernel,
        grid_spec=pltpu.PrefetchScalarGridSpec(
            num_scalar_prefetch=1,
            grid=(T // tm,),
            in_specs=[
                pl.BlockSpec((2 * tm,), lambda i, lb: (i,), memory_space=pltpu.SMEM),
                pl.BlockSpec((tm, D), lambda i, lb: (i, 0)),
            ],
            out_specs=pl.BlockSpec(memory_space=pl.ANY),
            scratch_shapes=[pltpu.VMEM((MOE_ROW_TILE, D), F32), pltpu.SemaphoreType.DMA(())],
        ),
        out_shape=jax.ShapeDtypeStruct((n_rows, D), F32),
        compiler_params=_params("arbitrary"),
        name="moe_dispatch",
    )(last_blk, dest, hf)


def _combine_kernel(dest_ref, x_ref, gt_ref, y_ref, o_ref, buf_ref, sem):
    tm = x_ref.shape[0]

    def row_copy(slot, r, d):
        return pltpu.make_async_copy(y_ref.at[pl.ds(d, 1)], buf_ref.at[slot, pl.ds(r, 1)], sem)

    def issue(r, _):
        row_copy(0, r, dest_ref[2 * r]).start()
        row_copy(1, r, dest_ref[2 * r + 1]).start()
        return 0

    lax.fori_loop(0, tm, issue, 0, unroll=DMA_ISSUE_UNROLL)
    for slot in range(2):
        pltpu.make_async_copy(y_ref.at[pl.ds(0, tm)], buf_ref.at[slot], sem).wait()
    gt = gt_ref[...]
    o_ref[...] = x_ref[...] + gt[:, 0:1] * buf_ref[0] + gt[:, 1:2] * buf_ref[1]


def _combine(dest, x2, gates, y):
    T, D = x2.shape
    tm = TOKEN_TILE
    return pl.pallas_call(
        _combine_kernel,
        grid=(T // tm,),
        in_specs=[
            pl.BlockSpec((2 * tm,), lambda i: (i,), memory_space=pltpu.SMEM),
            pl.BlockSpec((tm, D), lambda i: (i, 0)),
            pl.BlockSpec((tm, 8), lambda i: (i, 0)),
            pl.BlockSpec(memory_space=pl.ANY),
        ],
        out_specs=pl.BlockSpec((tm, D), lambda i: (i, 0)),
        out_shape=jax.ShapeDtypeStruct((T, D), F32),
        scratch_shapes=[pltpu.VMEM((2, tm, D), F32), pltpu.SemaphoreType.DMA(())],
        compiler_params=_params("arbitrary"),
        name="moe_combine",
    )(dest, x2, gates, y)


def _moe(x2, g, rw, rb, wg, wu, wd, l, i_moe):
    T, D = x2.shape
    tmm = MOE_ROW_TILE
    P = 2 * T + N_EXPERTS * tmm
    hf, ei, gates, cnt = _router(x2, g, rw, rb, l, i_moe)
    counts = cnt[0, :N_EXPERTS].astype(jnp.int32)
    pcounts = (counts + tmm - 1) // tmm * tmm
    pends = jnp.cumsum(pcounts)
    pstarts = pends - pcounts
    experts = ei[:, 0:2]
    start_of = sum(jnp.where(experts == e, pstarts[e], 0) for e in range(N_EXPERTS))
    dest = (start_of + ei[:, 2:4]).reshape(-1)
    blk_start = jnp.arange(P // tmm, dtype=jnp.int32) * tmm
    blk_e = jnp.minimum(jnp.sum(blk_start[:, None] >= pends[None, :], axis=1), N_EXPERTS - 1).astype(jnp.int32)
    n_used = (pends[N_EXPERTS - 1:] // tmm).astype(jnp.int32)
    last_blk = jnp.maximum(pends // tmm - 1, 0).astype(jnp.int32)
    xs = _dispatch(last_blk, dest, hf, P)
    y = _moe_ffn(blk_e, n_used, xs, wg, wu, wd, i_moe)
    return _combine(dest, x2, gates, y)


def _t5_buckets(dist):
    n = np.maximum(dist, 0)
    max_exact = N_BUCKETS // 2
    large = max_exact + (np.log(np.maximum(n, 1) / max_exact) / np.log(MAX_DISTANCE / max_exact)
                         * (N_BUCKETS - max_exact)).astype(np.int32)
    large = np.minimum(large, N_BUCKETS - 1)
    return np.where(n < max_exact, n, large).astype(np.int32)


def kernel(x, mem, norm_mix, w_in, sb_out_gain, swa_q_gain, swa_k_gain, swa_sinks, swa_out_gain, rel_bias, w_out, norm_xattn, norm_mem, xattn_wq, xattn_wkv, xattn_q_gain, xattn_k_gain, xattn_wo, norm_ffn, dense_w_gate, dense_w_up, dense_w_down, router_w, router_b, exp_w_gate, exp_w_up, exp_w_down):
    B, S, D = x.shape
    depth = w_in.shape[0]
    T = B * S
    row3 = lambda a: a.reshape(a.shape[0], 1, a.shape[1])
    bf = lambda a: a.astype(BF16)

    dist = WINDOW + np.arange(WINDOW)[:, None] - np.arange(2 * WINDOW)[None, :]
    swa_bias = jnp.transpose(rel_bias[_t5_buckets(dist)], (2, 0, 1))
    band = (dist >= 0) & (dist < WINDOW)
    band = np.stack([band & (np.arange(2 * WINDOW)[None, :] >= WINDOW), band])
    swa_bias = jnp.where(band[:, None], swa_bias[None], -jnp.inf)
    router_w_p = jnp.pad(router_w, ((0, 0), (0, 0), (0, LANES - N_EXPERTS)))
    router_b_p = row3(jnp.pad(router_b, ((0, 0), (0, LANES - N_EXPERTS))))

    w_in_b, w_out_b = bf(w_in), bf(w_out)
    wq_b, wkv_b, wo_b = bf(xattn_wq), bf(xattn_wkv), bf(xattn_wo)
    dg_b, du_b, dd_b = bf(dense_w_gate), bf(dense_w_up), bf(dense_w_down)
    eg_b, eu_b, ed_b = bf(exp_w_gate), bf(exp_w_up), bf(exp_w_down)
    norm_mix3, norm_x3, norm_f3 = row3(norm_mix), row3(norm_xattn), row3(norm_ffn)
    sb_g3, sw_g3 = row3(sb_out_gain), row3(swa_out_gain)
    xq_g3, xk_g3 = row3(xattn_q_gain), row3(xattn_k_gain)

    kmem, vmem = _memkv(mem, row3(norm_mem), wkv_b, xk_g3)

    x2 = x.reshape(T, D)
    for l in range(depth):
        head_gain = jnp.concatenate([jnp.tile(swa_q_gain[l], SWA_Q_HEADS) * (HEAD_DIM ** -0.5),
                                     jnp.tile(swa_k_gain[l], SWA_KV_HEADS)])[None, :]
        proj = _inproj(x2, norm_mix3, w_in_b, head_gain, l).reshape(B, S, -1)
        sb_o = _sb_attention(proj)
        sw_o = _swa_attention(proj, swa_sinks[l], swa_bias)
        x2 = _xattn(sb_o, sw_o, sb_g3, sw_g3, w_out_b, x2.reshape(B, S, D), norm_x3, wq_b, xq_g3, kmem, vmem,
                    wo_b, l).reshape(T, D)
        if l % 2 == 0:
            x2 = _ffn_dense(x2, norm_f3, dg_b, du_b, dd_b, l, l // 2)
        else:
            x2 = _moe(x2, norm_f3, router_w_p, router_b_p, eg_b, eu_b, ed_b, l, l // 2)
    return x2.reshape(B, S, D)
```

```python
import functools

import numpy as np
import jax
import jax.numpy as jnp
from jax import lax
from jax.experimental import pallas as pl
from jax.experimental.pallas import tpu as pltpu

F32 = jnp.float32
BF16 = jnp.bfloat16

HEAD_DIM = 64
SB_WIDTH = 512
SWA_WIDTH = 512
SWA_Q_HEADS = 8
SWA_GROUP = 4
SWA_KV_HEADS = 2
WINDOW = 128
N_BUCKETS = 32
MAX_DISTANCE = 128
MEM_HEADS = 4
MEM_HEAD_DIM = 128
MEM_WIDTH = 512
N_EXPERTS = 8
EPS = 1e-6
LANES = 128

VMEM_LIMIT = 56 * 1024 * 1024

TOKEN_TILE = 512
PROJ_TOKEN_TILE = 1024
FFN_TOKEN_TILE = 1024
FFN_COL_TILE = 512
SB_TILE = 256
MOE_ROW_TILE = 1024
DMA_ISSUE_UNROLL = 8
SB_LOG_UNDERFLOW = -104.0


def _params(*sem):
    return pltpu.CompilerParams(dimension_semantics=("arbitrary",) * len(sem), vmem_limit_bytes=VMEM_LIMIT)


def _rms(x, g):
    return x * lax.rsqrt(jnp.mean(x * x, axis=-1, keepdims=True) + EPS) * g


def _dot(a, b):
    return jnp.dot(a, b, preferred_element_type=F32)


def _dot_nt(a, b):
    return lax.dot_general(a, b, (((1,), (1,)), ((), ())), preferred_element_type=F32)


def _group_mean_sq(x, ones_ref):
    sq = x * x
    hi = sq.astype(BF16)
    lo = (sq - hi.astype(F32)).astype(BF16)
    return _dot(hi, ones_ref[...]) + _dot(lo, ones_ref[...])


def _inproj_kernel(x_ref, g_ref, w_ref, hg_ref, ones_q_ref, ones_k_ref, o_ref):
    h = _rms(x_ref[...], g_ref[...]).astype(BF16)
    sb_cols = 3 * SB_WIDTH
    for c in range(3):
        sl = slice(c * SB_WIDTH, (c + 1) * SB_WIDTH)
        o_ref[:, sl] = _dot(h, w_ref[:, sl]).astype(BF16)
    y = _dot(h, w_ref[:, sb_cols:])
    half = SWA_WIDTH // 2
    qk = SWA_WIDTH + SWA_KV_HEADS * HEAD_DIM
    ms = jnp.concatenate([_group_mean_sq(y[:, :half], ones_q_ref),
                          _group_mean_sq(y[:, half:SWA_WIDTH], ones_q_ref),
                          _group_mean_sq(y[:, SWA_WIDTH:qk], ones_k_ref)], axis=1)
    o_ref[:, sb_cols:sb_cols + qk] = (y[:, :qk] * lax.rsqrt(ms + EPS) * hg_ref[...]).astype(BF16)
    o_ref[:, sb_cols + qk:] = y[:, qk:].astype(BF16)


def _head_group_ones(n):
    return jnp.asarray((np.arange(n)[:, None] // HEAD_DIM == np.arange(n)[None, :] // HEAD_DIM) / HEAD_DIM, BF16)


def _inproj(x2, g, w, head_gain, l):
    T, D = x2.shape
    N = w.shape[-1]
    tm = min(PROJ_TOKEN_TILE, T)
    half, kv = SWA_WIDTH // 2, SWA_KV_HEADS * HEAD_DIM
    const = lambda *shape: pl.BlockSpec(shape, lambda i: (0,) * len(shape))
    return pl.pallas_call(
        _inproj_kernel,
        grid=(T // tm,),
        in_specs=[
            pl.BlockSpec((tm, D), lambda i: (i, 0)),
            pl.BlockSpec((None, 1, D), lambda i: (l, 0, 0)),
            pl.BlockSpec((None, D, N), lambda i: (l, 0, 0)),
            const(1, SWA_WIDTH + kv), const(half, half), const(kv, kv),
        ],
        out_specs=pl.BlockSpec((tm, N), lambda i: (i, 0)),
        out_shape=jax.ShapeDtypeStruct((T, N), BF16),
        compiler_params=_params("arbitrary"),
        name="inproj",
    )(x2, g, w, head_gain, _head_group_ones(half), _head_group_ones(kv))


def _sb_kernel(q_ref, k_ref, v_ref, suf_ref, o_ref, kt_ref, z_ref, zn_ref, acc_ref, c_ref, *, tile):
    S = q_ref.shape[0]
    lane = lax.broadcasted_iota(jnp.int32, (1, LANES), 1)
    lo_half = lane < HEAD_DIM
    row = lax.broadcasted_iota(jnp.int32, (tile, tile), 0)
    col = lax.broadcasted_iota(jnp.int32, (tile, tile), 1)
    strict = col < row
    suffix = suf_ref[...]

    for j in range(S // tile):
        kt_ref[j] = k_ref[j * tile:(j + 1) * tile, :].T

    heads = range(2)
    sl = [slice(h * tile, (h + 1) * tile) for h in heads]

    def log_terms(z, masked):
        n = range(len(z))
        log_beta = [jnp.minimum(z[u], 0.0) - jnp.log(1.0 + jnp.exp(-jnp.abs(z[u]))) for u in n]
        log_om = [log_beta[u] - z[u] for u in n]
        log_om = [jnp.where(strict, log_om[u], 0.0) if masked[u] else log_om[u] for u in n]
        hi = [log_om[u].astype(BF16) for u in n]
        lo = [(log_om[u] - hi[u].astype(F32)).astype(BF16) for u in n]
        tail = [_dot(jnp.concatenate([hi[u], lo[u]], axis=0), suffix) for u in n]
        tail = [tail[u][:tile] + tail[u][tile:] for u in n]
        row_sum = [jnp.sum(log_om[u], axis=-1, keepdims=True) for u in n]
        return log_beta, tail, row_sum

    n_q = S // tile

    def key_rows(j):
        return pl.ds(pl.multiple_of(j * tile, tile), tile)

    def head_queries(i):
        q2 = q_ref[key_rows(i), :] * jnp.asarray(HEAD_DIM ** -0.5, BF16)
        zero = jnp.zeros_like(q2)
        return [jnp.where(lo_half, q2, zero), jnp.where(lo_half, zero, q2)]

    def next_tile_scores(i, into):
        if n_q > 1:
            nxt = jnp.minimum(i + 1, n_q - 1)
            qn = head_queries(nxt)
            kt_d, kt_o = kt_ref[nxt], kt_ref[nxt - 1]
            for h in heads:
                zn_ref[into, h] = _dot(qn[h], kt_d)
                zn_ref[into, 2 + h] = _dot(qn[h], kt_o)

    def q_block(i, parity):
        q0 = pl.multiple_of(i * tile, tile)
        qs = head_queries(i)

        def prefetch_scores(j, slot):
            kt = kt_ref[jnp.maximum(j, 0)]
            for h in heads:
                z_ref[slot, sl[h]] = _dot(qs[h], kt)

        def first_block():
            kt = kt_ref[0]
            z = [_dot(qs[h], kt) for h in heads]
            next_tile_scores(i, 1 - parity)
            log_beta, tail, row_sum = log_terms(z, [True, True])
            a = [jnp.where(strict, jnp.exp(log_beta[h] + tail[h]), 0.0) for h in heads]
            v_d = v_ref[0:tile, :]
            for h in heads:
                acc_ref[sl[h]] = _dot(a[h].astype(BF16), v_d)
                c_ref[sl[h]] = row_sum[h]
            return jnp.float32(0.0)

        def later_block():
            z = [zn_ref[parity, u] for u in range(4)]
            next_tile_scores(i, 1 - parity)
            prefetch_scores(i - 2, 0)
            log_beta, tail, row_sum = log_terms(z, [True, True, False, False])
            a_d = [jnp.where(strict, jnp.exp(log_beta[h] + tail[h]), 0.0) for h in heads]
            a_o = [jnp.exp(log_beta[2 + h] + tail[2 + h] + row_sum[h]) for h in heads]
            v_d, v_o = v_ref[key_rows(i), :], v_ref[key_rows(i - 1), :]
            c = [row_sum[h] + row_sum[2 + h] for h in heads]
            for h in heads:
                acc_ref[sl[h]] = _dot(a_d[h].astype(BF16), v_d) + _dot(a_o[h].astype(BF16), v_o)
                c_ref[sl[h]] = c[h]
            return jnp.maximum(jnp.max(c[0]), jnp.max(c[1]))

        c_max = lax.cond(i == 0, first_block, later_block) if parity == 0 else later_block()

        def key_tile(state):
            j, slot, _ = state
            z = [z_ref[slot, sl[h]] for h in heads]
            prefetch_scores(j - 1, 1 - slot)
            log_beta, tail, row_sum = log_terms(z, [False, False])
            c = [c_ref[sl[h]] for h in heads]
            a = [jnp.exp(log_beta[h] + tail[h] + c[h]) for h in heads]
            c = [c[h] + row_sum[h] for h in heads]
            v_j = v_ref[key_rows(j), :]
            for h in heads:
                acc_ref[sl[h]] += _dot(a[h].astype(BF16), v_j)
                c_ref[sl[h]] = c[h]
            return j - 1, 1 - slot, jnp.maximum(jnp.max(c[0]), jnp.max(c[1]))

        lax.while_loop(lambda s: (s[0] >= 0) & (s[2] > SB_LOG_UNDERFLOW), key_tile,
                       (i - 2, jnp.int32(0), c_max))
        o_ref[pl.ds(q0, tile), :] = jnp.where(lo_half, acc_ref[:tile], acc_ref[tile:]).astype(BF16)

    if n_q == 1:
        q_block(0, 0)
    else:
        def q_block_pair(p, _):
            q_block(2 * p, 0)
            q_block(2 * p + 1, 1)
            return 0

        lax.fori_loop(0, n_q // 2, q_block_pair, 0)


def _sb_attention(proj3):
    B, S, _ = proj3.shape
    pairs = SB_WIDTH // LANES
    tile = min(SB_TILE, S)
    assert S % tile == 0 and (S // tile == 1 or (S // tile) % 2 == 0)
    suf = jnp.asarray(np.arange(tile)[:, None] > np.arange(tile)[None, :], BF16)
    spec = lambda off: pl.BlockSpec((None, S, LANES), lambda b, p: (b, 0, off + p))
    return pl.pallas_call(
        functools.partial(_sb_kernel, tile=tile),
        grid=(B, pairs),
        in_specs=[spec(0), spec(pairs), spec(2 * pairs),
                  pl.BlockSpec((tile, tile), lambda b, p: (0, 0))],
        out_specs=pl.BlockSpec((None, S, LANES), lambda b, p: (b, 0, p)),
        out_shape=jax.ShapeDtypeStruct((B, S, SB_WIDTH), BF16),
        scratch_shapes=[pltpu.VMEM((S // tile, LANES, tile), BF16),
                        pltpu.VMEM((2, 2 * tile, tile), F32), pltpu.VMEM((2, 4, tile, tile), F32),
                        pltpu.VMEM((2 * tile, LANES), F32), pltpu.VMEM((2 * tile, 1), F32)],
        compiler_params=_params("parallel", "parallel"),
        name="sb_attention",
    )(proj3, proj3, proj3, suf)


def _swa_kernel(sink_ref, q_ref, kp_ref, kc_ref, vp_ref, vc_ref, sel_ref, bias_ref, o_ref):
    W = WINDOW
    qn = q_ref[...]
    kn_t = jnp.concatenate([kp_ref[...], kc_ref[...]], axis=0).T
    v = jnp.concatenate([vp_ref[...], vc_ref[...]], axis=0)
    lane = lax.broadcasted_iota(jnp.int32, (1, LANES), 1)
    lo_half = lane < HEAD_DIM
    zero = jnp.zeros((W, LANES), BF16)
    head_row = lax.broadcasted_iota(jnp.int32, (SWA_GROUP * W, 1), 0) // W
    groups = range(SWA_KV_HEADS)
    kt_g = [kn_t[g * HEAD_DIM:(g + 1) * HEAD_DIM, :] for g in groups]
    kt_dup = [jnp.concatenate([kt_g[g], kt_g[g]], axis=0) for g in groups]
    v_dup = [_dot(v, sel_ref[g]).astype(BF16) for g in groups]
    first_slab = jnp.minimum(pl.program_id(1), 1)
    chains = [(blk, g) for blk in range(2) for g in groups]
    lhs = []
    for blk, g in chains:
        rows = []
        for t in range(g * SWA_GROUP // 2, (g + 1) * SWA_GROUP // 2):
            qt = qn[blk * W:(blk + 1) * W, t * LANES:(t + 1) * LANES]
            rows += [jnp.where(lo_half, qt, zero), jnp.where(lo_half, zero, qt)]
        lhs.append(jnp.concatenate(rows, axis=0))
    s = [_dot(lhs[c], kt_dup[g][:, blk * W:(blk + 2) * W]) for c, (blk, g) in enumerate(chains)]
    bias = [bias_ref[first_slab], bias_ref[1]]
    s = [s[c] + bias[blk][g * SWA_GROUP:(g + 1) * SWA_GROUP].reshape(SWA_GROUP * W, 2 * W)
         for c, (blk, g) in enumerate(chains)]
    sink = []
    for g in groups:
        sk = jnp.zeros((SWA_GROUP * W, 1), F32)
        for hh in range(SWA_GROUP):
            sk = jnp.where(head_row == hh, sink_ref[g * SWA_GROUP + hh], sk)
        sink.append(sk)
    m = [jnp.maximum(jnp.max(s[c], axis=-1, keepdims=True), sink[g]) for c, (blk, g) in enumerate(chains)]
    p = [jnp.exp(s[c] - m[c]) for c in range(len(chains))]
    den = [jnp.sum(p[c], axis=-1, keepdims=True) + jnp.exp(sink[g] - m[c]) for c, (blk, g) in enumerate(chains)]
    r = [_dot(p[c].astype(BF16), v_dup[g][blk * W:(blk + 2) * W]) / den[c]
         for c, (blk, g) in enumerate(chains)]
    for c, (blk, g) in enumerate(chains):
        for tt in range(SWA_GROUP // 2):
            t = g * SWA_GROUP // 2 + tt
            o_ref[blk * W:(blk + 1) * W, t * LANES:(t + 1) * LANES] = jnp.where(
                lo_half, r[c][2 * tt * W:(2 * tt + 1) * W], r[c][(2 * tt + 1) * W:(2 * tt + 2) * W]).astype(BF16)


def _swa_attention(proj3, sinks, bias):
    B, S, _ = proj3.shape
    W = WINDOW
    q_blk = (3 * SB_WIDTH) // SWA_WIDTH
    k_blk = (3 * SB_WIDTH + SWA_WIDTH) // LANES
    v_blk = k_blk + 1
    sel = jnp.asarray(np.stack([np.arange(LANES)[:, None] == g * HEAD_DIM + np.arange(LANES)[None, :] % HEAD_DIM
                                for g in range(SWA_KV_HEADS)]), BF16)
    prev = lambda c: pl.BlockSpec((None, W, LANES), lambda b, n: (b, jnp.maximum(2 * n - 1, 0), c))
    cur = lambda c: pl.BlockSpec((None, 2 * W, LANES), lambda b, n: (b, n, c))
    const = lambda *shape: pl.BlockSpec(shape, lambda b, n: (0,) * len(shape))
    return pl.pallas_call(
        _swa_kernel,
        grid=(B, S // (2 * W)),
        in_specs=[
            pl.BlockSpec(memory_space=pltpu.SMEM),
            pl.BlockSpec((None, 2 * W, SWA_WIDTH), lambda b, n: (b, n, q_blk)),
            prev(k_blk), cur(k_blk), prev(v_blk), cur(v_blk),
            const(SWA_KV_HEADS, LANES, LANES),
            const(2, SWA_Q_HEADS, W, 2 * W),
        ],
        out_specs=pl.BlockSpec((None, 2 * W, SWA_WIDTH), lambda b, n: (b, n, 0)),
        out_shape=jax.ShapeDtypeStruct((B, S, SWA_WIDTH), BF16),
        compiler_params=_params("arbitrary", "arbitrary"),
        name="swa_attention",
    )(sinks, proj3, proj3, proj3, proj3, proj3, sel, bias)


def _outproj(sb_ref, sw_ref, gsb_ref, gsw_ref, w_ref, x):
    a = _rms(sb_ref[...].astype(F32), gsb_ref[...]).astype(BF16)
    b = _rms(sw_ref[...].astype(F32), gsw_ref[...]).astype(BF16)
    return x + _dot(a, w_ref[:SB_WIDTH, :]) + _dot(b, w_ref[SB_WIDTH:, :])


def _memkv_kernel(m_ref, g_ref, w_ref, kg_ref, k_ref, v_ref):
    h = _rms(m_ref[...], g_ref[...]).astype(BF16)
    kv = _dot(h, w_ref[...])
    for hd in range(MEM_HEADS):
        sl = slice(hd * MEM_HEAD_DIM, (hd + 1) * MEM_HEAD_DIM)
        k_ref[:, sl] = _rms(kv[:, sl], kg_ref[...]).astype(BF16)
    v_ref[...] = kv[:, MEM_WIDTH:].astype(BF16)


def _memkv(mem, norm_mem, wkv, k_gain):
    B, N, D = mem.shape
    L = wkv.shape[0]
    out = jax.ShapeDtypeStruct((L, B, N, MEM_WIDTH), BF16)
    return pl.pallas_call(
        _memkv_kernel,
        grid=(L, B),
        in_specs=[
            pl.BlockSpec((None, N, D), lambda l, b: (b, 0, 0)),
            pl.BlockSpec((None, 1, D), lambda l, b: (l, 0, 0)),
            pl.BlockSpec((None, D, 2 * MEM_WIDTH), lambda l, b: (l, 0, 0)),
            pl.BlockSpec((None, 1, MEM_HEAD_DIM), lambda l, b: (l, 0, 0)),
        ],
        out_specs=[pl.BlockSpec((None, None, N, MEM_WIDTH), lambda l, b: (l, b, 0, 0))] * 2,
        out_shape=[out, out],
        compiler_params=_params("parallel", "parallel"),
        name="memkv",
    )(mem, norm_mem, wkv, k_gain)


def _xattn_kernel(sb_ref, sw_ref, gsb_ref, gsw_ref, wout_ref, x_ref, gx_ref, wq_ref, qg_ref, k_ref, v_ref, wo_ref,
                  o_ref):
    x = _outproj(sb_ref, sw_ref, gsb_ref, gsw_ref, wout_ref, x_ref[...])
    h = _rms(x, gx_ref[...]).astype(BF16)
    q = _dot(h, wq_ref[...])
    outs = []
    for hd in range(MEM_HEADS):
        sl = slice(hd * MEM_HEAD_DIM, (hd + 1) * MEM_HEAD_DIM)
        qn = _rms(q[:, sl], qg_ref[...]).astype(BF16)
        s = _dot_nt(qn, k_ref[:, sl]) * (MEM_HEAD_DIM ** -0.5)
        p = jnp.exp(s - jnp.max(s, axis=-1, keepdims=True))
        den = jnp.sum(p, axis=-1, keepdims=True)
        outs.append((_dot(p.astype(BF16), v_ref[:, sl]) / den).astype(BF16))
    o_ref[...] = x + _dot(jnp.concatenate(outs, axis=1), wo_ref[...])


def _xattn(sb_o, sw_o, g_sb, g_sw, w_out, x3, norm_x, wq, q_gain, kmem, vmem, wo, l):
    B, S, D = x3.shape
    N = kmem.shape[2]
    tm = min(PROJ_TOKEN_TILE, S)
    return pl.pallas_call(
        _xattn_kernel,
        grid=(B, S // tm),
        in_specs=[
            pl.BlockSpec((None, tm, SB_WIDTH), lambda b, i: (b, i, 0)),
            pl.BlockSpec((None, tm, SWA_WIDTH), lambda b, i: (b, i, 0)),
            pl.BlockSpec((None, 1, SB_WIDTH), lambda b, i: (l, 0, 0)),
            pl.BlockSpec((None, 1, SWA_WIDTH), lambda b, i: (l, 0, 0)),
            pl.BlockSpec((None, SB_WIDTH + SWA_WIDTH, D), lambda b, i: (l, 0, 0)),
            pl.BlockSpec((None, tm, D), lambda b, i: (b, i, 0)),
            pl.BlockSpec((None, 1, D), lambda b, i: (l, 0, 0)),
            pl.BlockSpec((None, D, MEM_WIDTH), lambda b, i: (l, 0, 0)),
            pl.BlockSpec((None, 1, MEM_HEAD_DIM), lambda b, i: (l, 0, 0)),
            pl.BlockSpec((None, None, N, MEM_WIDTH), lambda b, i: (l, b, 0, 0)),
            pl.BlockSpec((None, None, N, MEM_WIDTH), lambda b, i: (l, b, 0, 0)),
            pl.BlockSpec((None, MEM_WIDTH, D), lambda b, i: (l, 0, 0)),
        ],
        out_specs=pl.BlockSpec((None, tm, D), lambda b, i: (b, i, 0)),
        out_shape=jax.ShapeDtypeStruct((B, S, D), F32),
        compiler_params=_params("arbitrary", "arbitrary"),
        name="xattn",
    )(sb_o, sw_o, g_sb, g_sw, w_out, x3, norm_x, wq, q_gain, kmem, vmem, wo)


def _swiglu_step(h, wg_ref, wu_ref, wd_ref):
    a = _dot(h, wg_ref[...])
    u = _dot(h, wu_ref[...])
    act = a / (1.0 + jnp.exp(-a)) * u
    return _dot(act.astype(BF16), wd_ref[...])


def _ffn_kernel(x_ref, g_ref, wg_ref, wu_ref, wd_ref, o_ref, h_ref):
    @pl.when(pl.program_id(1) == 0)
    def _():
        x = x_ref[...]
        h_ref[...] = _rms(x, g_ref[...]).astype(BF16)
        o_ref[...] = x

    o_ref[...] += _swiglu_step(h_ref[...], wg_ref, wu_ref, wd_ref)


def _ffn_dense(x2, g, wg, wu, wd, l, i_dense):
    T, D = x2.shape
    F = wg.shape[-1]
    tm, tf = min(FFN_TOKEN_TILE, T), FFN_COL_TILE
    return pl.pallas_call(
        _ffn_kernel,
        grid=(T // tm, F // tf),
        in_specs=[
            pl.BlockSpec((tm, D), lambda i, f: (i, 0)),
            pl.BlockSpec((None, 1, D), lambda i, f: (l, 0, 0)),
            pl.BlockSpec((None, D, tf), lambda i, f: (i_dense, 0, f)),
            pl.BlockSpec((None, D, tf), lambda i, f: (i_dense, 0, f)),
            pl.BlockSpec((None, tf, D), lambda i, f: (i_dense, f, 0)),
        ],
        out_specs=pl.BlockSpec((tm, D), lambda i, f: (i, 0)),
        out_shape=jax.ShapeDtypeStruct((T, D), F32),
        scratch_shapes=[pltpu.VMEM((tm, D), BF16)],
        compiler_params=_params("parallel", "arbitrary"),
        name="ffn_dense",
    )(x2, g, wg, wu, wd)


def _moe_ffn_kernel(be_ref, nu_ref, xs_ref, wg_ref, wu_ref, wd_ref, o_ref, h_ref):
    del be_ref
    used = pl.program_id(0) < nu_ref[0]

    @pl.when(pl.program_id(1) == 0)
    def _():
        o_ref[...] = jnp.zeros_like(o_ref)

    @pl.when(used & (pl.program_id(1) == 0))
    def _():
        h_ref[...] = xs_ref[...].astype(BF16)

    @pl.when(used)
    def _():
        o_ref[...] += _swiglu_step(h_ref[...], wg_ref, wu_ref, wd_ref)


def _moe_ffn(blk_e, n_used, xs, wg, wu, wd, i_moe):
    P, D = xs.shape
    F = wg.shape[-1]
    tm, tf = MOE_ROW_TILE, FFN_COL_TILE
    nf = F // tf
    row_blk = lambda i, nu: jnp.minimum(i, nu[0] - 1)
    col_blk = lambda i, f, nu: jnp.where(i < nu[0], f, nf - 1)
    return pl.pallas_call(
        _moe_ffn_kernel,
        grid_spec=pltpu.PrefetchScalarGridSpec(
            num_scalar_prefetch=2,
            grid=(P // tm, nf),
            in_specs=[
                pl.BlockSpec((tm, D), lambda i, f, be, nu: (row_blk(i, nu), 0)),
                pl.BlockSpec((None, None, D, tf), lambda i, f, be, nu: (i_moe, be[i], 0, col_blk(i, f, nu))),
                pl.BlockSpec((None, None, D, tf), lambda i, f, be, nu: (i_moe, be[i], 0, col_blk(i, f, nu))),
                pl.BlockSpec((None, None, tf, D), lambda i, f, be, nu: (i_moe, be[i], col_blk(i, f, nu), 0)),
            ],
            out_specs=pl.BlockSpec((tm, D), lambda i, f, be, nu: (i, 0)),
            scratch_shapes=[pltpu.VMEM((tm, D), BF16)],
        ),
        out_shape=jax.ShapeDtypeStruct((P, D), F32),
        compiler_params=_params("arbitrary", "arbitrary"),
        name="moe_ffn",
    )(blk_e, n_used, xs, wg, wu, wd)


def _router_kernel(x_ref, g_ref, rw_ref, rb_ref, earlier_ref, hf_ref, ei_ref, gt_ref, cnt_ref, carry_ref):
    tm = x_ref.shape[0]
    rows = rw_ref.shape[0]

    @pl.when(pl.program_id(0) == 0)
    def _():
        carry_ref[...] = jnp.zeros_like(carry_ref)

    hf = _rms(x_ref[...], g_ref[...])
    hf_ref[...] = hf
    h_hi = hf.astype(BF16)
    h_lo = (hf - h_hi.astype(F32)).astype(BF16)
    w = rw_ref[...]
    w_hi = w.astype(BF16)
    w_lo = (w - w_hi.astype(F32)).astype(BF16)
    logits = _dot_nt(w_hi, h_hi) + _dot_nt(w_hi, h_lo) + _dot_nt(w_lo, h_hi) + rb_ref[...]
    row = lax.broadcasted_iota(jnp.int32, (rows, tm), 0)
    logits = jnp.where(row < N_EXPERTS, logits, -jnp.inf)
    m1 = jnp.max(logits, axis=0, keepdims=True)
    i1 = jnp.min(jnp.where(logits == m1, row, rows), axis=0, keepdims=True)
    rest = jnp.where(row == i1, -jnp.inf, logits)
    m2 = jnp.max(rest, axis=0, keepdims=True)
    i2 = jnp.min(jnp.where(rest == m2, row, rows), axis=0, keepdims=True)
    e = jnp.exp(m2 - m1)
    g1 = 1.0 / (1.0 + e)
    g2 = e / (1.0 + e)
    oh1 = row == i1
    oh2 = row == i2
    oh = (oh1 | oh2).astype(F32)
    before = _dot(oh.astype(BF16), earlier_ref[...]) + carry_ref[...]
    r1 = jnp.sum(jnp.where(oh1, before, 0.0), axis=0, keepdims=True).astype(jnp.int32)
    r2 = jnp.sum(jnp.where(oh2, before, 0.0), axis=0, keepdims=True).astype(jnp.int32)
    carry_ref[...] += jnp.sum(oh, axis=1, keepdims=True)
    cnt_ref[...] = jnp.broadcast_to(carry_ref[...], cnt_ref.shape)
    out_row = lax.broadcasted_iota(jnp.int32, (8, tm), 0)
    ei_ref[...] = jnp.where(out_row == 0, i1, jnp.where(out_row == 1, i2, jnp.where(out_row == 2, r1,
                            jnp.where(out_row == 3, r2, 0))))
    gt_ref[...] = jnp.where(out_row == 0, g1, jnp.where(out_row == 1, g2, 0.0))


def _router(x2, g, rw_t, rb_t, l, i_moe):
    T, D = x2.shape
    tm = TOKEN_TILE
    rows = rw_t.shape[1]
    earlier = jnp.asarray(np.arange(tm)[:, None] < np.arange(tm)[None, :], BF16)
    return pl.pallas_call(
        _router_kernel,
        grid=(T // tm,),
        in_specs=[
            pl.BlockSpec((tm, D), lambda i: (i, 0)),
            pl.BlockSpec((None, 1, D), lambda i: (l, 0, 0)),
            pl.BlockSpec((None, rows, D), lambda i: (i_moe, 0, 0)),
            pl.BlockSpec((None, rows, 1), lambda i: (i_moe, 0, 0)),
            pl.BlockSpec((tm, tm), lambda i: (0, 0)),
        ],
        out_specs=[
            pl.BlockSpec((tm, D), lambda i: (i, 0)),
            pl.BlockSpec((8, tm), lambda i: (0, i)),
            pl.BlockSpec((8, tm), lambda i: (0, i)),
            pl.BlockSpec((rows, LANES), lambda i: (0, 0)),
        ],
        out_shape=[
            jax.ShapeDtypeStruct((T, D), F32),
            jax.ShapeDtypeStruct((8, T), jnp.int32),
            jax.ShapeDtypeStruct((8, T), F32),
            jax.ShapeDtypeStruct((rows, LANES), F32),
        ],
        scratch_shapes=[pltpu.VMEM((rows, 1), F32)],
        compiler_params=_params("arbitrary"),
        name="router",
    )(x2, g, rw_t, rb_t, earlier)


def _dispatch_kernel(last_blk_ref, dest_ref, hf_ref, xs_ref, zero_ref, sem):
    tm = hf_ref.shape[0]

    @pl.when(pl.program_id(0) == 0)
    def _():
        zero_ref[...] = jnp.zeros_like(zero_ref)
        blk_rows = zero_ref.shape[0]

        def fill(e):
            start = pl.multiple_of(last_blk_ref[e] * blk_rows, blk_rows)
            return pltpu.make_async_copy(zero_ref, xs_ref.at[pl.ds(start, blk_rows)], sem)

        for e in range(N_EXPERTS):
            fill(e).start()
        for e in range(N_EXPERTS):
            fill(e).wait()

    def row_copy(r, d):
        return pltpu.make_async_copy(hf_ref.at[pl.ds(r, 1)], xs_ref.at[pl.ds(d, 1)], sem)

    def issue(r, _):
        row_copy(r, dest_ref[2 * r]).start()
        row_copy(r, dest_ref[2 * r + 1]).start()
        return 0

    lax.fori_loop(0, tm, issue, 0, unroll=DMA_ISSUE_UNROLL)
    for _ in range(2):
        pltpu.make_async_copy(hf_ref, xs_ref.at[pl.ds(0, tm)], sem).wait()


def _dispatch(last_blk, dest, hf, n_rows):
    T, D = hf.shape
    tm = TOKEN_TILE
    return pl.pallas_call(
        _dispatch_kernel,
        grid_spec=pltpu.PrefetchScalarGridSpec(
            num_scalar_prefetch=1,
            grid=(T // tm,),
            in_specs=[
                pl.BlockSpec((2 * tm,), lambda i, lb: (i,), memory_space=pltpu.SMEM),
                pl.BlockSpec((tm, D), lambda i, lb: (i, 0)),
            ],
            out_specs=pl.BlockSpec(memory_space=pl.ANY),
            scratch_shapes=[pltpu.VMEM((MOE_ROW_TILE, D), F32), pltpu.SemaphoreType.DMA(())],
        ),
        out_shape=jax.ShapeDtypeStruct((n_rows, D), F32),
        compiler_params=_params("arbitrary"),
        name="moe_dispatch",
    )(last_blk, dest, hf)


def _combine_kernel(dest_ref, x_ref, gt_ref, y_ref, o_ref, buf_ref, sem):
    tm = x_ref.shape[0]

    def row_copy(slot, r, d):
        return pltpu.make_async_copy(y_ref.at[pl.ds(d, 1)], buf_ref.at[slot, pl.ds(r, 1)], sem)

    def issue(r, _):
        row_copy(0, r, dest_ref[2 * r]).start()
        row_copy(1, r, dest_ref[2 * r + 1]).start()
        return 0

    lax.fori_loop(0, tm, issue, 0, unroll=DMA_ISSUE_UNROLL)
    for slot in range(2):
        pltpu.make_async_copy(y_ref.at[pl.ds(0, tm)], buf_ref.at[slot], sem).wait()
    gt = gt_ref[...]
    o_ref[...] = x_ref[...] + gt[:, 0:1] * buf_ref[0] + gt[:, 1:2] * buf_ref[1]


def _combine(dest, x2, gates, y):
    T, D = x2.shape
    tm = TOKEN_TILE
    return pl.pallas_call(
        _combine_kernel,
        grid=(T // tm,),
        in_specs=[
            pl.BlockSpec((2 * tm,), lambda i: (i,), memory_space=pltpu.SMEM),
            pl.BlockSpec((tm, D), lambda i: (i, 0)),
            pl.BlockSpec((tm, 8), lambda i: (i, 0)),
            pl.BlockSpec(memory_space=pl.ANY),
        ],
        out_specs=pl.BlockSpec((tm, D), lambda i: (i, 0)),
        out_shape=jax.ShapeDtypeStruct((T, D), F32),
        scratch_shapes=[pltpu.VMEM((2, tm, D), F32), pltpu.SemaphoreType.DMA(())],
        compiler_params=_params("arbitrary"),
        name="moe_combine",
    )(dest, x2, gates, y)


def _moe(x2, g, rw, rb, wg, wu, wd, l, i_moe):
    T, D = x2.shape
    tmm = MOE_ROW_TILE
    P = 2 * T + N_EXPERTS * tmm
    hf, ei_t, gates_t, cnt = _router(x2, g, rw, rb, l, i_moe)
    counts = cnt[:N_EXPERTS, 0].astype(jnp.int32)
    pcounts = (counts + tmm - 1) // tmm * tmm
    pends = jnp.cumsum(pcounts)
    pstarts = pends - pcounts
    experts, ranks = ei_t[0:2].T, ei_t[2:4].T
    gates = gates_t.T
    start_of = sum(jnp.where(experts == e, pstarts[e], 0) for e in range(N_EXPERTS))
    dest = (start_of + ranks).reshape(-1)
    blk_start = jnp.arange(P // tmm, dtype=jnp.int32) * tmm
    blk_e = jnp.minimum(jnp.sum(blk_start[:, None] >= pends[None, :], axis=1), N_EXPERTS - 1).astype(jnp.int32)
    n_used = (pends[N_EXPERTS - 1:] // tmm).astype(jnp.int32)
    last_blk = jnp.maximum(pends // tmm - 1, 0).astype(jnp.int32)
    xs = _dispatch(last_blk, dest, hf, P)
    y = _moe_ffn(blk_e, n_used, xs, wg, wu, wd, i_moe)
    return _combine(dest, x2, gates, y)


def _t5_buckets(dist):
    n = np.maximum(dist, 0)
    max_exact = N_BUCKETS // 2
    large = max_exact + (np.log(np.maximum(n, 1) / max_exact) / np.log(MAX_DISTANCE / max_exact)
                         * (N_BUCKETS - max_exact)).astype(np.int32)
    large = np.minimum(large, N_BUCKETS - 1)
    return np.where(n < max_exact, n, large).astype(np.int32)


def kernel(x, mem, norm_mix, w_in, sb_out_gain, swa_q_gain, swa_k_gain, swa_sinks, swa_out_gain, rel_bias, w_out, norm_xattn, norm_mem, xattn_wq, xattn_wkv, xattn_q_gain, xattn_k_gain, xattn_wo, norm_ffn, dense_w_gate, dense_w_up, dense_w_down, router_w, router_b, exp_w_gate, exp_w_up, exp_w_down):
    B, S, D = x.shape
    depth = w_in.shape[0]
    T = B * S
    row3 = lambda a: a.reshape(a.shape[0], 1, a.shape[1])
    bf = lambda a: a.astype(BF16)

    dist = WINDOW + np.arange(WINDOW)[:, None] - np.arange(2 * WINDOW)[None, :]
    swa_bias = jnp.transpose(rel_bias[_t5_buckets(dist)], (2, 0, 1))
    band = (dist >= 0) & (dist < WINDOW)
    band = np.stack([band & (np.arange(2 * WINDOW)[None, :] >= WINDOW), band])
    swa_bias = jnp.where(band[:, None], swa_bias[None], -jnp.inf)
    expert_rows = 2 * N_EXPERTS
    router_w_p = jnp.pad(jnp.swapaxes(router_w, 1, 2), ((0, 0), (0, expert_rows - N_EXPERTS), (0, 0)))
    router_b_p = jnp.pad(router_b, ((0, 0), (0, expert_rows - N_EXPERTS)))[:, :, None]

    w_in_b, w_out_b = bf(w_in), bf(w_out)
    wq_b, wkv_b, wo_b = bf(xattn_wq), bf(xattn_wkv), bf(xattn_wo)
    dg_b, du_b, dd_b = bf(dense_w_gate), bf(dense_w_up), bf(dense_w_down)
    eg_b, eu_b, ed_b = bf(exp_w_gate), bf(exp_w_up), bf(exp_w_down)
    norm_mix3, norm_x3, norm_f3 = row3(norm_mix), row3(norm_xattn), row3(norm_ffn)
    sb_g3, sw_g3 = row3(sb_out_gain), row3(swa_out_gain)
    xq_g3, xk_g3 = row3(xattn_q_gain), row3(xattn_k_gain)

    kmem, vmem = _memkv(mem, row3(norm_mem), wkv_b, xk_g3)

    x2 = x.reshape(T, D)
    for l in range(depth):
        head_gain = jnp.concatenate([jnp.tile(swa_q_gain[l], SWA_Q_HEADS) * (HEAD_DIM ** -0.5),
                                     jnp.tile(swa_k_gain[l], SWA_KV_HEADS)])[None, :]
        proj = _inproj(x2, norm_mix3, w_in_b, head_gain, l).reshape(B, S, -1)
        sb_o = _sb_attention(proj)
        sw_o = _swa_attention(proj, swa_sinks[l], swa_bias)
        x2 = _xattn(sb_o, sw_o, sb_g3, sw_g3, w_out_b, x2.reshape(B, S, D), norm_x3, wq_b, xq_g3, kmem, vmem,
                    wo_b, l).reshape(T, D)
        if l % 2 == 0:
            x2 = _ffn_dense(x2, norm_f3, dg_b, du_b, dd_b, l, l // 2)
        else:
            x2 = _moe(x2, norm_f3, router_w_p, router_b_p, eg_b, eu_b, ed_b, l, l // 2)
    return x2.reshape(B, S, D)
```

```python
import functools

import numpy as np
import jax
import jax.numpy as jnp
from jax import lax
from jax.experimental import pallas as pl
from jax.experimental.pallas import tpu as pltpu

F32 = jnp.float32
BF16 = jnp.bfloat16

HEAD_DIM = 64
SB_WIDTH = 512
SWA_WIDTH = 512
SWA_Q_HEADS = 8
SWA_GROUP = 4
SWA_KV_HEADS = 2
WINDOW = 128
N_BUCKETS = 32
MAX_DISTANCE = 128
MEM_HEADS = 4
MEM_HEAD_DIM = 128
MEM_WIDTH = 512
N_EXPERTS = 8
EPS = 1e-6
LANES = 128

VMEM_LIMIT = 56 * 1024 * 1024

TOKEN_TILE = 1024
PROJ_TOKEN_TILE = 1024
FFN_TOKEN_TILE = 1024
FFN_COL_TILE = 512
SB_TILE = 256
SWA_BLOCKS_PER_STEP = 2
MOE_ROW_TILE = 1024
DMA_ISSUE_UNROLL = 8
SB_LOG_UNDERFLOW = -104.0


def _params(*sem):
    return pltpu.CompilerParams(dimension_semantics=("arbitrary",) * len(sem), vmem_limit_bytes=VMEM_LIMIT)


def _rms(x, g):
    return x * lax.rsqrt(jnp.mean(x * x, axis=-1, keepdims=True) + EPS) * g


def _dot(a, b):
    return jnp.dot(a, b, preferred_element_type=F32)


def _dot_nt(a, b):
    return lax.dot_general(a, b, (((1,), (1,)), ((), ())), preferred_element_type=F32)


def _group_mean_sq(x, ones_ref):
    sq = x * x
    hi = sq.astype(BF16)
    lo = (sq - hi.astype(F32)).astype(BF16)
    return _dot(hi, ones_ref[...]) + _dot(lo, ones_ref[...])


def _inproj_kernel(x_ref, g_ref, w_ref, hg_ref, ones_q_ref, ones_k_ref, o_ref):
    h = _rms(x_ref[...], g_ref[...]).astype(BF16)
    sb_cols = 3 * SB_WIDTH
    for c in range(3):
        sl = slice(c * SB_WIDTH, (c + 1) * SB_WIDTH)
        o_ref[:, sl] = _dot(h, w_ref[:, sl]).astype(BF16)
    y = _dot(h, w_ref[:, sb_cols:])
    half = SWA_WIDTH // 2
    qk = SWA_WIDTH + SWA_KV_HEADS * HEAD_DIM
    ms = jnp.concatenate([_group_mean_sq(y[:, :half], ones_q_ref),
                          _group_mean_sq(y[:, half:SWA_WIDTH], ones_q_ref),
                          _group_mean_sq(y[:, SWA_WIDTH:qk], ones_k_ref)], axis=1)
    o_ref[:, sb_cols:sb_cols + qk] = (y[:, :qk] * lax.rsqrt(ms + EPS) * hg_ref[...]).astype(BF16)
    o_ref[:, sb_cols + qk:] = y[:, qk:].astype(BF16)


def _head_group_ones(n):
    return jnp.asarray((np.arange(n)[:, None] // HEAD_DIM == np.arange(n)[None, :] // HEAD_DIM) / HEAD_DIM, BF16)


def _inproj(x2, g, w, head_gain, l):
    T, D = x2.shape
    N = w.shape[-1]
    tm = min(PROJ_TOKEN_TILE, T)
    half, kv = SWA_WIDTH // 2, SWA_KV_HEADS * HEAD_DIM
    const = lambda *shape: pl.BlockSpec(shape, lambda i: (0,) * len(shape))
    return pl.pallas_call(
        _inproj_kernel,
        grid=(T // tm,),
        in_specs=[
            pl.BlockSpec((tm, D), lambda i: (i, 0)),
            pl.BlockSpec((None, 1, D), lambda i: (l, 0, 0)),
            pl.BlockSpec((None, D, N), lambda i: (l, 0, 0)),
            const(1, SWA_WIDTH + kv), const(half, half), const(kv, kv),
        ],
        out_specs=pl.BlockSpec((tm, N), lambda i: (i, 0)),
        out_shape=jax.ShapeDtypeStruct((T, N), BF16),
        compiler_params=_params("arbitrary"),
        name="inproj",
    )(x2, g, w, head_gain, _head_group_ones(half), _head_group_ones(kv))


def _sb_kernel(q_ref, k_ref, v_ref, suf_ref, o_ref, kt_ref, z_ref, zn_ref, acc_ref, c_ref, *, tile):
    S = q_ref.shape[0]
    lane = lax.broadcasted_iota(jnp.int32, (1, LANES), 1)
    lo_half = lane < HEAD_DIM
    row = lax.broadcasted_iota(jnp.int32, (tile, tile), 0)
    col = lax.broadcasted_iota(jnp.int32, (tile, tile), 1)
    strict = col < row
    suffix = suf_ref[...]

    for j in range(S // tile):
        kt_ref[j] = k_ref[j * tile:(j + 1) * tile, :].T

    heads = range(2)
    sl = [slice(h * tile, (h + 1) * tile) for h in heads]

    def log_terms(z, masked):
        n = range(len(z))
        log_beta = [jnp.minimum(z[u], 0.0) - jnp.log(1.0 + jnp.exp(-jnp.abs(z[u]))) for u in n]
        log_om = [log_beta[u] - z[u] for u in n]
        log_om = [jnp.where(strict, log_om[u], 0.0) if masked[u] else log_om[u] for u in n]
        hi = [log_om[u].astype(BF16) for u in n]
        lo = [(log_om[u] - hi[u].astype(F32)).astype(BF16) for u in n]
        tail = [_dot(jnp.concatenate([hi[u], lo[u]], axis=0), suffix) for u in n]
        tail = [tail[u][:tile] + tail[u][tile:] for u in n]
        row_sum = [jnp.sum(log_om[u], axis=-1, keepdims=True) for u in n]
        return log_beta, tail, row_sum

    n_q = S // tile

    def key_rows(j):
        return pl.ds(pl.multiple_of(j * tile, tile), tile)

    def head_queries(i):
        q2 = q_ref[key_rows(i), :] * jnp.asarray(HEAD_DIM ** -0.5, BF16)
        zero = jnp.zeros_like(q2)
        return [jnp.where(lo_half, q2, zero), jnp.where(lo_half, zero, q2)]

    def next_tile_scores(i, into):
        if n_q > 1:
            nxt = jnp.minimum(i + 1, n_q - 1)
            qn = head_queries(nxt)
            kt_d, kt_o = kt_ref[nxt], kt_ref[nxt - 1]
            for h in heads:
                zn_ref[into, h] = _dot(qn[h], kt_d)
                zn_ref[into, 2 + h] = _dot(qn[h], kt_o)

    def q_block(i, parity):
        q0 = pl.multiple_of(i * tile, tile)
        qs = head_queries(i)

        def prefetch_scores(j, slot):
            kt = kt_ref[jnp.maximum(j, 0)]
            for h in heads:
                z_ref[slot, sl[h]] = _dot(qs[h], kt)

        def first_block():
            kt = kt_ref[0]
            z = [_dot(qs[h], kt) for h in heads]
            next_tile_scores(i, 1 - parity)
            log_beta, tail, row_sum = log_terms(z, [True, True])
            a = [jnp.where(strict, jnp.exp(log_beta[h] + tail[h]), 0.0) for h in heads]
            v_d = v_ref[0:tile, :]
            for h in heads:
                acc_ref[sl[h]] = _dot(a[h].astype(BF16), v_d)
                c_ref[sl[h]] = row_sum[h]
            return jnp.float32(0.0)

        def later_block():
            z = [zn_ref[parity, u] for u in range(4)]
            next_tile_scores(i, 1 - parity)
            prefetch_scores(i - 2, 0)
            log_beta, tail, row_sum = log_terms(z, [True, True, False, False])
            a_d = [jnp.where(strict, jnp.exp(log_beta[h] + tail[h]), 0.0) for h in heads]
            a_o = [jnp.exp(log_beta[2 + h] + tail[2 + h] + row_sum[h]) for h in heads]
            v_d, v_o = v_ref[key_rows(i), :], v_ref[key_rows(i - 1), :]
            c = [row_sum[h] + row_sum[2 + h] for h in heads]
            for h in heads:
                acc_ref[sl[h]] = _dot(a_d[h].astype(BF16), v_d) + _dot(a_o[h].astype(BF16), v_o)
                c_ref[sl[h]] = c[h]
            return jnp.maximum(jnp.max(c[0]), jnp.max(c[1]))

        c_max = lax.cond(i == 0, first_block, later_block) if parity == 0 else later_block()

        def key_tile(state):
            j, slot, _ = state
            z = [z_ref[slot, sl[h]] for h in heads]
            prefetch_scores(j - 1, 1 - slot)
            log_beta, tail, row_sum = log_terms(z, [False, False])
            c = [c_ref[sl[h]] for h in heads]
            a = [jnp.exp(log_beta[h] + tail[h] + c[h]) for h in heads]
            c = [c[h] + row_sum[h] for h in heads]
            v_j = v_ref[key_rows(j), :]
            for h in heads:
                acc_ref[sl[h]] += _dot(a[h].astype(BF16), v_j)
                c_ref[sl[h]] = c[h]
            return j - 1, 1 - slot, jnp.maximum(jnp.max(c[0]), jnp.max(c[1]))

        lax.while_loop(lambda s: (s[0] >= 0) & (s[2] > SB_LOG_UNDERFLOW), key_tile,
                       (i - 2, jnp.int32(0), c_max))
        o_ref[pl.ds(q0, tile), :] = jnp.where(lo_half, acc_ref[:tile], acc_ref[tile:]).astype(BF16)

    if n_q == 1:
        q_block(0, 0)
    else:
        def q_block_pair(p, _):
            q_block(2 * p, 0)
            q_block(2 * p + 1, 1)
            return 0

        lax.fori_loop(0, n_q // 2, q_block_pair, 0)


def _sb_attention(proj3):
    B, S, _ = proj3.shape
    pairs = SB_WIDTH // LANES
    tile = min(SB_TILE, S)
    assert S % tile == 0 and (S // tile == 1 or (S // tile) % 2 == 0)
    suf = jnp.asarray(np.arange(tile)[:, None] > np.arange(tile)[None, :], BF16)
    spec = lambda off: pl.BlockSpec((None, S, LANES), lambda b, p: (b, 0, off + p))
    return pl.pallas_call(
        functools.partial(_sb_kernel, tile=tile),
        grid=(B, pairs),
        in_specs=[spec(0), spec(pairs), spec(2 * pairs),
                  pl.BlockSpec((tile, tile), lambda b, p: (0, 0))],
        out_specs=pl.BlockSpec((None, S, LANES), lambda b, p: (b, 0, p)),
        out_shape=jax.ShapeDtypeStruct((B, S, SB_WIDTH), BF16),
        scratch_shapes=[pltpu.VMEM((S // tile, LANES, tile), BF16),
                        pltpu.VMEM((2, 2 * tile, tile), F32), pltpu.VMEM((2, 4, tile, tile), F32),
                        pltpu.VMEM((2 * tile, LANES), F32), pltpu.VMEM((2 * tile, 1), F32)],
        compiler_params=_params("parallel", "parallel"),
        name="sb_attention",
    )(proj3, proj3, proj3, suf)


def _swa_kernel(sink_ref, q_ref, kp_ref, kc_ref, vp_ref, vc_ref, sel_ref, bias_ref, o_ref):
    W = WINDOW
    n_blk = q_ref.shape[0] // W
    qn = q_ref[...]
    kn_t = jnp.concatenate([kp_ref[...], kc_ref[...]], axis=0).T
    v = jnp.concatenate([vp_ref[...], vc_ref[...]], axis=0)
    lane = lax.broadcasted_iota(jnp.int32, (1, LANES), 1)
    lo_half = lane < HEAD_DIM
    zero = jnp.zeros((W, LANES), BF16)
    head_row = lax.broadcasted_iota(jnp.int32, (SWA_GROUP * W, 1), 0) // W
    groups = range(SWA_KV_HEADS)
    kt_g = [kn_t[g * HEAD_DIM:(g + 1) * HEAD_DIM, :] for g in groups]
    kt_dup = [jnp.concatenate([kt_g[g], kt_g[g]], axis=0) for g in groups]
    v_dup = [_dot(v, sel_ref[g]).astype(BF16) for g in groups]
    first_slab = jnp.minimum(pl.program_id(1), 1)
    chains = [(blk, g) for blk in range(n_blk) for g in groups]
    lhs = []
    for blk, g in chains:
        rows = []
        for t in range(g * SWA_GROUP // 2, (g + 1) * SWA_GROUP // 2):
            qt = qn[blk * W:(blk + 1) * W, t * LANES:(t + 1) * LANES]
            rows += [jnp.where(lo_half, qt, zero), jnp.where(lo_half, zero, qt)]
        lhs.append(jnp.concatenate(rows, axis=0))
    s = [_dot(lhs[c], kt_dup[g][:, blk * W:(blk + 2) * W]) for c, (blk, g) in enumerate(chains)]
    bias = [bias_ref[first_slab]] + [bias_ref[1]] * (n_blk - 1)
    s = [s[c] + bias[blk][g * SWA_GROUP:(g + 1) * SWA_GROUP].reshape(SWA_GROUP * W, 2 * W)
         for c, (blk, g) in enumerate(chains)]
    sink = []
    for g in groups:
        sk = jnp.zeros((SWA_GROUP * W, 1), F32)
        for hh in range(SWA_GROUP):
            sk = jnp.where(head_row == hh, sink_ref[g * SWA_GROUP + hh], sk)
        sink.append(sk)
    m = [jnp.maximum(jnp.max(s[c], axis=-1, keepdims=True), sink[g]) for c, (blk, g) in enumerate(chains)]
    p = [jnp.exp(s[c] - m[c]) for c in range(len(chains))]
    den = [jnp.sum(p[c], axis=-1, keepdims=True) + jnp.exp(sink[g] - m[c]) for c, (blk, g) in enumerate(chains)]
    r = [_dot(p[c].astype(BF16), v_dup[g][blk * W:(blk + 2) * W]) / den[c]
         for c, (blk, g) in enumerate(chains)]
    for c, (blk, g) in enumerate(chains):
        for tt in range(SWA_GROUP // 2):
            t = g * SWA_GROUP // 2 + tt
            o_ref[blk * W:(blk + 1) * W, t * LANES:(t + 1) * LANES] = jnp.where(
                lo_half, r[c][2 * tt * W:(2 * tt + 1) * W], r[c][(2 * tt + 1) * W:(2 * tt + 2) * W]).astype(BF16)


def _swa_attention(proj3, sinks, bias):
    B, S, _ = proj3.shape
    W = WINDOW
    q_blk = (3 * SB_WIDTH) // SWA_WIDTH
    k_blk = (3 * SB_WIDTH + SWA_WIDTH) // LANES
    v_blk = k_blk + 1
    sel = jnp.asarray(np.stack([np.arange(LANES)[:, None] == g * HEAD_DIM + np.arange(LANES)[None, :] % HEAD_DIM
                                for g in range(SWA_KV_HEADS)]), BF16)
    m = SWA_BLOCKS_PER_STEP
    assert S % (m * W) == 0
    prev = lambda c: pl.BlockSpec((None, W, LANES), lambda b, n: (b, jnp.maximum(m * n - 1, 0), c))
    cur = lambda c: pl.BlockSpec((None, m * W, LANES), lambda b, n: (b, n, c))
    const = lambda *shape: pl.BlockSpec(shape, lambda b, n: (0,) * len(shape))
    return pl.pallas_call(
        _swa_kernel,
        grid=(B, S // (m * W)),
        in_specs=[
            pl.BlockSpec(memory_space=pltpu.SMEM),
            pl.BlockSpec((None, m * W, SWA_WIDTH), lambda b, n: (b, n, q_blk)),
            prev(k_blk), cur(k_blk), prev(v_blk), cur(v_blk),
            const(SWA_KV_HEADS, LANES, LANES),
            const(2, SWA_Q_HEADS, W, 2 * W),
        ],
        out_specs=pl.BlockSpec((None, m * W, SWA_WIDTH), lambda b, n: (b, n, 0)),
        out_shape=jax.ShapeDtypeStruct((B, S, SWA_WIDTH), BF16),
        compiler_params=_params("arbitrary", "arbitrary"),
        name="swa_attention",
    )(sinks, proj3, proj3, proj3, proj3, proj3, sel, bias)


def _outproj(sb_ref, sw_ref, gsb_ref, gsw_ref, w_ref, x):
    a = _rms(sb_ref[...].astype(F32), gsb_ref[...]).astype(BF16)
    b = _rms(sw_ref[...].astype(F32), gsw_ref[...]).astype(BF16)
    return x + _dot(a, w_ref[:SB_WIDTH, :]) + _dot(b, w_ref[SB_WIDTH:, :])


def _memkv_kernel(m_ref, g_ref, w_ref, kg_ref, k_ref, v_ref):
    h = _rms(m_ref[...], g_ref[...]).astype(BF16)
    kv = _dot(h, w_ref[...])
    for hd in range(MEM_HEADS):
        sl = slice(hd * MEM_HEAD_DIM, (hd + 1) * MEM_HEAD_DIM)
        k_ref[:, sl] = _rms(kv[:, sl], kg_ref[...]).astype(BF16)
    v_ref[...] = kv[:, MEM_WIDTH:].astype(BF16)


def _memkv(mem, norm_mem, wkv, k_gain):
    B, N, D = mem.shape
    L = wkv.shape[0]
    out = jax.ShapeDtypeStruct((L, B, N, MEM_WIDTH), BF16)
    return pl.pallas_call(
        _memkv_kernel,
        grid=(L, B),
        in_specs=[
            pl.BlockSpec((None, N, D), lambda l, b: (b, 0, 0)),
            pl.BlockSpec((None, 1, D), lambda l, b: (l, 0, 0)),
            pl.BlockSpec((None, D, 2 * MEM_WIDTH), lambda l, b: (l, 0, 0)),
            pl.BlockSpec((None, 1, MEM_HEAD_DIM), lambda l, b: (l, 0, 0)),
        ],
        out_specs=[pl.BlockSpec((None, None, N, MEM_WIDTH), lambda l, b: (l, b, 0, 0))] * 2,
        out_shape=[out, out],
        compiler_params=_params("parallel", "parallel"),
        name="memkv",
    )(mem, norm_mem, wkv, k_gain)


def _xattn_kernel(sb_ref, sw_ref, gsb_ref, gsw_ref, wout_ref, x_ref, gx_ref, wq_ref, qg_ref, k_ref, v_ref, wo_ref,
                  o_ref):
    x = _outproj(sb_ref, sw_ref, gsb_ref, gsw_ref, wout_ref, x_ref[...])
    h = _rms(x, gx_ref[...]).astype(BF16)
    q = _dot(h, wq_ref[...])
    outs = []
    for hd in range(MEM_HEADS):
        sl = slice(hd * MEM_HEAD_DIM, (hd + 1) * MEM_HEAD_DIM)
        qn = _rms(q[:, sl], qg_ref[...]).astype(BF16)
        s = _dot_nt(qn, k_ref[:, sl]) * (MEM_HEAD_DIM ** -0.5)
        p = jnp.exp(s - jnp.max(s, axis=-1, keepdims=True))
        den = jnp.sum(p, axis=-1, keepdims=True)
        outs.append((_dot(p.astype(BF16), v_ref[:, sl]) / den).astype(BF16))
    o_ref[...] = x + _dot(jnp.concatenate(outs, axis=1), wo_ref[...])


def _xattn(sb_o, sw_o, g_sb, g_sw, w_out, x3, norm_x, wq, q_gain, kmem, vmem, wo, l):
    B, S, D = x3.shape
    N = kmem.shape[2]
    tm = min(PROJ_TOKEN_TILE, S)
    return pl.pallas_call(
        _xattn_kernel,
        grid=(B, S // tm),
        in_specs=[
            pl.BlockSpec((None, tm, SB_WIDTH), lambda b, i: (b, i, 0)),
            pl.BlockSpec((None, tm, SWA_WIDTH), lambda b, i: (b, i, 0)),
            pl.BlockSpec((None, 1, SB_WIDTH), lambda b, i: (l, 0, 0)),
            pl.BlockSpec((None, 1, SWA_WIDTH), lambda b, i: (l, 0, 0)),
            pl.BlockSpec((None, SB_WIDTH + SWA_WIDTH, D), lambda b, i: (l, 0, 0)),
            pl.BlockSpec((None, tm, D), lambda b, i: (b, i, 0)),
            pl.BlockSpec((None, 1, D), lambda b, i: (l, 0, 0)),
            pl.BlockSpec((None, D, MEM_WIDTH), lambda b, i: (l, 0, 0)),
            pl.BlockSpec((None, 1, MEM_HEAD_DIM), lambda b, i: (l, 0, 0)),
            pl.BlockSpec((None, None, N, MEM_WIDTH), lambda b, i: (l, b, 0, 0)),
            pl.BlockSpec((None, None, N, MEM_WIDTH), lambda b, i: (l, b, 0, 0)),
            pl.BlockSpec((None, MEM_WIDTH, D), lambda b, i: (l, 0, 0)),
        ],
        out_specs=pl.BlockSpec((None, tm, D), lambda b, i: (b, i, 0)),
        out_shape=jax.ShapeDtypeStruct((B, S, D), F32),
        compiler_params=_params("arbitrary", "arbitrary"),
        name="xattn",
    )(sb_o, sw_o, g_sb, g_sw, w_out, x3, norm_x, wq, q_gain, kmem, vmem, wo)


def _swiglu_step(h, wg_ref, wu_ref, wd_ref):
    a = _dot(h, wg_ref[...])
    u = _dot(h, wu_ref[...])
    act = a / (1.0 + jnp.exp(-a)) * u
    return _dot(act.astype(BF16), wd_ref[...])


def _ffn_kernel(x_ref, g_ref, wg_ref, wu_ref, wd_ref, o_ref, h_ref):
    @pl.when(pl.program_id(1) == 0)
    def _():
        x = x_ref[...]
        h_ref[...] = _rms(x, g_ref[...]).astype(BF16)
        o_ref[...] = x

    o_ref[...] += _swiglu_step(h_ref[...], wg_ref, wu_ref, wd_ref)


def _ffn_dense(x2, g, wg, wu, wd, l, i_dense):
    T, D = x2.shape
    F = wg.shape[-1]
    tm, tf = min(FFN_TOKEN_TILE, T), FFN_COL_TILE
    return pl.pallas_call(
        _ffn_kernel,
        grid=(T // tm, F // tf),
        in_specs=[
            pl.BlockSpec((tm, D), lambda i, f: (i, 0)),
            pl.BlockSpec((None, 1, D), lambda i, f: (l, 0, 0)),
            pl.BlockSpec((None, D, tf), lambda i, f: (i_dense, 0, f)),
            pl.BlockSpec((None, D, tf), lambda i, f: (i_dense, 0, f)),
            pl.BlockSpec((None, tf, D), lambda i, f: (i_dense, f, 0)),
        ],
        out_specs=pl.BlockSpec((tm, D), lambda i, f: (i, 0)),
        out_shape=jax.ShapeDtypeStruct((T, D), F32),
        scratch_shapes=[pltpu.VMEM((tm, D), BF16)],
        compiler_params=_params("parallel", "arbitrary"),
        name="ffn_dense",
    )(x2, g, wg, wu, wd)


def _moe_ffn_kernel(be_ref, nu_ref, xs_ref, wg_ref, wu_ref, wd_ref, o_ref, h_ref):
    del be_ref
    used = pl.program_id(0) < nu_ref[0]

    @pl.when(pl.program_id(1) == 0)
    def _():
        o_ref[...] = jnp.zeros_like(o_ref)

    @pl.when(used & (pl.program_id(1) == 0))
    def _():
        h_ref[...] = xs_ref[...].astype(BF16)

    @pl.when(used)
    def _():
        o_ref[...] += _swiglu_step(h_ref[...], wg_ref, wu_ref, wd_ref)


def _moe_ffn(blk_e, n_used, xs, wg, wu, wd, i_moe):
    P, D = xs.shape
    F = wg.shape[-1]
    tm, tf = MOE_ROW_TILE, FFN_COL_TILE
    nf = F // tf
    row_blk = lambda i, nu: jnp.minimum(i, nu[0] - 1)
    col_blk = lambda i, f, nu: jnp.where(i < nu[0], f, nf - 1)
    return pl.pallas_call(
        _moe_ffn_kernel,
        grid_spec=pltpu.PrefetchScalarGridSpec(
            num_scalar_prefetch=2,
            grid=(P // tm, nf),
            in_specs=[
                pl.BlockSpec((tm, D), lambda i, f, be, nu: (row_blk(i, nu), 0)),
                pl.BlockSpec((None, None, D, tf), lambda i, f, be, nu: (i_moe, be[i], 0, col_blk(i, f, nu))),
                pl.BlockSpec((None, None, D, tf), lambda i, f, be, nu: (i_moe, be[i], 0, col_blk(i, f, nu))),
                pl.BlockSpec((None, None, tf, D), lambda i, f, be, nu: (i_moe, be[i], col_blk(i, f, nu), 0)),
            ],
            out_specs=pl.BlockSpec((tm, D), lambda i, f, be, nu: (i, 0)),
            scratch_shapes=[pltpu.VMEM((tm, D), BF16)],
        ),
        out_shape=jax.ShapeDtypeStruct((P, D), F32),
        compiler_params=_params("arbitrary", "arbitrary"),
        name="moe_ffn",
    )(blk_e, n_used, xs, wg, wu, wd)


def _router_kernel(x_ref, g_ref, rw_ref, rb_ref, earlier_ref, hf_ref, ei_ref, gt_ref, cnt_ref, carry_ref):
    tm = x_ref.shape[0]
    rows = rw_ref.shape[0]

    @pl.when(pl.program_id(0) == 0)
    def _():
        carry_ref[...] = jnp.zeros_like(carry_ref)

    hf = _rms(x_ref[...], g_ref[...])
    hf_ref[...] = hf
    h_hi = hf.astype(BF16)
    h_lo = (hf - h_hi.astype(F32)).astype(BF16)
    w = rw_ref[...]
    w_hi = w.astype(BF16)
    w_lo = (w - w_hi.astype(F32)).astype(BF16)
    logits = _dot_nt(w_hi, h_hi) + _dot_nt(w_hi, h_lo) + _dot_nt(w_lo, h_hi) + rb_ref[...]
    row = lax.broadcasted_iota(jnp.int32, (rows, tm), 0)
    logits = jnp.where(row < N_EXPERTS, logits, -jnp.inf)
    m1 = jnp.max(logits, axis=0, keepdims=True)
    i1 = jnp.min(jnp.where(logits == m1, row, rows), axis=0, keepdims=True)
    rest = jnp.where(row == i1, -jnp.inf, logits)
    m2 = jnp.max(rest, axis=0, keepdims=True)
    i2 = jnp.min(jnp.where(rest == m2, row, rows), axis=0, keepdims=True)
    e = jnp.exp(m2 - m1)
    g1 = 1.0 / (1.0 + e)
    g2 = e / (1.0 + e)
    oh1 = row == i1
    oh2 = row == i2
    oh = (oh1 | oh2).astype(F32)
    before = _dot(oh.astype(BF16), earlier_ref[...]) + carry_ref[...]
    r1 = jnp.sum(jnp.where(oh1, before, 0.0), axis=0, keepdims=True).astype(jnp.int32)
    r2 = jnp.sum(jnp.where(oh2, before, 0.0), axis=0, keepdims=True).astype(jnp.int32)
    carry_ref[...] += jnp.sum(oh, axis=1, keepdims=True)
    cnt_ref[...] = jnp.broadcast_to(carry_ref[...], cnt_ref.shape)
    out_row = lax.broadcasted_iota(jnp.int32, (8, tm), 0)
    ei_ref[...] = jnp.where(out_row == 0, i1, jnp.where(out_row == 1, i2, jnp.where(out_row == 2, r1,
                            jnp.where(out_row == 3, r2, 0))))
    gt_ref[...] = jnp.where(out_row == 0, g1, jnp.where(out_row == 1, g2, 0.0))


def _router(x2, g, rw_t, rb_t, l, i_moe):
    T, D = x2.shape
    tm = TOKEN_TILE
    rows = rw_t.shape[1]
    earlier = jnp.asarray(np.arange(tm)[:, None] < np.arange(tm)[None, :], BF16)
    return pl.pallas_call(
        _router_kernel,
        grid=(T // tm,),
        in_specs=[
            pl.BlockSpec((tm, D), lambda i: (i, 0)),
            pl.BlockSpec((None, 1, D), lambda i: (l, 0, 0)),
            pl.BlockSpec((None, rows, D), lambda i: (i_moe, 0, 0)),
            pl.BlockSpec((None, rows, 1), lambda i: (i_moe, 0, 0)),
            pl.BlockSpec((tm, tm), lambda i: (0, 0)),
        ],
        out_specs=[
            pl.BlockSpec((tm, D), lambda i: (i, 0)),
            pl.BlockSpec((8, tm), lambda i: (0, i)),
            pl.BlockSpec((8, tm), lambda i: (0, i)),
            pl.BlockSpec((rows, LANES), lambda i: (0, 0)),
        ],
        out_shape=[
            jax.ShapeDtypeStruct((T, D), F32),
            jax.ShapeDtypeStruct((8, T), jnp.int32),
            jax.ShapeDtypeStruct((8, T), F32),
            jax.ShapeDtypeStruct((rows, LANES), F32),
        ],
        scratch_shapes=[pltpu.VMEM((rows, 1), F32)],
        compiler_params=_params("arbitrary"),
        name="router",
    )(x2, g, rw_t, rb_t, earlier)


def _dispatch_kernel(last_blk_ref, dest_ref, hf_ref, xs_ref, zero_ref, sem):
    tm = hf_ref.shape[0]

    @pl.when(pl.program_id(0) == 0)
    def _():
        zero_ref[...] = jnp.zeros_like(zero_ref)
        blk_rows = zero_ref.shape[0]

        def fill(e):
            start = pl.multiple_of(last_blk_ref[e] * blk_rows, blk_rows)
            return pltpu.make_async_copy(zero_ref, xs_ref.at[pl.ds(start, blk_rows)], sem)

        for e in range(N_EXPERTS):
            fill(e).start()
        for e in range(N_EXPERTS):
            fill(e).wait()

    def row_copy(r, d):
        return pltpu.make_async_copy(hf_ref.at[pl.ds(r, 1)], xs_ref.at[pl.ds(d, 1)], sem)

    def issue(r, _):
        row_copy(r, dest_ref[2 * r]).start()
        row_copy(r, dest_ref[2 * r + 1]).start()
        return 0

    lax.fori_loop(0, tm, issue, 0, unroll=DMA_ISSUE_UNROLL)
    for _ in range(2):
        pltpu.make_async_copy(hf_ref, xs_ref.at[pl.ds(0, tm)], sem).wait()


def _dispatch(last_blk, dest, hf, n_rows):
    T, D = hf.shape
    tm = TOKEN_TILE
    return pl.pallas_call(
        _dispatch_kernel,
        grid_spec=pltpu.PrefetchScalarGridSpec(
            num_scalar_prefetch=1,
            grid=(T // tm,),
            in_specs=[
                pl.BlockSpec((2 * tm,), lambda i, lb: (i,), memory_space=pltpu.SMEM),
                pl.BlockSpec((tm, D), lambda i, lb: (i, 0)),
            ],
            out_specs=pl.BlockSpec(memory_space=pl.ANY),
            scratch_shapes=[pltpu.VMEM((MOE_ROW_TILE, D), F32), pltpu.SemaphoreType.DMA(())],
        ),
        out_shape=jax.ShapeDtypeStruct((n_rows, D), F32),
        compiler_params=_params("arbitrary"),
        name="moe_dispatch",
    )(last_blk, dest, hf)


def _combine_kernel(dest_ref, x_ref, gt_ref, y_ref, o_ref, buf_ref, sem):
    tm = x_ref.shape[0]

    def row_copy(slot, r, d):
        return pltpu.make_async_copy(y_ref.at[pl.ds(d, 1)], buf_ref.at[slot, pl.ds(r, 1)], sem)

    def issue(r, _):
        row_copy(0, r, dest_ref[2 * r]).start()
        row_copy(1, r, dest_ref[2 * r + 1]).start()
        return 0

    lax.fori_loop(0, tm, issue, 0, unroll=DMA_ISSUE_UNROLL)
    for slot in range(2):
        pltpu.make_async_copy(y_ref.at[pl.ds(0, tm)], buf_ref.at[slot], sem).wait()
    gt = gt_ref[...]
    o_ref[...] = x_ref[...] + gt[:, 0:1] * buf_ref[0] + gt[:, 1:2] * buf_ref[1]


def _combine(dest, x2, gates, y):
    T, D = x2.shape
    tm = TOKEN_TILE
    return pl.pallas_call(
        _combine_kernel,
        grid=(T // tm,),
        in_specs=[
            pl.BlockSpec((2 * tm,), lambda i: (i,), memory_space=pltpu.SMEM),
            pl.BlockSpec((tm, D), lambda i: (i, 0)),
            pl.BlockSpec((tm, 8), lambda i: (i, 0)),
            pl.BlockSpec(memory_space=pl.ANY),
        ],
        out_specs=pl.BlockSpec((tm, D), lambda i: (i, 0)),
        out_shape=jax.ShapeDtypeStruct((T, D), F32),
        scratch_shapes=[pltpu.VMEM((2, tm, D), F32), pltpu.SemaphoreType.DMA(())],
        compiler_params=_params("arbitrary"),
        name="moe_combine",
    )(dest, x2, gates, y)


def _moe(x2, g, rw, rb, wg, wu, wd, l, i_moe):
    T, D = x2.shape
    tmm = MOE_ROW_TILE
    P = 2 * T + N_EXPERTS * tmm
    hf, ei_t, gates_t, cnt = _router(x2, g, rw, rb, l, i_moe)
    counts = cnt[:N_EXPERTS, 0].astype(jnp.int32)
    pcounts = (counts + tmm - 1) // tmm * tmm
    pends = jnp.cumsum(pcounts)
    pstarts = pends - pcounts
    experts, ranks = ei_t[0:2].T, ei_t[2:4].T
    gates = gates_t.T
    start_of = sum(jnp.where(experts == e, pstarts[e], 0) for e in range(N_EXPERTS))
    dest = (start_of + ranks).reshape(-1)
    blk_start = jnp.arange(P // tmm, dtype=jnp.int32) * tmm
    blk_e = jnp.minimum(jnp.sum(blk_start[:, None] >= pends[None, :], axis=1), N_EXPERTS - 1).astype(jnp.int32)
    n_used = (pends[N_EXPERTS - 1:] // tmm).astype(jnp.int32)
    last_blk = jnp.maximum(pends // tmm - 1, 0).astype(jnp.int32)
    xs = _dispatch(last_blk, dest, hf, P)
    y = _moe_ffn(blk_e, n_used, xs, wg, wu, wd, i_moe)
    return _combine(dest, x2, gates, y)


def _t5_buckets(dist):
    n = np.maximum(dist, 0)
    max_exact = N_BUCKETS // 2
    large = max_exact + (np.log(np.maximum(n, 1) / max_exact) / np.log(MAX_DISTANCE / max_exact)
                         * (N_BUCKETS - max_exact)).astype(np.int32)
    large = np.minimum(large, N_BUCKETS - 1)
    return np.where(n < max_exact, n, large).astype(np.int32)


def kernel(x, mem, norm_mix, w_in, sb_out_gain, swa_q_gain, swa_k_gain, swa_sinks, swa_out_gain, rel_bias, w_out, norm_xattn, norm_mem, xattn_wq, xattn_wkv, xattn_q_gain, xattn_k_gain, xattn_wo, norm_ffn, dense_w_gate, dense_w_up, dense_w_down, router_w, router_b, exp_w_gate, exp_w_up, exp_w_down):
    B, S, D = x.shape
    depth = w_in.shape[0]
    T = B * S
    row3 = lambda a: a.reshape(a.shape[0], 1, a.shape[1])
    bf = lambda a: a.astype(BF16)

    dist = WINDOW + np.arange(WINDOW)[:, None] - np.arange(2 * WINDOW)[None, :]
    swa_bias = jnp.transpose(rel_bias[_t5_buckets(dist)], (2, 0, 1))
    band = (dist >= 0) & (dist < WINDOW)
    band = np.stack([band & (np.arange(2 * WINDOW)[None, :] >= WINDOW), band])
    swa_bias = jnp.where(band[:, None], swa_bias[None], -jnp.inf)
    expert_rows = 2 * N_EXPERTS
    router_w_p = jnp.pad(jnp.swapaxes(router_w, 1, 2), ((0, 0), (0, expert_rows - N_EXPERTS), (0, 0)))
    router_b_p = jnp.pad(router_b, ((0, 0), (0, expert_rows - N_EXPERTS)))[:, :, None]

    w_in_b, w_out_b = bf(w_in), bf(w_out)
    wq_b, wkv_b, wo_b = bf(xattn_wq), bf(xattn_wkv), bf(xattn_wo)
    dg_b, du_b, dd_b = bf(dense_w_gate), bf(dense_w_up), bf(dense_w_down)
    eg_b, eu_b, ed_b = bf(exp_w_gate), bf(exp_w_up), bf(exp_w_down)
    norm_mix3, norm_x3, norm_f3 = row3(norm_mix), row3(norm_xattn), row3(norm_ffn)
    sb_g3, sw_g3 = row3(sb_out_gain), row3(swa_out_gain)
    xq_g3, xk_g3 = row3(xattn_q_gain), row3(xattn_k_gain)

    kmem, vmem = _memkv(mem, row3(norm_mem), wkv_b, xk_g3)

    x2 = x.reshape(T, D)
    for l in range(depth):
        head_gain = jnp.concatenate([jnp.tile(swa_q_gain[l], SWA_Q_HEADS) * (HEAD_DIM ** -0.5),
                                     jnp.tile(swa_k_gain[l], SWA_KV_HEADS)])[None, :]
        proj = _inproj(x2, norm_mix3, w_in_b, head_gain, l).reshape(B, S, -1)
        sb_o = _sb_attention(proj)
        sw_o = _swa_attention(proj, swa_sinks[l], swa_bias)
        x2 = _xattn(sb_o, sw_o, sb_g3, sw_g3, w_out_b, x2.reshape(B, S, D), norm_x3, wq_b, xq_g3, kmem, vmem,
                    wo_b, l).reshape(T, D)
        if l % 2 == 0:
            x2 = _ffn_dense(x2, norm_f3, dg_b, du_b, dd_b, l, l // 2)
        else:
            x2 = _moe(x2, norm_f3, router_w_p, router_b_p, eg_b, eu_b, ed_b, l, l // 2)
    return x2.reshape(B, S, D)
```

```python
import functools

import numpy as np
import jax
import jax.numpy as jnp
from jax import lax
from jax.experimental import pallas as pl
from jax.experimental.pallas import tpu as pltpu

F32 = jnp.float32
BF16 = jnp.bfloat16

HEAD_DIM = 64
SB_WIDTH = 512
SWA_WIDTH = 512
SWA_Q_HEADS = 8
SWA_GROUP = 4
SWA_KV_HEADS = 2
WINDOW = 128
N_BUCKETS = 32
MAX_DISTANCE = 128
MEM_HEADS = 4
MEM_HEAD_DIM = 128
MEM_WIDTH = 512
N_EXPERTS = 8
EPS = 1e-6
LANES = 128

VMEM_LIMIT = 56 * 1024 * 1024

TOKEN_TILE = 1024
PROJ_TOKEN_TILE = 1024
FFN_TOKEN_TILE = 1024
FFN_COL_TILE = 512
SB_TILE = 256
SWA_BLOCKS_PER_STEP = 2
MOE_ROW_TILE = 1024
DMA_ISSUE_UNROLL = 8
SB_LOG_UNDERFLOW = -104.0


def _params(*sem):
    return pltpu.CompilerParams(dimension_semantics=("arbitrary",) * len(sem), vmem_limit_bytes=VMEM_LIMIT)


def _rms(x, g):
    return x * lax.rsqrt(jnp.mean(x * x, axis=-1, keepdims=True) + EPS) * g


def _dot(a, b):
    return jnp.dot(a, b, preferred_element_type=F32)


def _dot_nt(a, b):
    return lax.dot_general(a, b, (((1,), (1,)), ((), ())), preferred_element_type=F32)


def _group_mean_sq(x, ones_ref):
    sq = x * x
    hi = sq.astype(BF16)
    lo = (sq - hi.astype(F32)).astype(BF16)
    return _dot(hi, ones_ref[...]) + _dot(lo, ones_ref[...])


def _inproj_kernel(x_ref, g_ref, w_ref, hg_ref, ones_q_ref, ones_k_ref, o_ref):
    h = _rms(x_ref[...], g_ref[...]).astype(BF16)
    sb_cols = 3 * SB_WIDTH
    for c in range(3):
        sl = slice(c * SB_WIDTH, (c + 1) * SB_WIDTH)
        o_ref[:, sl] = _dot(h, w_ref[:, sl]).astype(BF16)
    y = _dot(h, w_ref[:, sb_cols:])
    half = SWA_WIDTH // 2
    qk = SWA_WIDTH + SWA_KV_HEADS * HEAD_DIM
    ms = jnp.concatenate([_group_mean_sq(y[:, :half], ones_q_ref),
                          _group_mean_sq(y[:, half:SWA_WIDTH], ones_q_ref),
                          _group_mean_sq(y[:, SWA_WIDTH:qk], ones_k_ref)], axis=1)
    o_ref[:, sb_cols:sb_cols + qk] = (y[:, :qk] * lax.rsqrt(ms + EPS) * hg_ref[...]).astype(BF16)
    o_ref[:, sb_cols + qk:] = y[:, qk:].astype(BF16)


def _head_group_ones(n):
    return jnp.asarray((np.arange(n)[:, None] // HEAD_DIM == np.arange(n)[None, :] // HEAD_DIM) / HEAD_DIM, BF16)


def _inproj(x2, g, w, head_gain, l):
    T, D = x2.shape
    N = w.shape[-1]
    tm = min(PROJ_TOKEN_TILE, T)
    half, kv = SWA_WIDTH // 2, SWA_KV_HEADS * HEAD_DIM
    const = lambda *shape: pl.BlockSpec(shape, lambda i: (0,) * len(shape))
    return pl.pallas_call(
        _inproj_kernel,
        grid=(T // tm,),
        in_specs=[
            pl.BlockSpec((tm, D), lambda i: (i, 0)),
            pl.BlockSpec((None, 1, D), lambda i: (l, 0, 0)),
            pl.BlockSpec((None, D, N), lambda i: (l, 0, 0)),
            const(1, SWA_WIDTH + kv), const(half, half), const(kv, kv),
        ],
        out_specs=pl.BlockSpec((tm, N), lambda i: (i, 0)),
        out_shape=jax.ShapeDtypeStruct((T, N), BF16),
        compiler_params=_params("arbitrary"),
        name="inproj",
    )(x2, g, w, head_gain, _head_group_ones(half), _head_group_ones(kv))


def _sb_kernel(q_ref, k_ref, v_ref, suf_ref, o_ref, kt_ref, z_ref, zn_ref, acc_ref, c_ref, *, tile):
    S = q_ref.shape[0]
    lane = lax.broadcasted_iota(jnp.int32, (1, LANES), 1)
    lo_half = lane < HEAD_DIM
    row = lax.broadcasted_iota(jnp.int32, (tile, tile), 0)
    col = lax.broadcasted_iota(jnp.int32, (tile, tile), 1)
    strict = col < row
    suffix = suf_ref[...]

    for j in range(S // tile):
        kt_ref[j] = k_ref[j * tile:(j + 1) * tile, :].T

    heads = range(2)
    sl = [slice(h * tile, (h + 1) * tile) for h in heads]

    def log_terms(z, masked):
        n = range(len(z))
        log_beta = [jnp.minimum(z[u], 0.0) - jnp.log(1.0 + jnp.exp(-jnp.abs(z[u]))) for u in n]
        log_om = [log_beta[u] - z[u] for u in n]
        log_om = [jnp.where(strict, log_om[u], 0.0) if masked[u] else log_om[u] for u in n]
        hi = [log_om[u].astype(BF16) for u in n]
        lo = [(log_om[u] - hi[u].astype(F32)).astype(BF16) for u in n]
        tail = [_dot(jnp.concatenate([hi[u], lo[u]], axis=0), suffix) for u in n]
        tail = [tail[u][:tile] + tail[u][tile:] for u in n]
        row_sum = [jnp.sum(log_om[u], axis=-1, keepdims=True) for u in n]
        return log_beta, tail, row_sum

    n_q = S // tile

    def key_rows(j):
        return pl.ds(pl.multiple_of(j * tile, tile), tile)

    def head_queries(i):
        q2 = q_ref[key_rows(i), :] * jnp.asarray(HEAD_DIM ** -0.5, BF16)
        zero = jnp.zeros_like(q2)
        return [jnp.where(lo_half, q2, zero), jnp.where(lo_half, zero, q2)]

    def next_tile_scores(i, into):
        if n_q > 1:
            nxt = jnp.minimum(i + 1, n_q - 1)
            qn = head_queries(nxt)
            kt_d, kt_o = kt_ref[nxt], kt_ref[nxt - 1]
            for h in heads:
                zn_ref[into, h] = _dot(qn[h], kt_d)
                zn_ref[into, 2 + h] = _dot(qn[h], kt_o)

    def q_block(i, parity):
        q0 = pl.multiple_of(i * tile, tile)
        qs = head_queries(i)

        def prefetch_scores(j, slot):
            kt = kt_ref[jnp.maximum(j, 0)]
            for h in heads:
                z_ref[slot, sl[h]] = _dot(qs[h], kt)

        def first_block():
            kt = kt_ref[0]
            z = [_dot(qs[h], kt) for h in heads]
            next_tile_scores(i, 1 - parity)
            log_beta, tail, row_sum = log_terms(z, [True, True])
            a = [jnp.where(strict, jnp.exp(log_beta[h] + tail[h]), 0.0) for h in heads]
            v_d = v_ref[0:tile, :]
            for h in heads:
                acc_ref[sl[h]] = _dot(a[h].astype(BF16), v_d)
                c_ref[sl[h]] = row_sum[h]
            return jnp.float32(0.0)

        def later_block():
            z = [zn_ref[parity, u] for u in range(4)]
            next_tile_scores(i, 1 - parity)
            prefetch_scores(i - 2, 0)
            log_beta, tail, row_sum = log_terms(z, [True, True, False, False])
            a_d = [jnp.where(strict, jnp.exp(log_beta[h] + tail[h]), 0.0) for h in heads]
            a_o = [jnp.exp(log_beta[2 + h] + tail[2 + h] + row_sum[h]) for h in heads]
            v_d, v_o = v_ref[key_rows(i), :], v_ref[key_rows(i - 1), :]
            c = [row_sum[h] + row_sum[2 + h] for h in heads]
            for h in heads:
                acc_ref[sl[h]] = _dot(a_d[h].astype(BF16), v_d) + _dot(a_o[h].astype(BF16), v_o)
                c_ref[sl[h]] = c[h]
            return jnp.maximum(jnp.max(c[0]), jnp.max(c[1]))

        c_max = lax.cond(i == 0, first_block, later_block) if parity == 0 else later_block()

        def key_tile(state):
            j, slot, _ = state
            z = [z_ref[slot, sl[h]] for h in heads]
            prefetch_scores(j - 1, 1 - slot)
            log_beta, tail, row_sum = log_terms(z, [False, False])
            c = [c_ref[sl[h]] for h in heads]
            a = [jnp.exp(log_beta[h] + tail[h] + c[h]) for h in heads]
            c = [c[h] + row_sum[h] for h in heads]
            v_j = v_ref[key_rows(j), :]
            for h in heads:
                acc_ref[sl[h]] += _dot(a[h].astype(BF16), v_j)
                c_ref[sl[h]] = c[h]
            return j - 1, 1 - slot, jnp.maximum(jnp.max(c[0]), jnp.max(c[1]))

        lax.while_loop(lambda s: (s[0] >= 0) & (s[2] > SB_LOG_UNDERFLOW), key_tile,
                       (i - 2, jnp.int32(0), c_max))
        o_ref[pl.ds(q0, tile), :] = jnp.where(lo_half, acc_ref[:tile], acc_ref[tile:]).astype(BF16)

    if n_q == 1:
        q_block(0, 0)
    else:
        def q_block_pair(p, _):
            q_block(2 * p, 0)
            q_block(2 * p + 1, 1)
            return 0

        lax.fori_loop(0, n_q // 2, q_block_pair, 0)


def _sb_attention(proj3):
    B, S, _ = proj3.shape
    pairs = SB_WIDTH // LANES
    tile = min(SB_TILE, S)
    assert S % tile == 0 and (S // tile == 1 or (S // tile) % 2 == 0)
    suf = jnp.asarray(np.arange(tile)[:, None] > np.arange(tile)[None, :], BF16)
    spec = lambda off: pl.BlockSpec((None, S, LANES), lambda b, p: (b, 0, off + p))
    return pl.pallas_call(
        functools.partial(_sb_kernel, tile=tile),
        grid=(B, pairs),
        in_specs=[spec(0), spec(pairs), spec(2 * pairs),
                  pl.BlockSpec((tile, tile), lambda b, p: (0, 0))],
        out_specs=pl.BlockSpec((None, S, LANES), lambda b, p: (b, 0, p)),
        out_shape=jax.ShapeDtypeStruct((B, S, SB_WIDTH), BF16),
        scratch_shapes=[pltpu.VMEM((S // tile, LANES, tile), BF16),
                        pltpu.VMEM((2, 2 * tile, tile), F32), pltpu.VMEM((2, 4, tile, tile), F32),
                        pltpu.VMEM((2 * tile, LANES), F32), pltpu.VMEM((2 * tile, 1), F32)],
        compiler_params=_params("parallel", "parallel"),
        name="sb_attention",
    )(proj3, proj3, proj3, suf)


def _swa_kernel(sink_ref, q_ref, kp_ref, kc_ref, vp_ref, vc_ref, sel_ref, bias_ref, o_ref):
    W = WINDOW
    n_blk = q_ref.shape[0] // W
    qn = q_ref[...]
    kn_t = jnp.concatenate([kp_ref[...], kc_ref[...]], axis=0).T
    v = jnp.concatenate([vp_ref[...], vc_ref[...]], axis=0)
    lane = lax.broadcasted_iota(jnp.int32, (1, LANES), 1)
    lo_half = lane < HEAD_DIM
    zero = jnp.zeros((W, LANES), BF16)
    head_row = lax.broadcasted_iota(jnp.int32, (SWA_GROUP * W, 1), 0) // W
    groups = range(SWA_KV_HEADS)
    kt_g = [kn_t[g * HEAD_DIM:(g + 1) * HEAD_DIM, :] for g in groups]
    kt_dup = [jnp.concatenate([kt_g[g], kt_g[g]], axis=0) for g in groups]
    v_dup = [_dot(v, sel_ref[g]).astype(BF16) for g in groups]
    first_slab = jnp.minimum(pl.program_id(1), 1)
    chains = [(blk, g) for blk in range(n_blk) for g in groups]
    lhs = []
    for blk, g in chains:
        rows = []
        for t in range(g * SWA_GROUP // 2, (g + 1) * SWA_GROUP // 2):
            qt = qn[blk * W:(blk + 1) * W, t * LANES:(t + 1) * LANES]
            rows += [jnp.where(lo_half, qt, zero), jnp.where(lo_half, zero, qt)]
        lhs.append(jnp.concatenate(rows, axis=0))
    s = [_dot(lhs[c], kt_dup[g][:, blk * W:(blk + 2) * W]) for c, (blk, g) in enumerate(chains)]
    bias = [bias_ref[first_slab]] + [bias_ref[1]] * (n_blk - 1)
    s = [s[c] + bias[blk][g * SWA_GROUP:(g + 1) * SWA_GROUP].reshape(SWA_GROUP * W, 2 * W)
         for c, (blk, g) in enumerate(chains)]
    sink = []
    for g in groups:
        sk = jnp.zeros((SWA_GROUP * W, 1), F32)
        for hh in range(SWA_GROUP):
            sk = jnp.where(head_row == hh, sink_ref[g * SWA_GROUP + hh], sk)
        sink.append(sk)
    m = [jnp.maximum(jnp.max(s[c], axis=-1, keepdims=True), sink[g]) for c, (blk, g) in enumerate(chains)]
    p = [jnp.exp(s[c] - m[c]) for c in range(len(chains))]
    den = [jnp.sum(p[c], axis=-1, keepdims=True) + jnp.exp(sink[g] - m[c]) for c, (blk, g) in enumerate(chains)]
    r = [_dot(p[c].astype(BF16), v_dup[g][blk * W:(blk + 2) * W]) / den[c]
         for c, (blk, g) in enumerate(chains)]
    for c, (blk, g) in enumerate(chains):
        for tt in range(SWA_GROUP // 2):
            t = g * SWA_GROUP // 2 + tt
            o_ref[blk * W:(blk + 1) * W, t * LANES:(t + 1) * LANES] = jnp.where(
                lo_half, r[c][2 * tt * W:(2 * tt + 1) * W], r[c][(2 * tt + 1) * W:(2 * tt + 2) * W]).astype(BF16)


def _swa_attention(proj3, sinks, bias):
    B, S, _ = proj3.shape
    W = WINDOW
    q_blk = (3 * SB_WIDTH) // SWA_WIDTH
    k_blk = (3 * SB_WIDTH + SWA_WIDTH) // LANES
    v_blk = k_blk + 1
    sel = jnp.asarray(np.stack([np.arange(LANES)[:, None] == g * HEAD_DIM + np.arange(LANES)[None, :] % HEAD_DIM
                                for g in range(SWA_KV_HEADS)]), BF16)
    m = SWA_BLOCKS_PER_STEP
    assert S % (m * W) == 0
    prev = lambda c: pl.BlockSpec((None, W, LANES), lambda b, n: (b, jnp.maximum(m * n - 1, 0), c))
    cur = lambda c: pl.BlockSpec((None, m * W, LANES), lambda b, n: (b, n, c))
    const = lambda *shape: pl.BlockSpec(shape, lambda b, n: (0,) * len(shape))
    return pl.pallas_call(
        _swa_kernel,
        grid=(B, S // (m * W)),
        in_specs=[
            pl.BlockSpec(memory_space=pltpu.SMEM),
            pl.BlockSpec((None, m * W, SWA_WIDTH), lambda b, n: (b, n, q_blk)),
            prev(k_blk), cur(k_blk), prev(v_blk), cur(v_blk),
            const(SWA_KV_HEADS, LANES, LANES),
            const(2, SWA_Q_HEADS, W, 2 * W),
        ],
        out_specs=pl.BlockSpec((None, m * W, SWA_WIDTH), lambda b, n: (b, n, 0)),
        out_shape=jax.ShapeDtypeStruct((B, S, SWA_WIDTH), BF16),
        compiler_params=_params("arbitrary", "arbitrary"),
        name="swa_attention",
    )(sinks, proj3, proj3, proj3, proj3, proj3, sel, bias)


def _outproj(sb_ref, sw_ref, gsb_ref, gsw_ref, w_ref, x):
    a = _rms(sb_ref[...].astype(F32), gsb_ref[...]).astype(BF16)
    b = _rms(sw_ref[...].astype(F32), gsw_ref[...]).astype(BF16)
    return x + _dot(a, w_ref[:SB_WIDTH, :]) + _dot(b, w_ref[SB_WIDTH:, :])


def _memkv_kernel(m_ref, g_ref, w_ref, kg_ref, k_ref, v_ref):
    h = _rms(m_ref[...], g_ref[...]).astype(BF16)
    kv = _dot(h, w_ref[...])
    for hd in range(MEM_HEADS):
        sl = slice(hd * MEM_HEAD_DIM, (hd + 1) * MEM_HEAD_DIM)
        k_ref[:, sl] = _rms(kv[:, sl], kg_ref[...]).astype(BF16)
    v_ref[...] = kv[:, MEM_WIDTH:].astype(BF16)


def _memkv(mem, norm_mem, wkv, k_gain):
    B, N, D = mem.shape
    L = wkv.shape[0]
    out = jax.ShapeDtypeStruct((L, B, N, MEM_WIDTH), BF16)
    return pl.pallas_call(
        _memkv_kernel,
        grid=(L, B),
        in_specs=[
            pl.BlockSpec((None, N, D), lambda l, b: (b, 0, 0)),
            pl.BlockSpec((None, 1, D), lambda l, b: (l, 0, 0)),
            pl.BlockSpec((None, D, 2 * MEM_WIDTH), lambda l, b: (l, 0, 0)),
            pl.BlockSpec((None, 1, MEM_HEAD_DIM), lambda l, b: (l, 0, 0)),
        ],
        out_specs=[pl.BlockSpec((None, None, N, MEM_WIDTH), lambda l, b: (l, b, 0, 0))] * 2,
        out_shape=[out, out],
        compiler_params=_params("parallel", "parallel"),
        name="memkv",
    )(mem, norm_mem, wkv, k_gain)


def _xattn_kernel(sb_ref, sw_ref, gsb_ref, gsw_ref, wout_ref, x_ref, gx_ref, wq_ref, qg_ref, k_ref, v_ref, wo_ref,
                  o_ref):
    x = _outproj(sb_ref, sw_ref, gsb_ref, gsw_ref, wout_ref, x_ref[...])
    h = _rms(x, gx_ref[...]).astype(BF16)
    q = _dot(h, wq_ref[...])
    outs = []
    for hd in range(MEM_HEADS):
        sl = slice(hd * MEM_HEAD_DIM, (hd + 1) * MEM_HEAD_DIM)
        qn = _rms(q[:, sl], qg_ref[...]).astype(BF16)
        s = _dot_nt(qn, k_ref[:, sl]) * (MEM_HEAD_DIM ** -0.5)
        p = jnp.exp(s - jnp.max(s, axis=-1, keepdims=True))
        den = jnp.sum(p, axis=-1, keepdims=True)
        outs.append((_dot(p.astype(BF16), v_ref[:, sl]) / den).astype(BF16))
    o_ref[...] = x + _dot(jnp.concatenate(outs, axis=1), wo_ref[...])


def _xattn(sb_o, sw_o, g_sb, g_sw, w_out, x3, norm_x, wq, q_gain, kmem, vmem, wo, l):
    B, S, D = x3.shape
    N = kmem.shape[2]
    tm = min(PROJ_TOKEN_TILE, S)
    return pl.pallas_call(
        _xattn_kernel,
        grid=(B, S // tm),
        in_specs=[
            pl.BlockSpec((None, tm, SB_WIDTH), lambda b, i: (b, i, 0)),
            pl.BlockSpec((None, tm, SWA_WIDTH), lambda b, i: (b, i, 0)),
            pl.BlockSpec((None, 1, SB_WIDTH), lambda b, i: (l, 0, 0)),
            pl.BlockSpec((None, 1, SWA_WIDTH), lambda b, i: (l, 0, 0)),
            pl.BlockSpec((None, SB_WIDTH + SWA_WIDTH, D), lambda b, i: (l, 0, 0)),
            pl.BlockSpec((None, tm, D), lambda b, i: (b, i, 0)),
            pl.BlockSpec((None, 1, D), lambda b, i: (l, 0, 0)),
            pl.BlockSpec((None, D, MEM_WIDTH), lambda b, i: (l, 0, 0)),
            pl.BlockSpec((None, 1, MEM_HEAD_DIM), lambda b, i: (l, 0, 0)),
            pl.BlockSpec((None, None, N, MEM_WIDTH), lambda b, i: (l, b, 0, 0)),
            pl.BlockSpec((None, None, N, MEM_WIDTH), lambda b, i: (l, b, 0, 0)),
            pl.BlockSpec((None, MEM_WIDTH, D), lambda b, i: (l, 0, 0)),
        ],
        out_specs=pl.BlockSpec((None, tm, D), lambda b, i: (b, i, 0)),
        out_shape=jax.ShapeDtypeStruct((B, S, D), F32),
        compiler_params=_params("arbitrary", "arbitrary"),
        name="xattn",
    )(sb_o, sw_o, g_sb, g_sw, w_out, x3, norm_x, wq, q_gain, kmem, vmem, wo)


def _swiglu_step(h, wg_ref, wu_ref, wd_ref):
    a = _dot(h, wg_ref[...])
    u = _dot(h, wu_ref[...])
    act = a / (1.0 + jnp.exp(-a)) * u
    return _dot(act.astype(BF16), wd_ref[...])


def _ffn_kernel(x_ref, g_ref, wg_ref, wu_ref, wd_ref, o_ref, h_ref):
    @pl.when(pl.program_id(1) == 0)
    def _():
        x = x_ref[...]
        h_ref[...] = _rms(x, g_ref[...]).astype(BF16)
        o_ref[...] = x

    o_ref[...] += _swiglu_step(h_ref[...], wg_ref, wu_ref, wd_ref)


def _ffn_dense(x2, g, wg, wu, wd, l, i_dense):
    T, D = x2.shape
    F = wg.shape[-1]
    tm, tf = min(FFN_TOKEN_TILE, T), FFN_COL_TILE
    return pl.pallas_call(
        _ffn_kernel,
        grid=(T // tm, F // tf),
        in_specs=[
            pl.BlockSpec((tm, D), lambda i, f: (i, 0)),
            pl.BlockSpec((None, 1, D), lambda i, f: (l, 0, 0)),
            pl.BlockSpec((None, D, tf), lambda i, f: (i_dense, 0, f)),
            pl.BlockSpec((None, D, tf), lambda i, f: (i_dense, 0, f)),
            pl.BlockSpec((None, tf, D), lambda i, f: (i_dense, f, 0)),
        ],
        out_specs=pl.BlockSpec((tm, D), lambda i, f: (i, 0)),
        out_shape=jax.ShapeDtypeStruct((T, D), F32),
        scratch_shapes=[pltpu.VMEM((tm, D), BF16)],
        compiler_params=_params("parallel", "arbitrary"),
        name="ffn_dense",
    )(x2, g, wg, wu, wd)


def _moe_ffn_kernel(be_ref, nu_ref, xs_ref, wg_ref, wu_ref, wd_ref, o_ref, h_ref):
    del be_ref
    used = pl.program_id(0) < nu_ref[0]

    @pl.when(pl.program_id(1) == 0)
    def _():
        o_ref[...] = jnp.zeros_like(o_ref)

    @pl.when(used & (pl.program_id(1) == 0))
    def _():
        h_ref[...] = xs_ref[...].astype(BF16)

    @pl.when(used)
    def _():
        o_ref[...] += _swiglu_step(h_ref[...], wg_ref, wu_ref, wd_ref)


def _moe_ffn(blk_e, n_used, xs, wg, wu, wd, i_moe):
    P, D = xs.shape
    F = wg.shape[-1]
    tm, tf = MOE_ROW_TILE, FFN_COL_TILE
    nf = F // tf
    row_blk = lambda i, nu: jnp.minimum(i, nu[0] - 1)
    col_blk = lambda i, f, nu: jnp.where(i < nu[0], f, nf - 1)
    return pl.pallas_call(
        _moe_ffn_kernel,
        grid_spec=pltpu.PrefetchScalarGridSpec(
            num_scalar_prefetch=2,
            grid=(P // tm, nf),
            in_specs=[
                pl.BlockSpec((tm, D), lambda i, f, be, nu: (row_blk(i, nu), 0)),
                pl.BlockSpec((None, None, D, tf), lambda i, f, be, nu: (i_moe, be[i], 0, col_blk(i, f, nu))),
                pl.BlockSpec((None, None, D, tf), lambda i, f, be, nu: (i_moe, be[i], 0, col_blk(i, f, nu))),
                pl.BlockSpec((None, None, tf, D), lambda i, f, be, nu: (i_moe, be[i], col_blk(i, f, nu), 0)),
            ],
            out_specs=pl.BlockSpec((tm, D), lambda i, f, be, nu: (i, 0)),
            scratch_shapes=[pltpu.VMEM((tm, D), BF16)],
        ),
        out_shape=jax.ShapeDtypeStruct((P, D), F32),
        compiler_params=_params("arbitrary", "arbitrary"),
        name="moe_ffn",
    )(blk_e, n_used, xs, wg, wu, wd)


def _router_kernel(x_ref, g_ref, rw_ref, rb_ref, earlier_ref, hf_ref, ei_ref, gt_ref, cnt_ref, carry_ref):
    tm = x_ref.shape[0]
    rows = rw_ref.shape[0]

    @pl.when(pl.program_id(0) == 0)
    def _():
        carry_ref[...] = jnp.zeros_like(carry_ref)

    hf = _rms(x_ref[...], g_ref[...])
    hf_ref[...] = hf
    h_hi = hf.astype(BF16)
    h_lo = (hf - h_hi.astype(F32)).astype(BF16)
    w = rw_ref[...]
    w_hi = w.astype(BF16)
    w_lo = (w - w_hi.astype(F32)).astype(BF16)
    logits = _dot_nt(w_hi, h_hi) + _dot_nt(w_hi, h_lo) + _dot_nt(w_lo, h_hi) + rb_ref[...]
    row = lax.broadcasted_iota(jnp.int32, (rows, tm), 0)
    logits = jnp.where(row < N_EXPERTS, logits, -jnp.inf)
    m1 = jnp.max(logits, axis=0, keepdims=True)
    i1 = jnp.min(jnp.where(logits == m1, row, rows), axis=0, keepdims=True)
    rest = jnp.where(row == i1, -jnp.inf, logits)
    m2 = jnp.max(rest, axis=0, keepdims=True)
    i2 = jnp.min(jnp.where(rest == m2, row, rows), axis=0, keepdims=True)
    e = jnp.exp(m2 - m1)
    g1 = 1.0 / (1.0 + e)
    g2 = e / (1.0 + e)
    oh1 = row == i1
    oh2 = row == i2
    oh = (oh1 | oh2).astype(F32)
    before = _dot(oh.astype(BF16), earlier_ref[...]) + carry_ref[...]
    r1 = jnp.sum(jnp.where(oh1, before, 0.0), axis=0, keepdims=True).astype(jnp.int32)
    r2 = jnp.sum(jnp.where(oh2, before, 0.0), axis=0, keepdims=True).astype(jnp.int32)
    carry_ref[...] += jnp.sum(oh, axis=1, keepdims=True)
    cnt_ref[...] = jnp.broadcast_to(carry_ref[...], cnt_ref.shape)
    out_row = lax.broadcasted_iota(jnp.int32, (8, tm), 0)
    ei_ref[...] = jnp.where(out_row == 0, i1, jnp.where(out_row == 1, i2, jnp.where(out_row == 2, r1,
                            jnp.where(out_row == 3, r2, 0))))
    gt_ref[...] = jnp.where(out_row == 0, g1, jnp.where(out_row == 1, g2, 0.0))


def _router(x2, g, rw_t, rb_t, l, i_moe):
    T, D = x2.shape
    tm = TOKEN_TILE
    rows = rw_t.shape[1]
    earlier = jnp.asarray(np.arange(tm)[:, None] < np.arange(tm)[None, :], BF16)
    return pl.pallas_call(
        _router_kernel,
        grid=(T // tm,),
        in_specs=[
            pl.BlockSpec((tm, D), lambda i: (i, 0)),
            pl.BlockSpec((None, 1, D), lambda i: (l, 0, 0)),
            pl.BlockSpec((None, rows, D), lambda i: (i_moe, 0, 0)),
            pl.BlockSpec((None, rows, 1), lambda i: (i_moe, 0, 0)),
            pl.BlockSpec((tm, tm), lambda i: (0, 0)),
        ],
        out_specs=[
            pl.BlockSpec((tm, D), lambda i: (i, 0)),
            pl.BlockSpec((8, tm), lambda i: (0, i)),
            pl.BlockSpec((8, tm), lambda i: (0, i)),
            pl.BlockSpec((rows, LANES), lambda i: (0, 0)),
        ],
        out_shape=[
            jax.ShapeDtypeStruct((T, D), F32),
            jax.ShapeDtypeStruct((8, T), jnp.int32),
            jax.ShapeDtypeStruct((8, T), F32),
            jax.ShapeDtypeStruct((rows, LANES), F32),
        ],
        scratch_shapes=[pltpu.VMEM((rows, 1), F32)],
        compiler_params=_params("arbitrary"),
        name="router",
    )(x2, g, rw_t, rb_t, earlier)


def _dispatch_kernel(last_blk_ref, dest_ref, hf_ref, xs_ref, zero_ref, sem):
    tm = hf_ref.shape[0]

    @pl.when(pl.program_id(0) == 0)
    def _():
        zero_ref[...] = jnp.zeros_like(zero_ref)
        blk_rows = zero_ref.shape[0]

        def fill(e):
            start = pl.multiple_of(last_blk_ref[e] * blk_rows, blk_rows)
            return pltpu.make_async_copy(zero_ref, xs_ref.at[pl.ds(start, blk_rows)], sem)

        for e in range(N_EXPERTS):
            fill(e).start()
        for e in range(N_EXPERTS):
            fill(e).wait()

    def row_copy(r, d):
        return pltpu.make_async_copy(hf_ref.at[pl.ds(r, 1)], xs_ref.at[pl.ds(d, 1)], sem)

    def issue(r, _):
        row_copy(r, dest_ref[2 * r]).start()
        row_copy(r, dest_ref[2 * r + 1]).start()
        return 0

    lax.fori_loop(0, tm, issue, 0, unroll=DMA_ISSUE_UNROLL)
    for _ in range(2):
        pltpu.make_async_copy(hf_ref, xs_ref.at[pl.ds(0, tm)], sem).wait()


def _dispatch(last_blk, dest, hf, n_rows):
    T, D = hf.shape
    tm = TOKEN_TILE
    return pl.pallas_call(
        _dispatch_kernel,
        grid_spec=pltpu.PrefetchScalarGridSpec(
            num_scalar_prefetch=1,
            grid=(T // tm,),
            in_specs=[
                pl.BlockSpec((2 * tm,), lambda i, lb: (i,), memory_space=pltpu.SMEM),
                pl.BlockSpec((tm, D), lambda i, lb: (i, 0)),
            ],
            out_specs=pl.BlockSpec(memory_space=pl.ANY),
            scratch_shapes=[pltpu.VMEM((MOE_ROW_TILE, D), F32), pltpu.SemaphoreType.DMA(())],
        ),
        out_shape=jax.ShapeDtypeStruct((n_rows, D), F32),
        compiler_params=_params("arbitrary"),
        name="moe_dispatch",
    )(last_blk, dest, hf)


def _combine_kernel(dest_ref, x_ref, gt_ref, y_ref, o_ref, buf_ref, sem):
    tm = x_ref.shape[0]

    def row_copy(slot, r, d):
        return pltpu.make_async_copy(y_ref.at[pl.ds(d, 1)], buf_ref.at[slot, pl.ds(r, 1)], sem)

    def issue(r, _):
        row_copy(0, r, dest_ref[2 * r]).start()
        row_copy(1, r, dest_ref[2 * r + 1]).start()
        return 0

    lax.fori_loop(0, tm, issue, 0, unroll=DMA_ISSUE_UNROLL)
    for slot in range(2):
        pltpu.make_async_copy(y_ref.at[pl.ds(0, tm)], buf_ref.at[slot], sem).wait()
    gt = gt_ref[...]
    o_ref[...] = x_ref[...] + gt[:, 0:1] * buf_ref[0] + gt[:, 1:2] * buf_ref[1]


def _combine(dest, x2, gates, y):
    T, D = x2.shape
    tm = TOKEN_TILE
    return pl.pallas_call(
        _combine_kernel,
        grid=(T // tm,),
        in_specs=[
            pl.BlockSpec((2 * tm,), lambda i: (i,), memory_space=pltpu.SMEM),
            pl.BlockSpec((tm, D), lambda i: (i, 0)),
            pl.BlockSpec((tm, 8), lambda i: (i, 0)),
            pl.BlockSpec(memory_space=pl.ANY),
        ],
        out_specs=pl.BlockSpec((tm, D), lambda i: (i, 0)),
        out_shape=jax.ShapeDtypeStruct((T, D), F32),
        scratch_shapes=[pltpu.VMEM((2, tm, D), F32), pltpu.SemaphoreType.DMA(())],
        compiler_params=_params("arbitrary"),
        name="moe_combine",
    )(dest, x2, gates, y)


def _moe(x2, g, rw, rb, wg, wu, wd, l, i_moe):
    T, D = x2.shape
    assert T % TOKEN_TILE == 0
    tmm = MOE_ROW_TILE
    P = 2 * T + N_EXPERTS * tmm
    hf, ei_t, gates_t, cnt = _router(x2, g, rw, rb, l, i_moe)
    counts = cnt[:N_EXPERTS, 0].astype(jnp.int32)
    pcounts = (counts + tmm - 1) // tmm * tmm
    pends = jnp.cumsum(pcounts)
    pstarts = pends - pcounts
    experts, ranks = ei_t[0:2].T, ei_t[2:4].T
    gates = gates_t.T
    start_of = sum(jnp.where(experts == e, pstarts[e], 0) for e in range(N_EXPERTS))
    dest = (start_of + ranks).reshape(-1)
    blk_start = jnp.arange(P // tmm, dtype=jnp.int32) * tmm
    blk_e = jnp.minimum(jnp.sum(blk_start[:, None] >= pends[None, :], axis=1), N_EXPERTS - 1).astype(jnp.int32)
    n_used = (pends[N_EXPERTS - 1:] // tmm).astype(jnp.int32)
    last_blk = jnp.maximum(pends // tmm - 1, 0).astype(jnp.int32)
    xs = _dispatch(last_blk, dest, hf, P)
    y = _moe_ffn(blk_e, n_used, xs, wg, wu, wd, i_moe)
    return _combine(dest, x2, gates, y)


def _t5_buckets(dist):
    n = np.maximum(dist, 0)
    max_exact = N_BUCKETS // 2
    large = max_exact + (np.log(np.maximum(n, 1) / max_exact) / np.log(MAX_DISTANCE / max_exact)
                         * (N_BUCKETS - max_exact)).astype(np.int32)
    large = np.minimum(large, N_BUCKETS - 1)
    return np.where(n < max_exact, n, large).astype(np.int32)


def kernel(x, mem, norm_mix, w_in, sb_out_gain, swa_q_gain, swa_k_gain, swa_sinks, swa_out_gain, rel_bias, w_out, norm_xattn, norm_mem, xattn_wq, xattn_wkv, xattn_q_gain, xattn_k_gain, xattn_wo, norm_ffn, dense_w_gate, dense_w_up, dense_w_down, router_w, router_b, exp_w_gate, exp_w_up, exp_w_down):
    B, S, D = x.shape
    depth = w_in.shape[0]
    T = B * S
    row3 = lambda a: a.reshape(a.shape[0], 1, a.shape[1])
    bf = lambda a: a.astype(BF16)

    dist = WINDOW + np.arange(WINDOW)[:, None] - np.arange(2 * WINDOW)[None, :]
    bucket = _t5_buckets(dist)
    swa_bias = sum(jnp.where(bucket[None] == b, rel_bias[b][:, None, None], 0.0)
                   for b in range(N_BUCKETS))
    band = (dist >= 0) & (dist < WINDOW)
    band = np.stack([band & (np.arange(2 * WINDOW)[None, :] >= WINDOW), band])
    swa_bias = jnp.where(band[:, None], swa_bias[None], -jnp.inf)
    expert_rows = 2 * N_EXPERTS
    router_w_p = jnp.pad(jnp.swapaxes(router_w, 1, 2), ((0, 0), (0, expert_rows - N_EXPERTS), (0, 0)))
    router_b_p = jnp.pad(router_b, ((0, 0), (0, expert_rows - N_EXPERTS)))[:, :, None]

    w_in_b, w_out_b = bf(w_in), bf(w_out)
    wq_b, wkv_b, wo_b = bf(xattn_wq), bf(xattn_wkv), bf(xattn_wo)
    dg_b, du_b, dd_b = bf(dense_w_gate), bf(dense_w_up), bf(dense_w_down)
    eg_b, eu_b, ed_b = bf(exp_w_gate), bf(exp_w_up), bf(exp_w_down)
    norm_mix3, norm_x3, norm_f3 = row3(norm_mix), row3(norm_xattn), row3(norm_ffn)
    sb_g3, sw_g3 = row3(sb_out_gain), row3(swa_out_gain)
    xq_g3, xk_g3 = row3(xattn_q_gain), row3(xattn_k_gain)

    kmem, vmem = _memkv(mem, row3(norm_mem), wkv_b, xk_g3)

    x2 = x.reshape(T, D)
    for l in range(depth):
        head_gain = jnp.concatenate([jnp.tile(swa_q_gain[l], SWA_Q_HEADS) * (HEAD_DIM ** -0.5),
                                     jnp.tile(swa_k_gain[l], SWA_KV_HEADS)])[None, :]
        proj = _inproj(x2, norm_mix3, w_in_b, head_gain, l).reshape(B, S, -1)
        sb_o = _sb_attention(proj)
        sw_o = _swa_attention(proj, swa_sinks[l], swa_bias)
        x2 = _xattn(sb_o, sw_o, sb_g3, sw_g3, w_out_b, x2.reshape(B, S, D), norm_x3, wq_b, xq_g3, kmem, vmem,
                    wo_b, l).reshape(T, D)
        if l % 2 == 0:
            x2 = _ffn_dense(x2, norm_f3, dg_b, du_b, dd_b, l, l // 2)
        else:
            x2 = _moe(x2, norm_f3, router_w_p, router_b_p, eg_b, eu_b, ed_b, l, l // 2)
    return x2.reshape(B, S, D)
```

```python
import functools

import numpy as np
import jax
import jax.numpy as jnp
from jax import lax
from jax.experimental import pallas as pl
from jax.experimental.pallas import tpu as pltpu

F32 = jnp.float32
BF16 = jnp.bfloat16

HEAD_DIM = 64
SB_WIDTH = 512
SWA_WIDTH = 512
SWA_Q_HEADS = 8
SWA_GROUP = 4
SWA_KV_HEADS = 2
WINDOW = 128
N_BUCKETS = 32
MAX_DISTANCE = 128
MEM_HEADS = 4
MEM_HEAD_DIM = 128
MEM_WIDTH = 512
N_EXPERTS = 8
EPS = 1e-6
LANES = 128

VMEM_LIMIT = 56 * 1024 * 1024

TOKEN_TILE = 1024
PROJ_TOKEN_TILE = 1024
FFN_TOKEN_TILE = 1024
FFN_COL_TILE = 512
SB_TILE = 256
SWA_BLOCKS_PER_STEP = 2
MOE_ROW_TILE = 1024
DMA_ISSUE_UNROLL = 8
SB_LOG_UNDERFLOW = -104.0


def _params(*sem):
    return pltpu.CompilerParams(dimension_semantics=("arbitrary",) * len(sem), vmem_limit_bytes=VMEM_LIMIT)


def _rms(x, g):
    return x * lax.rsqrt(jnp.mean(x * x, axis=-1, keepdims=True) + EPS) * g


def _dot(a, b):
    return jnp.dot(a, b, preferred_element_type=F32)


def _dot_nt(a, b):
    return lax.dot_general(a, b, (((1,), (1,)), ((), ())), preferred_element_type=F32)


def _group_mean_sq(x, ones_ref):
    sq = x * x
    hi = sq.astype(BF16)
    lo = (sq - hi.astype(F32)).astype(BF16)
    return _dot(hi, ones_ref[...]) + _dot(lo, ones_ref[...])


def _inproj_kernel(x_ref, g_ref, w_ref, hg_ref, ones_q_ref, ones_k_ref, o_ref):
    h = _rms(x_ref[...], g_ref[...]).astype(BF16)
    sb_cols = 3 * SB_WIDTH
    for c in range(3):
        sl = slice(c * SB_WIDTH, (c + 1) * SB_WIDTH)
        o_ref[:, sl] = _dot(h, w_ref[:, sl]).astype(BF16)
    y = _dot(h, w_ref[:, sb_cols:])
    half = SWA_WIDTH // 2
    qk = SWA_WIDTH + SWA_KV_HEADS * HEAD_DIM
    ms = jnp.concatenate([_group_mean_sq(y[:, :half], ones_q_ref),
                          _group_mean_sq(y[:, half:SWA_WIDTH], ones_q_ref),
                          _group_mean_sq(y[:, SWA_WIDTH:qk], ones_k_ref)], axis=1)
    o_ref[:, sb_cols:sb_cols + qk] = (y[:, :qk] * lax.rsqrt(ms + EPS) * hg_ref[...]).astype(BF16)
    o_ref[:, sb_cols + qk:] = y[:, qk:].astype(BF16)


def _head_group_ones(n):
    return jnp.asarray((np.arange(n)[:, None] // HEAD_DIM == np.arange(n)[None, :] // HEAD_DIM) / HEAD_DIM, BF16)


def _inproj(x2, g, w, head_gain, l):
    T, D = x2.shape
    N = w.shape[-1]
    tm = min(PROJ_TOKEN_TILE, T)
    half, kv = SWA_WIDTH // 2, SWA_KV_HEADS * HEAD_DIM
    const = lambda *shape: pl.BlockSpec(shape, lambda i: (0,) * len(shape))
    return pl.pallas_call(
        _inproj_kernel,
        grid=(T // tm,),
        in_specs=[
            pl.BlockSpec((tm, D), lambda i: (i, 0)),
            pl.BlockSpec((None, 1, D), lambda i: (l, 0, 0)),
            pl.BlockSpec((None, D, N), lambda i: (l, 0, 0)),
            const(1, SWA_WIDTH + kv), const(half, half), const(kv, kv),
        ],
        out_specs=pl.BlockSpec((tm, N), lambda i: (i, 0)),
        out_shape=jax.ShapeDtypeStruct((T, N), BF16),
        compiler_params=_params("arbitrary"),
        name="inproj",
    )(x2, g, w, head_gain, _head_group_ones(half), _head_group_ones(kv))


def _sb_kernel(q_ref, k_ref, v_ref, suf_ref, o_ref, kt_ref, z_ref, zn_ref, acc_ref, c_ref, *, tile):
    S = q_ref.shape[0]
    lane = lax.broadcasted_iota(jnp.int32, (1, LANES), 1)
    lo_half = lane < HEAD_DIM
    row = lax.broadcasted_iota(jnp.int32, (tile, tile), 0)
    col = lax.broadcasted_iota(jnp.int32, (tile, tile), 1)
    strict = col < row
    suffix = suf_ref[...]

    for j in range(S // tile):
        kt_ref[j] = k_ref[j * tile:(j + 1) * tile, :].T

    heads = range(2)
    sl = [slice(h * tile, (h + 1) * tile) for h in heads]

    def log_terms(z, masked):
        n = range(len(z))
        log_beta = [jnp.minimum(z[u], 0.0) - jnp.log(1.0 + jnp.exp(-jnp.abs(z[u]))) for u in n]
        log_om = [log_beta[u] - z[u] for u in n]
        log_om = [jnp.where(strict, log_om[u], 0.0) if masked[u] else log_om[u] for u in n]
        hi = [log_om[u].astype(BF16) for u in n]
        lo = [(log_om[u] - hi[u].astype(F32)).astype(BF16) for u in n]
        tail = [_dot(jnp.concatenate([hi[u], lo[u]], axis=0), suffix) for u in n]
        tail = [tail[u][:tile] + tail[u][tile:] for u in n]
        row_sum = [jnp.sum(log_om[u], axis=-1, keepdims=True) for u in n]
        return log_beta, tail, row_sum

    n_q = S // tile

    def key_rows(j):
        return pl.ds(pl.multiple_of(j * tile, tile), tile)

    def head_queries(i):
        q2 = q_ref[key_rows(i), :] * jnp.asarray(HEAD_DIM ** -0.5, BF16)
        zero = jnp.zeros_like(q2)
        return [jnp.where(lo_half, q2, zero), jnp.where(lo_half, zero, q2)]

    def next_tile_scores(i, into):
        if n_q > 1:
            nxt = jnp.minimum(i + 1, n_q - 1)
            qn = head_queries(nxt)
            kt_d, kt_o = kt_ref[nxt], kt_ref[nxt - 1]
            for h in heads:
                zn_ref[into, h] = _dot(qn[h], kt_d)
                zn_ref[into, 2 + h] = _dot(qn[h], kt_o)

    def q_block(i, parity):
        q0 = pl.multiple_of(i * tile, tile)
        qs = head_queries(i)

        def prefetch_scores(j, slot):
            kt = kt_ref[jnp.maximum(j, 0)]
            for h in heads:
                z_ref[slot, sl[h]] = _dot(qs[h], kt)

        def first_block():
            kt = kt_ref[0]
            z = [_dot(qs[h], kt) for h in heads]
            next_tile_scores(i, 1 - parity)
            log_beta, tail, row_sum = log_terms(z, [True, True])
            a = [jnp.where(strict, jnp.exp(log_beta[h] + tail[h]), 0.0) for h in heads]
            v_d = v_ref[0:tile, :]
            for h in heads:
                acc_ref[sl[h]] = _dot(a[h].astype(BF16), v_d)
                c_ref[sl[h]] = row_sum[h]
            return jnp.float32(0.0)

        def later_block():
            z = [zn_ref[parity, u] for u in range(4)]
            next_tile_scores(i, 1 - parity)
            prefetch_scores(i - 2, 0)
            log_beta, tail, row_sum = log_terms(z, [True, True, False, False])
            a_d = [jnp.where(strict, jnp.exp(log_beta[h] + tail[h]), 0.0) for h in heads]
            a_o = [jnp.exp(log_beta[2 + h] + tail[2 + h] + row_sum[h]) for h in heads]
            v_d, v_o = v_ref[key_rows(i), :], v_ref[key_rows(i - 1), :]
            c = [row_sum[h] + row_sum[2 + h] for h in heads]
            for h in heads:
                acc_ref[sl[h]] = _dot(a_d[h].astype(BF16), v_d) + _dot(a_o[h].astype(BF16), v_o)
                c_ref[sl[h]] = c[h]
            return jnp.maximum(jnp.max(c[0]), jnp.max(c[1]))

        c_max = lax.cond(i == 0, first_block, later_block) if parity == 0 else later_block()

        def key_tile(state):
            j, slot, _ = state
            z = [z_ref[slot, sl[h]] for h in heads]
            prefetch_scores(j - 1, 1 - slot)
            log_beta, tail, row_sum = log_terms(z, [False, False])
            c = [c_ref[sl[h]] for h in heads]
            a = [jnp.exp(log_beta[h] + tail[h] + c[h]) for h in heads]
            c = [c[h] + row_sum[h] for h in heads]
            v_j = v_ref[key_rows(j), :]
            for h in heads:
                acc_ref[sl[h]] += _dot(a[h].astype(BF16), v_j)
                c_ref[sl[h]] = c[h]
            return j - 1, 1 - slot, jnp.maximum(jnp.max(c[0]), jnp.max(c[1]))

        lax.while_loop(lambda s: (s[0] >= 0) & (s[2] > SB_LOG_UNDERFLOW), key_tile,
                       (i - 2, jnp.int32(0), c_max))
        o_ref[pl.ds(q0, tile), :] = jnp.where(lo_half, acc_ref[:tile], acc_ref[tile:]).astype(BF16)

    if n_q == 1:
        q_block(0, 0)
    else:
        def q_block_pair(p, _):
            q_block(2 * p, 0)
            q_block(2 * p + 1, 1)
            return 0

        lax.fori_loop(0, n_q // 2, q_block_pair, 0)


def _sb_attention(proj3):
    B, S, _ = proj3.shape
    pairs = SB_WIDTH // LANES
    tile = min(SB_TILE, S)
    assert S % tile == 0 and (S // tile == 1 or (S // tile) % 2 == 0)
    suf = jnp.asarray(np.arange(tile)[:, None] > np.arange(tile)[None, :], BF16)
    spec = lambda off: pl.BlockSpec((None, S, LANES), lambda b, p: (b, 0, off + p))
    return pl.pallas_call(
        functools.partial(_sb_kernel, tile=tile),
        grid=(B, pairs),
        in_specs=[spec(0), spec(pairs), spec(2 * pairs),
                  pl.BlockSpec((tile, tile), lambda b, p: (0, 0))],
        out_specs=pl.BlockSpec((None, S, LANES), lambda b, p: (b, 0, p)),
        out_shape=jax.ShapeDtypeStruct((B, S, SB_WIDTH), BF16),
        scratch_shapes=[pltpu.VMEM((S // tile, LANES, tile), BF16),
                        pltpu.VMEM((2, 2 * tile, tile), F32), pltpu.VMEM((2, 4, tile, tile), F32),
                        pltpu.VMEM((2 * tile, LANES), F32), pltpu.VMEM((2 * tile, 1), F32)],
        compiler_params=_params("parallel", "parallel"),
        name="sb_attention",
    )(proj3, proj3, proj3, suf)


def _swa_kernel(sink_ref, q_ref, kp_ref, kc_ref, vp_ref, vc_ref, sel_ref, bias_ref, o_ref):
    W = WINDOW
    n_blk = q_ref.shape[0] // W
    qn = q_ref[...]
    kn_t = jnp.concatenate([kp_ref[...], kc_ref[...]], axis=0).T
    v = jnp.concatenate([vp_ref[...], vc_ref[...]], axis=0)
    lane = lax.broadcasted_iota(jnp.int32, (1, LANES), 1)
    lo_half = lane < HEAD_DIM
    zero = jnp.zeros((W, LANES), BF16)
    head_row = lax.broadcasted_iota(jnp.int32, (SWA_GROUP * W, 1), 0) // W
    groups = range(SWA_KV_HEADS)
    kt_g = [kn_t[g * HEAD_DIM:(g + 1) * HEAD_DIM, :] for g in groups]
    kt_dup = [jnp.concatenate([kt_g[g], kt_g[g]], axis=0) for g in groups]
    v_dup = [_dot(v, sel_ref[g]).astype(BF16) for g in groups]
    first_slab = jnp.minimum(pl.program_id(1), 1)
    chains = [(blk, g) for blk in range(n_blk) for g in groups]
    lhs = []
    for blk, g in chains:
        rows = []
        for t in range(g * SWA_GROUP // 2, (g + 1) * SWA_GROUP // 2):
            qt = qn[blk * W:(blk + 1) * W, t * LANES:(t + 1) * LANES]
            rows += [jnp.where(lo_half, qt, zero), jnp.where(lo_half, zero, qt)]
        lhs.append(jnp.concatenate(rows, axis=0))
    s = [_dot(lhs[c], kt_dup[g][:, blk * W:(blk + 2) * W]) for c, (blk, g) in enumerate(chains)]
    bias = [bias_ref[first_slab]] + [bias_ref[1]] * (n_blk - 1)
    s = [s[c] + bias[blk][g * SWA_GROUP:(g + 1) * SWA_GROUP].reshape(SWA_GROUP * W, 2 * W)
         for c, (blk, g) in enumerate(chains)]
    sink = []
    for g in groups:
        sk = jnp.zeros((SWA_GROUP * W, 1), F32)
        for hh in range(SWA_GROUP):
            sk = jnp.where(head_row == hh, sink_ref[g * SWA_GROUP + hh], sk)
        sink.append(sk)
    m = [jnp.maximum(jnp.max(s[c], axis=-1, keepdims=True), sink[g]) for c, (blk, g) in enumerate(chains)]
    p = [jnp.exp(s[c] - m[c]) for c in range(len(chains))]
    den = [jnp.sum(p[c], axis=-1, keepdims=True) + jnp.exp(sink[g] - m[c]) for c, (blk, g) in enumerate(chains)]
    r = [_dot(p[c].astype(BF16), v_dup[g][blk * W:(blk + 2) * W]) / den[c]
         for c, (blk, g) in enumerate(chains)]
    for c, (blk, g) in enumerate(chains):
        for tt in range(SWA_GROUP // 2):
            t = g * SWA_GROUP // 2 + tt
            o_ref[blk * W:(blk + 1) * W, t * LANES:(t + 1) * LANES] = jnp.where(
                lo_half, r[c][2 * tt * W:(2 * tt + 1) * W], r[c][(2 * tt + 1) * W:(2 * tt + 2) * W]).astype(BF16)


def _swa_attention(proj3, sinks, bias):
    B, S, _ = proj3.shape
    W = WINDOW
    q_blk = (3 * SB_WIDTH) // SWA_WIDTH
    k_blk = (3 * SB_WIDTH + SWA_WIDTH) // LANES
    v_blk = k_blk + 1
    sel = jnp.asarray(np.stack([np.arange(LANES)[:, None] == g * HEAD_DIM + np.arange(LANES)[None, :] % HEAD_DIM
                                for g in range(SWA_KV_HEADS)]), BF16)
    m = SWA_BLOCKS_PER_STEP
    assert S % (m * W) == 0
    prev = lambda c: pl.BlockSpec((None, W, LANES), lambda b, n: (b, jnp.maximum(m * n - 1, 0), c))
    cur = lambda c: pl.BlockSpec((None, m * W, LANES), lambda b, n: (b, n, c))
    const = lambda *shape: pl.BlockSpec(shape, lambda b, n: (0,) * len(shape))
    return pl.pallas_call(
        _swa_kernel,
        grid=(B, S // (m * W)),
        in_specs=[
            pl.BlockSpec(memory_space=pltpu.SMEM),
            pl.BlockSpec((None, m * W, SWA_WIDTH), lambda b, n: (b, n, q_blk)),
            prev(k_blk), cur(k_blk), prev(v_blk), cur(v_blk),
            const(SWA_KV_HEADS, LANES, LANES),
            const(2, SWA_Q_HEADS, W, 2 * W),
        ],
        out_specs=pl.BlockSpec((None, m * W, SWA_WIDTH), lambda b, n: (b, n, 0)),
        out_shape=jax.ShapeDtypeStruct((B, S, SWA_WIDTH), BF16),
        compiler_params=_params("arbitrary", "arbitrary"),
        name="swa_attention",
    )(sinks, proj3, proj3, proj3, proj3, proj3, sel, bias)


def _outproj(sb_ref, sw_ref, gsb_ref, gsw_ref, w_ref, x):
    a = _rms(sb_ref[...].astype(F32), gsb_ref[...]).astype(BF16)
    b = _rms(sw_ref[...].astype(F32), gsw_ref[...]).astype(BF16)
    return x + _dot(a, w_ref[:SB_WIDTH, :]) + _dot(b, w_ref[SB_WIDTH:, :])


def _memkv_kernel(m_ref, g_ref, w_ref, kg_ref, k_ref, v_ref):
    h = _rms(m_ref[...], g_ref[...]).astype(BF16)
    kv = _dot(h, w_ref[...])
    for hd in range(MEM_HEADS):
        sl = slice(hd * MEM_HEAD_DIM, (hd + 1) * MEM_HEAD_DIM)
        k_ref[:, sl] = _rms(kv[:, sl], kg_ref[...]).astype(BF16)
    v_ref[...] = kv[:, MEM_WIDTH:].astype(BF16)


def _memkv(mem, norm_mem, wkv, k_gain):
    B, N, D = mem.shape
    L = wkv.shape[0]
    out = jax.ShapeDtypeStruct((L, B, N, MEM_WIDTH), BF16)
    return pl.pallas_call(
        _memkv_kernel,
        grid=(L, B),
        in_specs=[
            pl.BlockSpec((None, N, D), lambda l, b: (b, 0, 0)),
            pl.BlockSpec((None, 1, D), lambda l, b: (l, 0, 0)),
            pl.BlockSpec((None, D, 2 * MEM_WIDTH), lambda l, b: (l, 0, 0)),
            pl.BlockSpec((None, 1, MEM_HEAD_DIM), lambda l, b: (l, 0, 0)),
        ],
        out_specs=[pl.BlockSpec((None, None, N, MEM_WIDTH), lambda l, b: (l, b, 0, 0))] * 2,
        out_shape=[out, out],
        compiler_params=_params("parallel", "parallel"),
        name="memkv",
    )(mem, norm_mem, wkv, k_gain)


def _xattn_kernel(sb_ref, sw_ref, gsb_ref, gsw_ref, wout_ref, x_ref, gx_ref, wq_ref, qg_ref, k_ref, v_ref, wo_ref,
                  o_ref):
    x = _outproj(sb_ref, sw_ref, gsb_ref, gsw_ref, wout_ref, x_ref[...])
    h = _rms(x, gx_ref[...]).astype(BF16)
    q = _dot(h, wq_ref[...])
    outs = []
    for hd in range(MEM_HEADS):
        sl = slice(hd * MEM_HEAD_DIM, (hd + 1) * MEM_HEAD_DIM)
        qn = _rms(q[:, sl], qg_ref[...]).astype(BF16)
        s = _dot_nt(qn, k_ref[:, sl]) * (MEM_HEAD_DIM ** -0.5)
        p = jnp.exp(s - jnp.max(s, axis=-1, keepdims=True))
        den = jnp.sum(p, axis=-1, keepdims=True)
        outs.append((_dot(p.astype(BF16), v_ref[:, sl]) / den).astype(BF16))
    o_ref[...] = x + _dot(jnp.concatenate(outs, axis=1), wo_ref[...])


def _xattn(sb_o, sw_o, g_sb, g_sw, w_out, x3, norm_x, wq, q_gain, kmem, vmem, wo, l):
    B, S, D = x3.shape
    N = kmem.shape[2]
    tm = min(PROJ_TOKEN_TILE, S)
    return pl.pallas_call(
        _xattn_kernel,
        grid=(B, S // tm),
        in_specs=[
            pl.BlockSpec((None, tm, SB_WIDTH), lambda b, i: (b, i, 0)),
            pl.BlockSpec((None, tm, SWA_WIDTH), lambda b, i: (b, i, 0)),
            pl.BlockSpec((None, 1, SB_WIDTH), lambda b, i: (l, 0, 0)),
            pl.BlockSpec((None, 1, SWA_WIDTH), lambda b, i: (l, 0, 0)),
            pl.BlockSpec((None, SB_WIDTH + SWA_WIDTH, D), lambda b, i: (l, 0, 0)),
            pl.BlockSpec((None, tm, D), lambda b, i: (b, i, 0)),
            pl.BlockSpec((None, 1, D), lambda b, i: (l, 0, 0)),
            pl.BlockSpec((None, D, MEM_WIDTH), lambda b, i: (l, 0, 0)),
            pl.BlockSpec((None, 1, MEM_HEAD_DIM), lambda b, i: (l, 0, 0)),
            pl.BlockSpec((None, None, N, MEM_WIDTH), lambda b, i: (l, b, 0, 0)),
            pl.BlockSpec((None, None, N, MEM_WIDTH), lambda b, i: (l, b, 0, 0)),
            pl.BlockSpec((None, MEM_WIDTH, D), lambda b, i: (l, 0, 0)),
        ],
        out_specs=pl.BlockSpec((None, tm, D), lambda b, i: (b, i, 0)),
        out_shape=jax.ShapeDtypeStruct((B, S, D), F32),
        compiler_params=_params("arbitrary", "arbitrary"),
        name="xattn",
    )(sb_o, sw_o, g_sb, g_sw, w_out, x3, norm_x, wq, q_gain, kmem, vmem, wo)


def _swiglu_step(h, wg_ref, wu_ref, wd_ref):
    a = _dot(h, wg_ref[...])
    u = _dot(h, wu_ref[...])
    act = a / (1.0 + jnp.exp(-a)) * u
    return _dot(act.astype(BF16), wd_ref[...])


def _ffn_kernel(x_ref, g_ref, wg_ref, wu_ref, wd_ref, o_ref, h_ref):
    @pl.when(pl.program_id(1) == 0)
    def _():
        x = x_ref[...]
        h_ref[...] = _rms(x, g_ref[...]).astype(BF16)
        o_ref[...] = x

    o_ref[...] += _swiglu_step(h_ref[...], wg_ref, wu_ref, wd_ref)


def _ffn_dense(x2, g, wg, wu, wd, l, i_dense):
    T, D = x2.shape
    F = wg.shape[-1]
    tm, tf = min(FFN_TOKEN_TILE, T), FFN_COL_TILE
    return pl.pallas_call(
        _ffn_kernel,
        grid=(T // tm, F // tf),
        in_specs=[
            pl.BlockSpec((tm, D), lambda i, f: (i, 0)),
            pl.BlockSpec((None, 1, D), lambda i, f: (l, 0, 0)),
            pl.BlockSpec((None, D, tf), lambda i, f: (i_dense, 0, f)),
            pl.BlockSpec((None, D, tf), lambda i, f: (i_dense, 0, f)),
            pl.BlockSpec((None, tf, D), lambda i, f: (i_dense, f, 0)),
        ],
        out_specs=pl.BlockSpec((tm, D), lambda i, f: (i, 0)),
        out_shape=jax.ShapeDtypeStruct((T, D), F32),
        scratch_shapes=[pltpu.VMEM((tm, D), BF16)],
        compiler_params=_params("parallel", "arbitrary"),
        name="ffn_dense",
    )(x2, g, wg, wu, wd)


def _moe_ffn_kernel(be_ref, nu_ref, xs_ref, wg_ref, wu_ref, wd_ref, o_ref, h_ref):
    del be_ref
    used = pl.program_id(0) < nu_ref[0]

    @pl.when(pl.program_id(1) == 0)
    def _():
        o_ref[...] = jnp.zeros_like(o_ref)

    @pl.when(used & (pl.program_id(1) == 0))
    def _():
        h_ref[...] = xs_ref[...].astype(BF16)

    @pl.when(used)
    def _():
        o_ref[...] += _swiglu_step(h_ref[...], wg_ref, wu_ref, wd_ref)


def _moe_ffn(blk_e, n_used, xs, wg, wu, wd, i_moe):
    P, D = xs.shape
    F = wg.shape[-1]
    tm, tf = MOE_ROW_TILE, FFN_COL_TILE
    nf = F // tf
    row_blk = lambda i, nu: jnp.minimum(i, nu[0] - 1)
    col_blk = lambda i, f, nu: jnp.where(i < nu[0], f, nf - 1)
    return pl.pallas_call(
        _moe_ffn_kernel,
        grid_spec=pltpu.PrefetchScalarGridSpec(
            num_scalar_prefetch=2,
            grid=(P // tm, nf),
            in_specs=[
                pl.BlockSpec((tm, D), lambda i, f, be, nu: (row_blk(i, nu), 0)),
                pl.BlockSpec((None, None, D, tf), lambda i, f, be, nu: (i_moe, be[i], 0, col_blk(i, f, nu))),
                pl.BlockSpec((None, None, D, tf), lambda i, f, be, nu: (i_moe, be[i], 0, col_blk(i, f, nu))),
                pl.BlockSpec((None, None, tf, D), lambda i, f, be, nu: (i_moe, be[i], col_blk(i, f, nu), 0)),
            ],
            out_specs=pl.BlockSpec((tm, D), lambda i, f, be, nu: (i, 0)),
            scratch_shapes=[pltpu.VMEM((tm, D), BF16)],
        ),
        out_shape=jax.ShapeDtypeStruct((P, D), F32),
        compiler_params=_params("arbitrary", "arbitrary"),
        name="moe_ffn",
    )(blk_e, n_used, xs, wg, wu, wd)


def _router_kernel(x_ref, g_ref, rw_ref, rb_ref, earlier_ref, hf_ref, ei_ref, gt_ref, cnt_ref, carry_ref):
    tm = x_ref.shape[0]
    rows = rw_ref.shape[0]

    @pl.when(pl.program_id(0) == 0)
    def _():
        carry_ref[...] = jnp.zeros_like(carry_ref)

    hf = _rms(x_ref[...], g_ref[...])
    hf_ref[...] = hf
    h_hi = hf.astype(BF16)
    h_lo = (hf - h_hi.astype(F32)).astype(BF16)
    w = rw_ref[...]
    w_hi = w.astype(BF16)
    w_lo = (w - w_hi.astype(F32)).astype(BF16)
    logits = _dot_nt(w_hi, h_hi) + _dot_nt(w_hi, h_lo) + _dot_nt(w_lo, h_hi) + rb_ref[...]
    row = lax.broadcasted_iota(jnp.int32, (rows, tm), 0)
    logits = jnp.where(row < N_EXPERTS, logits, -jnp.inf)
    m1 = jnp.max(logits, axis=0, keepdims=True)
    i1 = jnp.min(jnp.where(logits == m1, row, rows), axis=0, keepdims=True)
    rest = jnp.where(row == i1, -jnp.inf, logits)
    m2 = jnp.max(rest, axis=0, keepdims=True)
    i2 = jnp.min(jnp.where(rest == m2, row, rows), axis=0, keepdims=True)
    e = jnp.exp(m2 - m1)
    g1 = 1.0 / (1.0 + e)
    g2 = e / (1.0 + e)
    oh1 = row == i1
    oh2 = row == i2
    oh = (oh1 | oh2).astype(F32)
    before = _dot(oh.astype(BF16), earlier_ref[...]) + carry_ref[...]
    r1 = jnp.sum(jnp.where(oh1, before, 0.0), axis=0, keepdims=True).astype(jnp.int32)
    r2 = jnp.sum(jnp.where(oh2, before, 0.0), axis=0, keepdims=True).astype(jnp.int32)
    carry_ref[...] += jnp.sum(oh, axis=1, keepdims=True)
    cnt_ref[...] = jnp.broadcast_to(carry_ref[...], cnt_ref.shape)
    out_row = lax.broadcasted_iota(jnp.int32, (8, tm), 0)
    ei_ref[...] = jnp.where(out_row == 0, i1, jnp.where(out_row == 1, i2, jnp.where(out_row == 2, r1,
                            jnp.where(out_row == 3, r2, 0))))
    gt_ref[...] = jnp.where(out_row == 0, g1, jnp.where(out_row == 1, g2, 0.0))


def _router(x2, g, rw_t, rb_t, l, i_moe):
    T, D = x2.shape
    tm = TOKEN_TILE
    rows = rw_t.shape[1]
    earlier = jnp.asarray(np.arange(tm)[:, None] < np.arange(tm)[None, :], BF16)
    return pl.pallas_call(
        _router_kernel,
        grid=(T // tm,),
        in_specs=[
            pl.BlockSpec((tm, D), lambda i: (i, 0)),
            pl.BlockSpec((None, 1, D), lambda i: (l, 0, 0)),
            pl.BlockSpec((None, rows, D), lambda i: (i_moe, 0, 0)),
            pl.BlockSpec((None, rows, 1), lambda i: (i_moe, 0, 0)),
            pl.BlockSpec((tm, tm), lambda i: (0, 0)),
        ],
        out_specs=[
            pl.BlockSpec((tm, D), lambda i: (i, 0)),
            pl.BlockSpec((8, tm), lambda i: (0, i)),
            pl.BlockSpec((8, tm), lambda i: (0, i)),
            pl.BlockSpec((rows, LANES), lambda i: (0, 0)),
        ],
        out_shape=[
            jax.ShapeDtypeStruct((T, D), F32),
            jax.ShapeDtypeStruct((8, T), jnp.int32),
            jax.ShapeDtypeStruct((8, T), F32),
            jax.ShapeDtypeStruct((rows, LANES), F32),
        ],
        scratch_shapes=[pltpu.VMEM((rows, 1), F32)],
        compiler_params=_params("arbitrary"),
        name="router",
    )(x2, g, rw_t, rb_t, earlier)


def _dispatch_kernel(last_blk_ref, dest_ref, hf_ref, xs_ref, zero_ref, sem):
    tm = hf_ref.shape[0]

    @pl.when(pl.program_id(0) == 0)
    def _():
        zero_ref[...] = jnp.zeros_like(zero_ref)
        blk_rows = zero_ref.shape[0]

        def fill(e):
            start = pl.multiple_of(last_blk_ref[e] * blk_rows, blk_rows)
            return pltpu.make_async_copy(zero_ref, xs_ref.at[pl.ds(start, blk_rows)], sem)

        for e in range(N_EXPERTS):
            fill(e).start()
        for e in range(N_EXPERTS):
            fill(e).wait()

    def row_copy(r, d):
        return pltpu.make_async_copy(hf_ref.at[pl.ds(r, 1)], xs_ref.at[pl.ds(d, 1)], sem)

    def issue(r, _):
        row_copy(r, dest_ref[2 * r]).start(priority=0)
        row_copy(r, dest_ref[2 * r + 1]).start(priority=1)
        return 0

    lax.fori_loop(0, tm, issue, 0, unroll=DMA_ISSUE_UNROLL)
    for _ in range(2):
        pltpu.make_async_copy(hf_ref, xs_ref.at[pl.ds(0, tm)], sem).wait()


def _dispatch(last_blk, dest, hf, n_rows):
    T, D = hf.shape
    tm = TOKEN_TILE
    return pl.pallas_call(
        _dispatch_kernel,
        grid_spec=pltpu.PrefetchScalarGridSpec(
            num_scalar_prefetch=1,
            grid=(T // tm,),
            in_specs=[
                pl.BlockSpec((2 * tm,), lambda i, lb: (i,), memory_space=pltpu.SMEM),
                pl.BlockSpec((tm, D), lambda i, lb: (i, 0)),
            ],
            out_specs=pl.BlockSpec(memory_space=pl.ANY),
            scratch_shapes=[pltpu.VMEM((MOE_ROW_TILE, D), F32), pltpu.SemaphoreType.DMA(())],
        ),
        out_shape=jax.ShapeDtypeStruct((n_rows, D), F32),
        compiler_params=_params("arbitrary"),
        name="moe_dispatch",
    )(last_blk, dest, hf)


def _combine_kernel(dest_ref, x_ref, gt_ref, y_ref, o_ref, buf_ref, sem):
    tm = x_ref.shape[0]

    def row_copy(slot, r, d):
        return pltpu.make_async_copy(y_ref.at[pl.ds(d, 1)], buf_ref.at[slot, pl.ds(r, 1)], sem)

    def issue(r, _):
        row_copy(0, r, dest_ref[2 * r]).start(priority=0)
        row_copy(1, r, dest_ref[2 * r + 1]).start(priority=1)
        return 0

    lax.fori_loop(0, tm, issue, 0, unroll=DMA_ISSUE_UNROLL)
    for slot in range(2):
        pltpu.make_async_copy(y_ref.at[pl.ds(0, tm)], buf_ref.at[slot], sem).wait()
    gt = gt_ref[...]
    o_ref[...] = x_ref[...] + gt[:, 0:1] * buf_ref[0] + gt[:, 1:2] * buf_ref[1]


def _combine(dest, x2, gates, y):
    T, D = x2.shape
    tm = TOKEN_TILE
    return pl.pallas_call(
        _combine_kernel,
        grid=(T // tm,),
        in_specs=[
            pl.BlockSpec((2 * tm,), lambda i: (i,), memory_space=pltpu.SMEM),
            pl.BlockSpec((tm, D), lambda i: (i, 0)),
            pl.BlockSpec((tm, 8), lambda i: (i, 0)),
            pl.BlockSpec(memory_space=pl.ANY),
        ],
        out_specs=pl.BlockSpec((tm, D), lambda i: (i, 0)),
        out_shape=jax.ShapeDtypeStruct((T, D), F32),
        scratch_shapes=[pltpu.VMEM((2, tm, D), F32), pltpu.SemaphoreType.DMA(())],
        compiler_params=_params("arbitrary"),
        name="moe_combine",
    )(dest, x2, gates, y)


def _moe(x2, g, rw, rb, wg, wu, wd, l, i_moe):
    T, D = x2.shape
    assert T % TOKEN_TILE == 0
    tmm = MOE_ROW_TILE
    P = 2 * T + N_EXPERTS * tmm
    hf, ei_t, gates_t, cnt = _router(x2, g, rw, rb, l, i_moe)
    counts = cnt[:N_EXPERTS, 0].astype(jnp.int32)
    pcounts = (counts + tmm - 1) // tmm * tmm
    pends = jnp.cumsum(pcounts)
    pstarts = pends - pcounts
    experts, ranks = ei_t[0:2].T, ei_t[2:4].T
    gates = gates_t.T
    start_of = sum(jnp.where(experts == e, pstarts[e], 0) for e in range(N_EXPERTS))
    dest = (start_of + ranks).reshape(-1)
    blk_start = jnp.arange(P // tmm, dtype=jnp.int32) * tmm
    blk_e = jnp.minimum(jnp.sum(blk_start[:, None] >= pends[None, :], axis=1), N_EXPERTS - 1).astype(jnp.int32)
    n_used = (pends[N_EXPERTS - 1:] // tmm).astype(jnp.int32)
    last_blk = jnp.maximum(pends // tmm - 1, 0).astype(jnp.int32)
    xs = _dispatch(last_blk, dest, hf, P)
    y = _moe_ffn(blk_e, n_used, xs, wg, wu, wd, i_moe)
    return _combine(dest, x2, gates, y)


def _t5_buckets(dist):
    n = np.maximum(dist, 0)
    max_exact = N_BUCKETS // 2
    large = max_exact + (np.log(np.maximum(n, 1) / max_exact) / np.log(MAX_DISTANCE / max_exact)
                         * (N_BUCKETS - max_exact)).astype(np.int32)
    large = np.minimum(large, N_BUCKETS - 1)
    return np.where(n < max_exact, n, large).astype(np.int32)


def kernel(x, mem, norm_mix, w_in, sb_out_gain, swa_q_gain, swa_k_gain, swa_sinks, swa_out_gain, rel_bias, w_out, norm_xattn, norm_mem, xattn_wq, xattn_wkv, xattn_q_gain, xattn_k_gain, xattn_wo, norm_ffn, dense_w_gate, dense_w_up, dense_w_down, router_w, router_b, exp_w_gate, exp_w_up, exp_w_down):
    B, S, D = x.shape
    depth = w_in.shape[0]
    T = B * S
    row3 = lambda a: a.reshape(a.shape[0], 1, a.shape[1])
    bf = lambda a: a.astype(BF16)

    dist = WINDOW + np.arange(WINDOW)[:, None] - np.arange(2 * WINDOW)[None, :]
    bucket = _t5_buckets(dist)
    swa_bias = sum(jnp.where(bucket[None] == b, rel_bias[b][:, None, None], 0.0)
                   for b in range(N_BUCKETS))
    band = (dist >= 0) & (dist < WINDOW)
    band = np.stack([band & (np.arange(2 * WINDOW)[None, :] >= WINDOW), band])
    swa_bias = jnp.where(band[:, None], swa_bias[None], -jnp.inf)
    expert_rows = 2 * N_EXPERTS
    router_w_p = jnp.pad(jnp.swapaxes(router_w, 1, 2), ((0, 0), (0, expert_rows - N_EXPERTS), (0, 0)))
    router_b_p = jnp.pad(router_b, ((0, 0), (0, expert_rows - N_EXPERTS)))[:, :, None]

    w_in_b, w_out_b = bf(w_in), bf(w_out)
    wq_b, wkv_b, wo_b = bf(xattn_wq), bf(xattn_wkv), bf(xattn_wo)
    dg_b, du_b, dd_b = bf(dense_w_gate), bf(dense_w_up), bf(dense_w_down)
    eg_b, eu_b, ed_b = bf(exp_w_gate), bf(exp_w_up), bf(exp_w_down)
    norm_mix3, norm_x3, norm_f3 = row3(norm_mix), row3(norm_xattn), row3(norm_ffn)
    sb_g3, sw_g3 = row3(sb_out_gain), row3(swa_out_gain)
    xq_g3, xk_g3 = row3(xattn_q_gain), row3(xattn_k_gain)

    kmem, vmem = _memkv(mem, row3(norm_mem), wkv_b, xk_g3)

    x2 = x.reshape(T, D)
    for l in range(depth):
        head_gain = jnp.concatenate([jnp.tile(swa_q_gain[l], SWA_Q_HEADS) * (HEAD_DIM ** -0.5),
                                     jnp.tile(swa_k_gain[l], SWA_KV_HEADS)])[None, :]
        proj = _inproj(x2, norm_mix3, w_in_b, head_gain, l).reshape(B, S, -1)
        sb_o = _sb_attention(proj)
        sw_o = _swa_attention(proj, swa_sinks[l], swa_bias)
        x2 = _xattn(sb_o, sw_o, sb_g3, sw_g3, w_out_b, x2.reshape(B, S, D), norm_x3, wq_b, xq_g3, kmem, vmem,
                    wo_b, l).reshape(T, D)
        if l % 2 == 0:
            x2 = _ffn_dense(x2, norm_f3, dg_b, du_b, dd_b, l, l // 2)
        else:
            x2 = _moe(x2, norm_f3, router_w_p, router_b_p, eg_b, eu_b, ed_b, l, l // 2)
    return x2.reshape(B, S, D)
```
